```python
import math
import jax, jax.numpy as jnp
from jax import lax
import numpy as np

D_MODEL = 2048
BATCH = 4
SEQ = 2048
DEPTH = 4

N_MIXERS = 3
D_INNER = D_MODEL
DIFF_HEADS = 8
DIFF_QK_DIM = D_INNER // (2 * DIFF_HEADS)
DIFF_V_DIM = 2 * DIFF_QK_DIM
MOBA_HEADS = 16
MOBA_HEAD_DIM = D_INNER // MOBA_HEADS
MOBA_BLOCK = 256
MOBA_TOPK = 3
MOBA_Q_CHUNK = 32
FOX_HEADS = 16
FOX_HEAD_DIM = D_INNER // FOX_HEADS
FORGET_W_SCALE = 0.1
FORGET_B_MIN = 2.0
FORGET_B_MAX = 5.0
REL_BUCKETS = 32
REL_MAX_DIST = 128
REL_HEADS = 16
Q_BLOCK = 128
LN_EPS = 1e-5
RMS_EPS = 1e-5
NEG = -1e30
DN_ALPHA = (2.0 * DEPTH) ** 0.25
DN_BETA = (8.0 * DEPTH) ** -0.25

kernel_name = "hybrid_diff_moba_fox_deepnorm"


def layer_norm(x, g, b):
    xf = x.astype(jnp.float32)
    mu = jnp.mean(xf, axis=-1, keepdims=True)
    var = jnp.mean(jnp.square(xf - mu), axis=-1, keepdims=True)
    return ((xf - mu) * lax.rsqrt(var + LN_EPS) * g + b).astype(x.dtype)


def rms_norm(x, g):
    xf = x.astype(jnp.float32)
    return (xf * lax.rsqrt(jnp.mean(jnp.square(xf), axis=-1, keepdims=True) + RMS_EPS) * g).astype(x.dtype)


def t5_bucket(dist):
    n = jnp.maximum(dist, 0)
    max_exact = REL_BUCKETS // 2
    nf = jnp.maximum(n, 1).astype(jnp.float32)
    large = max_exact + (jnp.log(nf / max_exact) / math.log(REL_MAX_DIST / max_exact)
                         * (REL_BUCKETS - max_exact)).astype(jnp.int32)
    large = jnp.minimum(large, REL_BUCKETS - 1)
    return jnp.where(n < max_exact, n, large)


def rel_bias_dense(table, q_pos, k_pos):
    return jnp.moveaxis(table[t5_bucket(q_pos[:, None] - k_pos[None, :])], -1, 0).astype(jnp.float32)


def split_heads(t, n_heads):
    B, S, _ = t.shape
    return t.reshape(B, S, n_heads, -1).transpose(0, 2, 1, 3)


def merge_blocks(o):
    nb, B, H, Qb, d = o.shape
    return o.transpose(1, 0, 3, 2, 4).reshape(B, nb * Qb, H * d)


def diff_attention(u, w_in, w_out, lam_q1, lam_k1, lam_q2, lam_k2, subln_g, rel_table, layer_idx):
    B, S, _ = u.shape
    E = D_INNER
    proj = u @ w_in
    q = split_heads(proj[..., :E], 2 * DIFF_HEADS)
    k = split_heads(proj[..., E:2 * E], 2 * DIFF_HEADS)
    v = split_heads(proj[..., 2 * E:3 * E], DIFF_HEADS)
    z = proj[..., 3 * E:4 * E]
    lam_init = 0.8 - 0.6 * math.exp(-0.3 * layer_idx)
    lam = (jnp.exp(jnp.sum((lam_q1 * lam_k1).astype(jnp.float32)))
           - jnp.exp(jnp.sum((lam_q2 * lam_k2).astype(jnp.float32))) + lam_init)
    scale = DIFF_QK_DIM ** -0.5
    nb = S // Q_BLOCK
    k_pos = jnp.arange(S)
    q_blocks = q.reshape(B, 2 * DIFF_HEADS, nb, Q_BLOCK, DIFF_QK_DIM).transpose(2, 0, 1, 3, 4)

    def block(args):
        q_blk, start = args
        q_pos = start + jnp.arange(Q_BLOCK)
        logits = (jnp.einsum('bhqd,bhsd->bhqs', q_blk, k).astype(jnp.float32) * scale
                  + rel_bias_dense(rel_table, q_pos, k_pos)[None])
        logits = jnp.where(k_pos[None, :] <= q_pos[:, None], logits, NEG)
        p = jax.nn.softmax(logits, axis=-1).reshape(B, DIFF_HEADS, 2, Q_BLOCK, S)
        a = p[:, :, 0] - lam * p[:, :, 1]
        return jnp.einsum('bhqs,bhsd->bhqd', a.astype(v.dtype), v)

    o = lax.map(block, (q_blocks, jnp.arange(nb) * Q_BLOCK))
    o = rms_norm(o, subln_g) * (1.0 - lam_init)
    o = merge_blocks(o)
    return (o * jax.nn.silu(z)) @ w_out


def moba_attention(u, w_in, w_out, rel_table):
    B, S, _ = u.shape
    E, H, d = D_INNER, MOBA_HEADS, MOBA_HEAD_DIM
    proj = u @ w_in
    q = split_heads(proj[..., :E], H)
    k = split_heads(proj[..., E:2 * E], H)
    v = split_heads(proj[..., 2 * E:3 * E], H)
    z = proj[..., 3 * E:4 * E]
    n_kb = -(-S // MOBA_BLOCK)
    pad = n_kb * MOBA_BLOCK - S
    k_blocks = jnp.pad(k, ((0, 0), (0, 0), (0, pad), (0, 0))).reshape(B, H, n_kb, MOBA_BLOCK, d)
    v_blocks = jnp.pad(v, ((0, 0), (0, 0), (0, pad), (0, 0))).reshape(B, H, n_kb, MOBA_BLOCK, d)
    k_mean = jnp.mean(k_blocks.astype(jnp.float32), axis=3)
    top_k = max(1, min(MOBA_TOPK, n_kb - 1))
    scale = d ** -0.5
    bi = jnp.arange(B)[:, None, None, None]
    hi = jnp.arange(H)[None, :, None, None]
    hi5 = jnp.arange(H)[None, :, None, None, None]
    n_chunks = S // MOBA_Q_CHUNK
    q_chunks = q.reshape(B, H, n_chunks, MOBA_Q_CHUNK, d).transpose(2, 0, 1, 3, 4)
    blk_off = jnp.arange(MOBA_BLOCK)

    def chunk(args):
        q_blk, start = args
        q_pos = start + jnp.arange(MOBA_Q_CHUNK)
        own = start // MOBA_BLOCK
        gate = jnp.einsum('bhqd,bhnd->bhqn', q_blk.astype(jnp.float32), k_mean)
        gate = jnp.where(jnp.arange(n_kb) < own, gate, NEG)
        _, sel = lax.top_k(gate, top_k)
        sel_valid = sel < own
        k_sel = k_blocks[bi, hi, sel]
        v_sel = v_blocks[bi, hi, sel]
        sel_pos = sel[..., None] * MOBA_BLOCK + blk_off
        logit_sel = jnp.einsum('bhqd,bhqkcd->bhqkc', q_blk, k_sel).astype(jnp.float32) * scale
        logit_sel = logit_sel + rel_table[t5_bucket(q_pos[None, None, :, None, None] - sel_pos), hi5].astype(jnp.float32)
        logit_sel = jnp.where(sel_valid[..., None], logit_sel, NEG)
        k_own = lax.dynamic_index_in_dim(k_blocks, own, axis=2, keepdims=False)
        v_own = lax.dynamic_index_in_dim(v_blocks, own, axis=2, keepdims=False)
        own_pos = own * MOBA_BLOCK + blk_off
        logit_own = (jnp.einsum('bhqd,bhcd->bhqc', q_blk, k_own).astype(jnp.float32) * scale
                     + rel_bias_dense(rel_table, q_pos, own_pos)[None])
        logit_own = jnp.where(own_pos[None, :] <= q_pos[:, None], logit_own, NEG)
        n_sel = top_k * MOBA_BLOCK
        logits = jnp.concatenate([logit_sel.reshape(B, H, MOBA_Q_CHUNK, n_sel), logit_own], axis=-1)
        p = jax.nn.softmax(logits, axis=-1)
        p_sel = p[..., :n_sel].reshape(B, H, MOBA_Q_CHUNK, top_k, MOBA_BLOCK).astype(v.dtype)
        p_own = p[..., n_sel:].astype(v.dtype)
        return (jnp.einsum('bhqkc,bhqkcd->bhqd', p_sel, v_sel)
                + jnp.einsum('bhqc,bhcd->bhqd', p_own, v_own))

    o = merge_blocks(lax.map(chunk, (q_chunks, jnp.arange(n_chunks) * MOBA_Q_CHUNK)))
    return (o * jax.nn.silu(z)) @ w_out


def forgetting_attention(u, w_in, w_out, b_f):
    B, S, _ = u.shape
    E, H, d = D_INNER, FOX_HEADS, FOX_HEAD_DIM
    proj = u @ w_in
    q = split_heads(proj[..., :E], H)
    k = split_heads(proj[..., E:2 * E], H)
    v = split_heads(proj[..., 2 * E:3 * E], H)
    z = proj[..., 3 * E:4 * E]
    log_f = jax.nn.log_sigmoid((proj[..., 4 * E:] + b_f).astype(jnp.float32))
    cum = jnp.cumsum(log_f, axis=1).transpose(0, 2, 1)
    scale = d ** -0.5
    nb = S // Q_BLOCK
    k_pos = jnp.arange(S)
    q_blocks = q.reshape(B, H, nb, Q_BLOCK, d).transpose(2, 0, 1, 3, 4)
    c_blocks = cum.reshape(B, H, nb, Q_BLOCK).transpose(2, 0, 1, 3)

    def block(args):
        q_blk, c_blk, start = args
        q_pos = start + jnp.arange(Q_BLOCK)
        logits = (jnp.einsum('bhqd,bhsd->bhqs', q_blk, k).astype(jnp.float32) * scale
                  + c_blk[..., :, None] - cum[..., None, :])
        logits = jnp.where(k_pos[None, :] <= q_pos[:, None], logits, NEG)
        p = jax.nn.softmax(logits, axis=-1)
        return jnp.einsum('bhqs,bhsd->bhqd', p.astype(v.dtype), v)

    o = merge_blocks(lax.map(block, (q_blocks, c_blocks, jnp.arange(nb) * Q_BLOCK)))
    return (o * jax.nn.silu(z)) @ w_out


def setup_inputs(seed: int = 0) -> dict:
    key = jax.random.key(seed)
    ks = iter(jax.random.split(key, 32))
    f32 = jnp.float32
    s_in = D_MODEL ** -0.5

    def in_proj(k, extra_cols=0):
        scales = jnp.concatenate([
            jnp.full((2 * D_INNER,), s_in, f32),
            jnp.full((D_INNER,), DN_BETA * s_in, f32),
            jnp.full((D_INNER,), s_in, f32),
            jnp.full((extra_cols,), FORGET_W_SCALE * s_in, f32)])
        return jax.random.normal(k, (D_MODEL, 4 * D_INNER + extra_cols), f32) * scales

    def out_proj(k):
        return jax.random.normal(k, (D_INNER, D_MODEL), f32) * (DN_BETA * D_INNER ** -0.5)

    def gain(k, n):
        return 1.0 + 0.02 * jax.random.normal(k, (n,), f32)

    def small(k, n, s=0.02):
        return s * jax.random.normal(k, (n,), f32)

    p = {}
    p["x"] = jax.random.normal(next(ks), (BATCH, SEQ, D_MODEL), f32)
    p["rel_bias"] = 0.5 * jax.random.normal(next(ks), (REL_BUCKETS, REL_HEADS), f32)
    for i in (0,):
        pass
    p["w_in_0"] = in_proj(next(ks))
    p["lam_q1_0"] = small(next(ks), DIFF_QK_DIM, 0.1)
    p["lam_k1_0"] = small(next(ks), DIFF_QK_DIM, 0.1)
    p["lam_q2_0"] = small(next(ks), DIFF_QK_DIM, 0.1)
    p["lam_k2_0"] = small(next(ks), DIFF_QK_DIM, 0.1)
    p["subln_g_0"] = gain(next(ks), DIFF_V_DIM)
    p["w_out_0"] = out_proj(next(ks))
    p["ln_g_0"] = gain(next(ks), D_MODEL)
    p["ln_b_0"] = small(next(ks), D_MODEL)
    p["w_in_1"] = in_proj(next(ks))
    p["w_out_1"] = out_proj(next(ks))
    p["ln_g_1"] = gain(next(ks), D_MODEL)
    p["ln_b_1"] = small(next(ks), D_MODEL)
    p["w_in_2"] = in_proj(next(ks), FOX_HEADS)
    p["b_f_2"] = jax.random.uniform(next(ks), (FOX_HEADS,), f32, FORGET_B_MIN, FORGET_B_MAX)
    p["w_out_2"] = out_proj(next(ks))
    p["ln_g_2"] = gain(next(ks), D_MODEL)
    p["ln_b_2"] = small(next(ks), D_MODEL)
    p["w_in_3"] = in_proj(next(ks))
    p["lam_q1_3"] = small(next(ks), DIFF_QK_DIM, 0.1)
    p["lam_k1_3"] = small(next(ks), DIFF_QK_DIM, 0.1)
    p["lam_q2_3"] = small(next(ks), DIFF_QK_DIM, 0.1)
    p["lam_k2_3"] = small(next(ks), DIFF_QK_DIM, 0.1)
    p["subln_g_3"] = gain(next(ks), DIFF_V_DIM)
    p["w_out_3"] = out_proj(next(ks))
    p["ln_g_3"] = gain(next(ks), D_MODEL)
    p["ln_b_3"] = small(next(ks), D_MODEL)
    return p


def reference(x, rel_bias,
              w_in_0, lam_q1_0, lam_k1_0, lam_q2_0, lam_k2_0, subln_g_0, w_out_0, ln_g_0, ln_b_0,
              w_in_1, w_out_1, ln_g_1, ln_b_1,
              w_in_2, b_f_2, w_out_2, ln_g_2, ln_b_2,
              w_in_3, lam_q1_3, lam_k1_3, lam_q2_3, lam_k2_3, subln_g_3, w_out_3, ln_g_3, ln_b_3):
    layers = [
        (w_in_0, w_out_0, (lam_q1_0, lam_k1_0, lam_q2_0, lam_k2_0, subln_g_0), ln_g_0, ln_b_0),
        (w_in_1, w_out_1, (), ln_g_1, ln_b_1),
        (w_in_2, w_out_2, (b_f_2,), ln_g_2, ln_b_2),
        (w_in_3, w_out_3, (lam_q1_3, lam_k1_3, lam_q2_3, lam_k2_3, subln_g_3), ln_g_3, ln_b_3),
    ]
    h = x
    for i in range(DEPTH):
        w_in, w_out, extra, g, b = layers[i]
        kind = i % N_MIXERS
        if kind == 0:
            y = diff_attention(h, w_in, w_out, *extra, rel_bias, i)
        elif kind == 1:
            y = moba_attention(h, w_in, w_out, rel_bias)
        else:
            y = forgetting_attention(h, w_in, w_out, *extra)
        h = layer_norm(DN_ALPHA * h + y, g, b)
    return h
```

```python
import functools
import math

import jax
import jax.numpy as jnp
import numpy as np
from jax import lax
from jax.experimental import pallas as pl
from jax.experimental.pallas import tpu as pltpu

D_MODEL = 2048
D_INNER = 2048
DEPTH = 4
DIFF_HEADS = 8
MOBA_HEADS = 16
MOBA_BLOCK = 256
MOBA_TOPK = 3
FOX_HEADS = 16
HEAD_DIM = 128
REL_BUCKETS = 32
REL_MAX_DIST = 128
REL_HEADS = 16
LN_EPS = 1e-5
RMS_EPS = 1e-5
NEG = -1e30
DN_ALPHA = (2.0 * DEPTH) ** 0.25
LOG2E = math.log2(math.e)
QK_SCALE = HEAD_DIM ** -0.5

TQ = 256
VMEM_LIMIT = 56 * 1024 * 1024

F32 = jnp.float32
BF16 = jnp.bfloat16


def _dot_nt(a, b):
    return lax.dot_general(a, b, (((1,), (1,)), ((), ())), preferred_element_type=F32)


def _silu(z):
    return z / (1.0 + jnp.exp(-z))


def _inproj_kernel(x_ref, w_ref, cs_ref, o_ref, wbf_ref):
    @pl.when(pl.program_id(1) == 0)
    def _():
        wbf_ref[...] = w_ref[...].astype(BF16)

    acc = jnp.dot(x_ref[...], wbf_ref[...], preferred_element_type=F32)
    o_ref[...] = (acc * cs_ref[...]).astype(o_ref.dtype)


def _in_proj(x_bf, w, col_scale, n_out, tm=1024, tn=1024):
    m, k = x_bf.shape
    return pl.pallas_call(
        _inproj_kernel,
        grid=(n_out // tn, m // tm),
        in_specs=[
            pl.BlockSpec((tm, k), lambda n, i: (i, 0)),
            pl.BlockSpec((k, tn), lambda n, i: (0, n)),
            pl.BlockSpec((1, tn), lambda n, i: (0, n)),
        ],
        out_specs=pl.BlockSpec((tm, tn), lambda n, i: (i, n)),
        out_shape=jax.ShapeDtypeStruct((m, n_out), BF16),
        scratch_shapes=[pltpu.VMEM((k, tn), BF16)],
        compiler_params=pltpu.CompilerParams(
            dimension_semantics=("arbitrary", "arbitrary"),
            vmem_limit_bytes=VMEM_LIMIT),
        name="in_proj",
    )(x_bf, w, col_scale)


def _outproj_ln_kernel(a_ref, w_ref, h_ref, g_ref, b_ref, o_ref, obf_ref):
    y = jnp.dot(a_ref[...], w_ref[...], preferred_element_type=F32)
    r = DN_ALPHA * h_ref[...] + y
    mu = jnp.mean(r, axis=1, keepdims=True)
    d = r - mu
    var = jnp.mean(d * d, axis=1, keepdims=True)
    out = d * lax.rsqrt(var + LN_EPS) * g_ref[...] + b_ref[...]
    o_ref[...] = out
    obf_ref[...] = out.astype(BF16)


def _out_proj_ln(a_bf, w_bf, h, g, b, tm=512):
    m, k = a_bf.shape
    n = w_bf.shape[1]
    return pl.pallas_call(
        _outproj_ln_kernel,
        grid=(m // tm,),
        in_specs=[
            pl.BlockSpec((tm, k), lambda i: (i, 0)),
            pl.BlockSpec((k, n), lambda i: (0, 0)),
            pl.BlockSpec((tm, n), lambda i: (i, 0)),
            pl.BlockSpec((1, n), lambda i: (0, 0)),
            pl.BlockSpec((1, n), lambda i: (0, 0)),
        ],
        out_specs=[
            pl.BlockSpec((tm, n), lambda i: (i, 0)),
            pl.BlockSpec((tm, n), lambda i: (i, 0)),
        ],
        out_shape=[jax.ShapeDtypeStruct((m, n), F32),
                   jax.ShapeDtypeStruct((m, n), BF16)],
        compiler_params=pltpu.CompilerParams(
            dimension_semantics=("arbitrary",),
            vmem_limit_bytes=VMEM_LIMIT),
        name="out_proj_ln",
    )(a_bf, w_bf, h, g.reshape(1, n), b.reshape(1, n))


def _bucket_tiles():
    r = np.arange(TQ)[:, None]
    c = np.arange(TQ)[None, :]
    tiles = []
    for delta in (0, 1):
        n = np.maximum(r - c + delta * TQ, 0)
        max_exact = REL_BUCKETS // 2
        nf = np.maximum(n, 1).astype(np.float32)
        large = max_exact + (np.log(nf / max_exact) / math.log(REL_MAX_DIST / max_exact)
                             * (REL_BUCKETS - max_exact)).astype(np.int32)
        large = np.minimum(large, REL_BUCKETS - 1)
        tiles.append(np.where(n < max_exact, n, large).astype(np.int32))
    return np.stack(tiles)


def _bias_kernel(tab_ref, bkt_ref, o_ref):
    c = pl.program_id(0)
    far = tab_ref[c, REL_BUCKETS - 1]
    row = lax.broadcasted_iota(jnp.int32, (TQ, TQ), 0)
    col = lax.broadcasted_iota(jnp.int32, (TQ, TQ), 1)
    for t in range(2):
        bk = bkt_ref[t]
        acc = jnp.zeros((TQ, TQ), F32)
        for j in range(REL_BUCKETS):
            acc = jnp.where(bk == j, tab_ref[c, j], acc)
        val = (acc - far) * LOG2E
        if t == 0:
            val = jnp.where(col <= row, val, NEG)
        o_ref[0, t] = val


def _bias_tiles(rel_bias):
    tab = rel_bias.T
    bkt = jnp.asarray(_bucket_tiles())
    return pl.pallas_call(
        _bias_kernel,
        grid=(REL_HEADS,),
        in_specs=[
            pl.BlockSpec(memory_space=pltpu.SMEM),
            pl.BlockSpec((2, TQ, TQ), lambda c: (0, 0, 0)),
        ],
        out_specs=pl.BlockSpec((1, 2, TQ, TQ), lambda c: (c, 0, 0, 0)),
        out_shape=jax.ShapeDtypeStruct((REL_HEADS, 2, TQ, TQ), F32),
        name="rel_bias_tiles",
    )(tab, bkt)


def _softmax_pv(parts, v, row_shift=None):
    m = parts[0].max(axis=1, keepdims=True)
    for p in parts[1:]:
        m = jnp.maximum(m, p.max(axis=1, keepdims=True))
    if row_shift is None:
        shift = -m
    else:
        m_full = m + row_shift
        shift = row_shift - m_full
    ps = [jnp.exp2(p + shift) for p in parts]
    l = ps[0].sum(axis=1, keepdims=True)
    for p in ps[1:]:
        l = l + p.sum(axis=1, keepdims=True)
    pb = jnp.concatenate([p.astype(BF16) for p in ps], axis=1) if len(ps) > 1 else ps[0].astype(BF16)
    acc = jnp.dot(pb, v, preferred_element_type=F32)
    return acc / l


def _diff_kernel(lamv_ref, g_ref, q_ref, k_ref, v_ref, z_ref, bias_ref, o_ref, *, lam_init, seq):
    lv = lamv_ref[...]
    s1 = jnp.sum(lv[0:1] * lv[1:2], axis=1, keepdims=True)
    s2 = jnp.sum(lv[2:3] * lv[3:4], axis=1, keepdims=True)
    lam = jnp.exp(s1) - jnp.exp(s2) + lam_init
    for qi in range(seq // TQ):
        r0 = qi * TQ
        outs = []
        for j in range(2):
            c0 = j * HEAD_DIM
            q = q_ref[0, r0:r0 + TQ, c0:c0 + HEAD_DIM]
            parts = []
            if qi >= 2:
                parts.append(_dot_nt(q, k_ref[0, 0:r0 - TQ, c0:c0 + HEAD_DIM]))
            if qi >= 1:
                parts.append(_dot_nt(q, k_ref[0, r0 - TQ:r0, c0:c0 + HEAD_DIM]) + bias_ref[j, 1])
            parts.append(_dot_nt(q, k_ref[0, r0:r0 + TQ, c0:c0 + HEAD_DIM]) + bias_ref[j, 0])
            outs.append(_softmax_pv(parts, v_ref[0, 0:r0 + TQ, :]))
        o = outs[0] - lam * outs[1]
        ms = jnp.mean(o * o, axis=1, keepdims=True)
        o = o * lax.rsqrt(ms + RMS_EPS) * g_ref[...] * (1.0 - lam_init)
        z = z_ref[0, r0:r0 + TQ, :].astype(F32)
        o_ref[0, r0:r0 + TQ, :] = (o * _silu(z)).astype(BF16)


def _diff_attention(proj, lamv, subln_g, bias, lam_init):
    b, s, _ = proj.shape
    w = 2 * HEAD_DIM
    nh = D_INNER // w
    return pl.pallas_call(
        functools.partial(_diff_kernel, lam_init=lam_init, seq=s),
        grid=(b, DIFF_HEADS),
        in_specs=[
            pl.BlockSpec((4, HEAD_DIM), lambda i, h: (0, 0)),
            pl.BlockSpec((1, w), lambda i, h: (0, 0)),
            pl.BlockSpec((1, s, w), lambda i, h: (i, 0, h)),
            pl.BlockSpec((1, s, w), lambda i, h: (i, 0, nh + h)),
            pl.BlockSpec((1, s, w), lambda i, h: (i, 0, 2 * nh + h)),
            pl.BlockSpec((1, s, w), lambda i, h: (i, 0, 3 * nh + h)),
            pl.BlockSpec((2, 2, TQ, TQ), lambda i, h: (h, 0, 0, 0)),
        ],
        out_specs=pl.BlockSpec((1, s, w), lambda i, h: (i, 0, h)),
        out_shape=jax.ShapeDtypeStruct((b, s, D_INNER), BF16),
        compiler_params=pltpu.CompilerParams(
            dimension_semantics=("arbitrary", "arbitrary"),
            vmem_limit_bytes=VMEM_LIMIT),
        name="diff_attention",
    )(lamv, subln_g.reshape(1, w), proj, proj, proj, proj, bias)


def _block_indicator(seq):
    n_kb = seq // MOBA_BLOCK
    ind = np.zeros((HEAD_DIM, seq), np.float32)
    for j in range(n_kb):
        ind[j, j * MOBA_BLOCK:(j + 1) * MOBA_BLOCK] = 1.0
        ind[n_kb + j, j * MOBA_BLOCK:(j + 1) * MOBA_BLOCK] = 1.0
    return ind


def _moba_kernel(ind_ref, q_ref, k_ref, v_ref, z_ref, bias_ref, o_ref, *, seq):
    n_kb = seq // MOBA_BLOCK
    ksum = jnp.dot(ind_ref[...], k_ref[0], preferred_element_type=F32)
    kmean = ksum * (1.0 / MOBA_BLOCK)
    hi = kmean.astype(BF16)
    lo = (kmean - hi.astype(F32)).astype(BF16)
    row = lax.broadcasted_iota(jnp.int32, (HEAD_DIM, HEAD_DIM), 0)
    km = jnp.where(row < n_kb, hi, lo)
    for qi in range(n_kb):
        r0 = qi * TQ
        q = q_ref[0, r0:r0 + TQ, :]
        sel = None
        if qi > MOBA_TOPK:
            graw = _dot_nt(q, km)
            gate = graw + pltpu.roll(graw, HEAD_DIM - n_kb, 1)
            cols = [gate[:, j:j + 1] for j in range(qi)]
            sel = []
            for j in range(qi):
                rank = jnp.zeros((TQ, 1), F32)
                for jj in range(qi):
                    if jj == j:
                        continue
                    beats = (cols[jj] >= cols[j]) if jj < j else (cols[jj] > cols[j])
                    rank = rank + jnp.where(beats, 1.0, 0.0)
                sel.append(rank < float(MOBA_TOPK))
        parts = []
        for j in range(qi):
            sj = _dot_nt(q, k_ref[0, j * TQ:(j + 1) * TQ, :])
            if j == qi - 1:
                sj = sj + bias_ref[0, 1]
            if sel is not None:
                sj = jnp.where(sel[j], sj, NEG)
            parts.append(sj)
        parts.append(_dot_nt(q, k_ref[0, r0:r0 + TQ, :]) + bias_ref[0, 0])
        o = _softmax_pv(parts, v_ref[0, 0:r0 + TQ, :])
        z = z_ref[0, r0:r0 + TQ, :].astype(F32)
        o_ref[0, r0:r0 + TQ, :] = (o * _silu(z)).astype(BF16)


def _head_specs(s, nh):
    w = HEAD_DIM
    return [
        pl.BlockSpec((1, s, w), lambda i, h: (i, 0, h)),
        pl.BlockSpec((1, s, w), lambda i, h: (i, 0, nh + h)),
        pl.BlockSpec((1, s, w), lambda i, h: (i, 0, 2 * nh + h)),
        pl.BlockSpec((1, s, w), lambda i, h: (i, 0, 3 * nh + h)),
    ]


def _moba_attention(proj, bias):
    b, s, _ = proj.shape
    ind = jnp.asarray(_block_indicator(s), BF16)
    return pl.pallas_call(
        functools.partial(_moba_kernel, seq=s),
        grid=(b, MOBA_HEADS),
        in_specs=[pl.BlockSpec((HEAD_DIM, s), lambda i, h: (0, 0))]
        + _head_specs(s, MOBA_HEADS)
        + [pl.BlockSpec((1, 2, TQ, TQ), lambda i, h: (h, 0, 0, 0))],
        out_specs=pl.BlockSpec((1, s, HEAD_DIM), lambda i, h: (i, 0, h)),
        out_shape=jax.ShapeDtypeStruct((b, s, D_INNER), BF16),
        compiler_params=pltpu.CompilerParams(
            dimension_semantics=("arbitrary", "arbitrary"),
            vmem_limit_bytes=VMEM_LIMIT),
        name="moba_attention",
    )(ind, proj, proj, proj, proj, bias)


def _fox_gate_kernel(x_ref, wf_ref, bf_ref, o_ref, *, seq):
    f = _dot_nt(wf_ref[...], x_ref[0])
    f = f[0:FOX_HEADS] + bf_ref[0:FOX_HEADS]
    c = jnp.minimum(f, 0.0) - jnp.log1p(jnp.exp(-jnp.abs(f)))
    lane = lax.broadcasted_iota(jnp.int32, c.shape, 1)
    d = 1
    while d < seq:
        c = c + jnp.where(lane >= d, pltpu.roll(c, d, 1), 0.0)
        d *= 2
    o_ref[0] = c * LOG2E


def _fox_gate(h_bf3, wf_t, bf_col):
    b, s, d = h_bf3.shape
    return pl.pallas_call(
        functools.partial(_fox_gate_kernel, seq=s),
        grid=(b,),
        in_specs=[
            pl.BlockSpec((1, s, d), lambda i: (i, 0, 0)),
            pl.BlockSpec((HEAD_DIM, d), lambda i: (0, 0)),
            pl.BlockSpec((HEAD_DIM, 1), lambda i: (0, 0)),
        ],
        out_specs=pl.BlockSpec((1, FOX_HEADS, s), lambda i: (i, 0, 0)),
        out_shape=jax.ShapeDtypeStruct((b, FOX_HEADS, s), F32),
        compiler_params=pltpu.CompilerParams(
            dimension_semantics=("arbitrary",),
            vmem_limit_bytes=VMEM_LIMIT),
        name="fox_gate",
    )(h_bf3, wf_t, bf_col)


def _causal_tile():
    r = np.arange(TQ)[:, None]
    c = np.arange(TQ)[None, :]
    return np.where(c <= r, 0.0, NEG).astype(np.float32)


def _fox_kernel(mask_ref, crow_ref, ccol_ref, q_ref, k_ref, v_ref, z_ref, o_ref, *, seq):
    for qi in range(seq // TQ):
        r0 = qi * TQ
        q = q_ref[0, r0:r0 + TQ, :]
        parts = []
        if qi >= 1:
            parts.append(_dot_nt(q, k_ref[0, 0:r0, :]) - crow_ref[0, 0, :, 0:r0])
        parts.append(_dot_nt(q, k_ref[0, r0:r0 + TQ, :]) - crow_ref[0, 0, :, r0:r0 + TQ] + mask_ref[...])
        o = _softmax_pv(parts, v_ref[0, 0:r0 + TQ, :], row_shift=ccol_ref[0, 0, r0:r0 + TQ, :])
        z = z_ref[0, r0:r0 + TQ, :].astype(F32)
        o_ref[0, r0:r0 + TQ, :] = (o * _silu(z)).astype(BF16)


def _fox_attention(proj, cum):
    b, s, _ = proj.shape
    crow = cum.reshape(b, FOX_HEADS, 1, s)
    ccol = cum.reshape(b, FOX_HEADS, s, 1)
    mask = jnp.asarray(_causal_tile())
    return pl.pallas_call(
        functools.partial(_fox_kernel, seq=s),
        grid=(b, FOX_HEADS),
        in_specs=[
            pl.BlockSpec((TQ, TQ), lambda i, h: (0, 0)),
            pl.BlockSpec((1, 1, 1, s), lambda i, h: (i, h, 0, 0)),
            pl.BlockSpec((1, 1, s, 1), lambda i, h: (i, h, 0, 0)),
        ] + _head_specs(s, FOX_HEADS),
        out_specs=pl.BlockSpec((1, s, HEAD_DIM), lambda i, h: (i, 0, h)),
        out_shape=jax.ShapeDtypeStruct((b, s, D_INNER), BF16),
        compiler_params=pltpu.CompilerParams(
            dimension_semantics=("arbitrary", "arbitrary"),
            vmem_limit_bytes=VMEM_LIMIT),
        name="fox_attention",
    )(mask, crow, ccol, proj, proj, proj, proj)


def _col_scale():
    cs = np.ones((1, 4 * D_INNER), np.float32)
    cs[0, :D_INNER] = QK_SCALE * LOG2E
    return jnp.asarray(cs)


def kernel(x, rel_bias, w_in_0, lam_q1_0, lam_k1_0, lam_q2_0, lam_k2_0, subln_g_0, w_out_0, ln_g_0, ln_b_0, w_in_1, w_out_1, ln_g_1, ln_b_1, w_in_2, b_f_2, w_out_2, ln_g_2, ln_b_2, w_in_3, lam_q1_3, lam_k1_3, lam_q2_3, lam_k2_3, subln_g_3, w_out_3, ln_g_3, ln_b_3):
    b, s, d = x.shape
    m = b * s
    e = D_INNER
    col_scale = _col_scale()
    bias = _bias_tiles(rel_bias)

    h = x.reshape(m, d)
    h_bf = h.astype(BF16)
    layers = [
        (w_in_0, w_out_0, ln_g_0, ln_b_0),
        (w_in_1, w_out_1, ln_g_1, ln_b_1),
        (w_in_2, w_out_2, ln_g_2, ln_b_2),
        (w_in_3, w_out_3, ln_g_3, ln_b_3),
    ]
    diff_extra = {
        0: (lam_q1_0, lam_k1_0, lam_q2_0, lam_k2_0, subln_g_0),
        3: (lam_q1_3, lam_k1_3, lam_q2_3, lam_k2_3, subln_g_3),
    }
    for i, (w_in, w_out, g, beta) in enumerate(layers):
        proj = _in_proj(h_bf, w_in, col_scale, 4 * e).reshape(b, s, 4 * e)
        kind = i % 3
        if kind == 0:
            lq1, lk1, lq2, lk2, sg = diff_extra[i]
            lam_init = 0.8 - 0.6 * math.exp(-0.3 * i)
            a = _diff_attention(proj, jnp.stack([lq1, lk1, lq2, lk2]), sg, bias, lam_init)
        elif kind == 1:
            a = _moba_attention(proj, bias)
        else:
            wf_t = jnp.zeros((HEAD_DIM, d), BF16).at[:FOX_HEADS].set(w_in[:, 4 * e:].T.astype(BF16))
            bf_col = jnp.zeros((HEAD_DIM, 1), F32).at[:FOX_HEADS, 0].set(b_f_2)
            cum = _fox_gate(h_bf.reshape(b, s, d), wf_t, bf_col)
            a = _fox_attention(proj, cum)
        h, h_bf = _out_proj_ln(a.reshape(m, e), w_out.astype(BF16), h, g, beta)
    return h.reshape(b, s, d)
```

```python
import functools
import math

import jax
import jax.numpy as jnp
import numpy as np
from jax import lax
from jax.experimental import pallas as pl
from jax.experimental.pallas import tpu as pltpu

D_MODEL = 2048
D_INNER = 2048
DEPTH = 4
DIFF_HEADS = 8
MOBA_HEADS = 16
MOBA_BLOCK = 256
MOBA_TOPK = 3
FOX_HEADS = 16
HEAD_DIM = 128
REL_BUCKETS = 32
REL_MAX_DIST = 128
REL_HEADS = 16
LN_EPS = 1e-5
RMS_EPS = 1e-5
NEG = -1e30
DN_ALPHA = (2.0 * DEPTH) ** 0.25
LOG2E = math.log2(math.e)
QK_SCALE = HEAD_DIM ** -0.5

TQ = 256
VMEM_LIMIT = 56 * 1024 * 1024

F32 = jnp.float32
BF16 = jnp.bfloat16


def _dot_nt(a, b):
    return lax.dot_general(a, b, (((1,), (1,)), ((), ())), preferred_element_type=F32)


def _silu(z):
    return z / (1.0 + jnp.exp(-z))


def _inproj_kernel(x_ref, w_ref, cs_ref, o_ref, wbf_ref):
    @pl.when(pl.program_id(1) == 0)
    def _():
        wbf_ref[...] = w_ref[...].astype(BF16)

    acc = jnp.dot(x_ref[...], wbf_ref[...], preferred_element_type=F32)
    o_ref[...] = (acc * cs_ref[...]).astype(o_ref.dtype)


def _in_proj(x_bf, w, col_scale, n_out, tm=1024, tn=1024):
    m, k = x_bf.shape
    return pl.pallas_call(
        _inproj_kernel,
        grid=(n_out // tn, m // tm),
        in_specs=[
            pl.BlockSpec((tm, k), lambda n, i: (i, 0)),
            pl.BlockSpec((k, tn), lambda n, i: (0, n)),
            pl.BlockSpec((1, tn), lambda n, i: (0, n)),
        ],
        out_specs=pl.BlockSpec((tm, tn), lambda n, i: (i, n)),
        out_shape=jax.ShapeDtypeStruct((m, n_out), BF16),
        scratch_shapes=[pltpu.VMEM((k, tn), BF16)],
        compiler_params=pltpu.CompilerParams(
            dimension_semantics=("arbitrary", "arbitrary"),
            vmem_limit_bytes=VMEM_LIMIT),
        name="in_proj",
    )(x_bf, w, col_scale)


def _outproj_ln_kernel(a_ref, w_ref, h_ref, g_ref, b_ref, o_ref, obf_ref):
    y = jnp.dot(a_ref[...], w_ref[...], preferred_element_type=F32)
    r = DN_ALPHA * h_ref[...] + y
    mu = jnp.mean(r, axis=1, keepdims=True)
    d = r - mu
    var = jnp.mean(d * d, axis=1, keepdims=True)
    out = d * lax.rsqrt(var + LN_EPS) * g_ref[...] + b_ref[...]
    o_ref[...] = out
    obf_ref[...] = out.astype(BF16)


def _out_proj_ln(a_bf, w_bf, h, g, b, tm=512):
    m, k = a_bf.shape
    n = w_bf.shape[1]
    return pl.pallas_call(
        _outproj_ln_kernel,
        grid=(m // tm,),
        in_specs=[
            pl.BlockSpec((tm, k), lambda i: (i, 0)),
            pl.BlockSpec((k, n), lambda i: (0, 0)),
            pl.BlockSpec((tm, n), lambda i: (i, 0)),
            pl.BlockSpec((1, n), lambda i: (0, 0)),
            pl.BlockSpec((1, n), lambda i: (0, 0)),
        ],
        out_specs=[
            pl.BlockSpec((tm, n), lambda i: (i, 0)),
            pl.BlockSpec((tm, n), lambda i: (i, 0)),
        ],
        out_shape=[jax.ShapeDtypeStruct((m, n), F32),
                   jax.ShapeDtypeStruct((m, n), BF16)],
        compiler_params=pltpu.CompilerParams(
            dimension_semantics=("arbitrary",),
            vmem_limit_bytes=VMEM_LIMIT),
        name="out_proj_ln",
    )(a_bf, w_bf, h, g.reshape(1, n), b.reshape(1, n))


def _bucket_tiles():
    r = np.arange(TQ)[:, None]
    c = np.arange(TQ)[None, :]
    tiles = []
    for delta in (0, 1):
        n = np.maximum(r - c + delta * TQ, 0)
        max_exact = REL_BUCKETS // 2
        nf = np.maximum(n, 1).astype(np.float32)
        large = max_exact + (np.log(nf / max_exact) / math.log(REL_MAX_DIST / max_exact)
                             * (REL_BUCKETS - max_exact)).astype(np.int32)
        large = np.minimum(large, REL_BUCKETS - 1)
        tiles.append(np.where(n < max_exact, n, large).astype(np.int32))
    return np.stack(tiles)


def _bias_kernel(tab_ref, bkt_ref, o_ref):
    c = pl.program_id(0)
    far = tab_ref[c, REL_BUCKETS - 1]
    row = lax.broadcasted_iota(jnp.int32, (TQ, TQ), 0)
    col = lax.broadcasted_iota(jnp.int32, (TQ, TQ), 1)
    for t in range(2):
        bk = bkt_ref[t]
        acc = jnp.zeros((TQ, TQ), F32)
        for j in range(REL_BUCKETS):
            acc = jnp.where(bk == j, tab_ref[c, j], acc)
        val = (acc - far) * LOG2E
        if t == 0:
            val = jnp.where(col <= row, val, NEG)
        o_ref[0, t] = val


def _bias_tiles(rel_bias):
    tab = rel_bias.T
    bkt = jnp.asarray(_bucket_tiles())
    return pl.pallas_call(
        _bias_kernel,
        grid=(REL_HEADS,),
        in_specs=[
            pl.BlockSpec(memory_space=pltpu.SMEM),
            pl.BlockSpec((2, TQ, TQ), lambda c: (0, 0, 0)),
        ],
        out_specs=pl.BlockSpec((1, 2, TQ, TQ), lambda c: (c, 0, 0, 0)),
        out_shape=jax.ShapeDtypeStruct((REL_HEADS, 2, TQ, TQ), F32),
        name="rel_bias_tiles",
    )(tab, bkt)


def _fold_lanes(parts, op):
    acc = None
    for p in parts:
        for t in range(p.shape[1] // HEAD_DIM):
            blk = p[:, t * HEAD_DIM:(t + 1) * HEAD_DIM]
            acc = blk if acc is None else op(acc, blk)
    return acc


def _softmax_pv(parts, v, row_shift=None):
    m = _fold_lanes(parts, jnp.maximum).max(axis=1, keepdims=True)
    if row_shift is None:
        shift = -m
    else:
        m_full = m + row_shift
        shift = row_shift - m_full
    ps = [jnp.exp2(p + shift) for p in parts]
    l = _fold_lanes(ps, jnp.add).sum(axis=1, keepdims=True)
    pb = jnp.concatenate([p.astype(BF16) for p in ps], axis=1) if len(ps) > 1 else ps[0].astype(BF16)
    acc = jnp.dot(pb, v, preferred_element_type=F32)
    return acc / l


def _diff_kernel(lamv_ref, g_ref, q_ref, k_ref, v_ref, z_ref, bias_ref, o_ref, *, lam_init, seq):
    lv = lamv_ref[...]
    s1 = jnp.sum(lv[0:1] * lv[1:2], axis=1, keepdims=True)
    s2 = jnp.sum(lv[2:3] * lv[3:4], axis=1, keepdims=True)
    lam = jnp.exp(s1) - jnp.exp(s2) + lam_init

    def logits(qi, j):
        r0 = qi * TQ
        c0 = j * HEAD_DIM
        q = q_ref[0, r0:r0 + TQ, c0:c0 + HEAD_DIM]
        parts = []
        if qi >= 2:
            parts.append(_dot_nt(q, k_ref[0, 0:r0 - TQ, c0:c0 + HEAD_DIM]))
        if qi >= 1:
            parts.append(_dot_nt(q, k_ref[0, r0 - TQ:r0, c0:c0 + HEAD_DIM]) + bias_ref[j, 1])
        parts.append(_dot_nt(q, k_ref[0, r0:r0 + TQ, c0:c0 + HEAD_DIM]) + bias_ref[j, 0])
        return parts

    tasks = [(qi, j) for qi in range(seq // TQ) for j in range(2)]
    nxt = logits(*tasks[0])
    outs = []
    for t, (qi, j) in enumerate(tasks):
        r0 = qi * TQ
        parts = nxt
        if t + 1 < len(tasks):
            nxt = logits(*tasks[t + 1])
        outs.append(_softmax_pv(parts, v_ref[0, 0:r0 + TQ, :]))
        if j == 0:
            continue
        o = outs[-2] - lam * outs[-1]
        ms = jnp.mean(o * o, axis=1, keepdims=True)
        o = o * lax.rsqrt(ms + RMS_EPS) * g_ref[...] * (1.0 - lam_init)
        z = z_ref[0, r0:r0 + TQ, :].astype(F32)
        o_ref[0, r0:r0 + TQ, :] = (o * _silu(z)).astype(BF16)


def _diff_attention(proj, lamv, subln_g, bias, lam_init):
    b, s, _ = proj.shape
    w = 2 * HEAD_DIM
    nh = D_INNER // w
    return pl.pallas_call(
        functools.partial(_diff_kernel, lam_init=lam_init, seq=s),
        grid=(b, DIFF_HEADS),
        in_specs=[
            pl.BlockSpec((4, HEAD_DIM), lambda i, h: (0, 0)),
            pl.BlockSpec((1, w), lambda i, h: (0, 0)),
            pl.BlockSpec((1, s, w), lambda i, h: (i, 0, h)),
            pl.BlockSpec((1, s, w), lambda i, h: (i, 0, nh + h)),
            pl.BlockSpec((1, s, w), lambda i, h: (i, 0, 2 * nh + h)),
            pl.BlockSpec((1, s, w), lambda i, h: (i, 0, 3 * nh + h)),
            pl.BlockSpec((2, 2, TQ, TQ), lambda i, h: (h, 0, 0, 0)),
        ],
        out_specs=pl.BlockSpec((1, s, w), lambda i, h: (i, 0, h)),
        out_shape=jax.ShapeDtypeStruct((b, s, D_INNER), BF16),
        compiler_params=pltpu.CompilerParams(
            dimension_semantics=("arbitrary", "arbitrary"),
            vmem_limit_bytes=VMEM_LIMIT),
        name="diff_attention",
    )(lamv, subln_g.reshape(1, w), proj, proj, proj, proj, bias)


def _block_indicator(seq):
    n_kb = seq // MOBA_BLOCK
    ind = np.zeros((HEAD_DIM, seq), np.float32)
    for j in range(n_kb):
        ind[j, j * MOBA_BLOCK:(j + 1) * MOBA_BLOCK] = 1.0
    return ind


def _moba_kernel(ind_ref, kind_ref, q_ref, k_ref, v_ref, z_ref, bias_ref, o_ref, *, seq):
    n_kb = seq // MOBA_BLOCK
    ksum = jnp.dot(ind_ref[...], k_ref[0], preferred_element_type=F32)
    kmean = ksum * (1.0 / MOBA_BLOCK)
    hi = kmean.astype(BF16)
    lo = (kmean - hi.astype(F32)).astype(BF16)
    km2 = jnp.concatenate([hi, lo], axis=1)
    sub = lax.broadcasted_iota(jnp.int32, (n_kb, TQ), 0)

    def block_mask(qi, q):
        gate = _dot_nt(km2, jnp.concatenate([q, q], axis=1))[0:n_kb]
        rank = jnp.zeros((n_kb, TQ), F32)
        for bp in range(qi):
            gb = gate[bp:bp + 1, :]
            rank = rank + jnp.where(sub > bp, jnp.where(gb >= gate, 1.0, 0.0), jnp.where(gb > gate, 1.0, 0.0))
        mt = jnp.where(sub >= qi, 0.0, jnp.where(rank < float(MOBA_TOPK), 0.0, NEG))
        mt = jnp.concatenate([mt, jnp.zeros((HEAD_DIM - n_kb, TQ), F32)], axis=0)
        return mt.T.astype(BF16)

    def logits(qi):
        r0 = qi * TQ
        q = q_ref[0, r0:r0 + TQ, :]
        masked = qi > MOBA_TOPK
        if masked:
            q = jnp.concatenate([q, block_mask(qi, q)], axis=1)
        parts = []
        for j in range(qi + 1):
            kj = k_ref[0, j * TQ:(j + 1) * TQ, :]
            if masked:
                kj = jnp.concatenate([kj, kind_ref[j * TQ:(j + 1) * TQ, :]], axis=1)
            sj = _dot_nt(q, kj)
            if j >= qi - 1:
                sj = sj + bias_ref[0, qi - j]
            parts.append(sj)
        return parts

    nxt = logits(0)
    for qi in range(n_kb):
        r0 = qi * TQ
        parts = nxt
        if qi + 1 < n_kb:
            nxt = logits(qi + 1)
        o = _softmax_pv(parts, v_ref[0, 0:r0 + TQ, :])
        z = z_ref[0, r0:r0 + TQ, :].astype(F32)
        o_ref[0, r0:r0 + TQ, :] = (o * _silu(z)).astype(BF16)


def _head_specs(s, nh):
    w = HEAD_DIM
    return [
        pl.BlockSpec((1, s, w), lambda i, h: (i, 0, h)),
        pl.BlockSpec((1, s, w), lambda i, h: (i, 0, nh + h)),
        pl.BlockSpec((1, s, w), lambda i, h: (i, 0, 2 * nh + h)),
        pl.BlockSpec((1, s, w), lambda i, h: (i, 0, 3 * nh + h)),
    ]


def _moba_attention(proj, bias):
    b, s, _ = proj.shape
    assert s // MOBA_BLOCK == 8 and MOBA_BLOCK == TQ
    ind = _block_indicator(s)
    return pl.pallas_call(
        functools.partial(_moba_kernel, seq=s),
        grid=(b, MOBA_HEADS),
        in_specs=[pl.BlockSpec((HEAD_DIM, s), lambda i, h: (0, 0)),
                  pl.BlockSpec((s, HEAD_DIM), lambda i, h: (0, 0))]
        + _head_specs(s, MOBA_HEADS)
        + [pl.BlockSpec((1, 2, TQ, TQ), lambda i, h: (h, 0, 0, 0))],
        out_specs=pl.BlockSpec((1, s, HEAD_DIM), lambda i, h: (i, 0, h)),
        out_shape=jax.ShapeDtypeStruct((b, s, D_INNER), BF16),
        compiler_params=pltpu.CompilerParams(
            dimension_semantics=("arbitrary", "arbitrary"),
            vmem_limit_bytes=VMEM_LIMIT),
        name="moba_attention",
    )(jnp.asarray(ind, BF16), jnp.asarray(ind.T, BF16), proj, proj, proj, proj, bias)


def _fox_gate_kernel(x_ref, wf_ref, bf_ref, row_ref, col_ref, *, seq):
    f = jnp.dot(x_ref[0], wf_ref[...], preferred_element_type=F32) + bf_ref[...]
    c = jnp.minimum(f, 0.0) - jnp.log1p(jnp.exp(-jnp.abs(f)))
    c = c.T[0:FOX_HEADS]
    lane = lax.broadcasted_iota(jnp.int32, c.shape, 1)
    d = 1
    while d < seq:
        c = c + jnp.where(lane >= d, pltpu.roll(c, d, 1), 0.0)
        d *= 2
    c = c * LOG2E
    row_ref[0] = c
    col_ref[0] = jnp.concatenate([c, jnp.zeros((HEAD_DIM - FOX_HEADS, seq), F32)], axis=0).T


def _fox_gate(h_bf3, wf, bf_row):
    b, s, d = h_bf3.shape
    return pl.pallas_call(
        functools.partial(_fox_gate_kernel, seq=s),
        grid=(b,),
        in_specs=[
            pl.BlockSpec((1, s, d), lambda i: (i, 0, 0)),
            pl.BlockSpec((d, HEAD_DIM), lambda i: (0, 0)),
            pl.BlockSpec((1, HEAD_DIM), lambda i: (0, 0)),
        ],
        out_specs=[pl.BlockSpec((1, FOX_HEADS, s), lambda i: (i, 0, 0)),
                   pl.BlockSpec((1, s, HEAD_DIM), lambda i: (i, 0, 0))],
        out_shape=[jax.ShapeDtypeStruct((b, FOX_HEADS, s), F32),
                   jax.ShapeDtypeStruct((b, s, HEAD_DIM), F32)],
        compiler_params=pltpu.CompilerParams(
            dimension_semantics=("arbitrary",),
            vmem_limit_bytes=VMEM_LIMIT),
        name="fox_gate",
    )(h_bf3, wf, bf_row)


def _causal_tile():
    r = np.arange(TQ)[:, None]
    c = np.arange(TQ)[None, :]
    return np.where(c <= r, 0.0, NEG).astype(np.float32)


def _fox_kernel(mask_ref, crow_ref, ccol_ref, q_ref, k_ref, v_ref, z_ref, o_ref, *, seq):
    def logits(qi):
        r0 = qi * TQ
        q = q_ref[0, r0:r0 + TQ, :]
        parts = []
        if qi >= 1:
            parts.append(_dot_nt(q, k_ref[0, 0:r0, :]) - crow_ref[0, 0, :, 0:r0])
        parts.append(_dot_nt(q, k_ref[0, r0:r0 + TQ, :]) - crow_ref[0, 0, :, r0:r0 + TQ] + mask_ref[...])
        return parts

    nq = seq // TQ
    head_lane = lax.broadcasted_iota(jnp.int32, (TQ, HEAD_DIM), 1) == pl.program_id(1)
    nxt = logits(0)
    for qi in range(nq):
        r0 = qi * TQ
        parts = nxt
        if qi + 1 < nq:
            nxt = logits(qi + 1)
        c_t = jnp.sum(jnp.where(head_lane, ccol_ref[0, r0:r0 + TQ, :], 0.0), axis=1, keepdims=True)
        o = _softmax_pv(parts, v_ref[0, 0:r0 + TQ, :], row_shift=c_t)
        z = z_ref[0, r0:r0 + TQ, :].astype(F32)
        o_ref[0, r0:r0 + TQ, :] = (o * _silu(z)).astype(BF16)


def _fox_attention(proj, cum_row, cum_col):
    b, s, _ = proj.shape
    crow = cum_row.reshape(b, FOX_HEADS, 1, s)
    mask = jnp.asarray(_causal_tile())
    return pl.pallas_call(
        functools.partial(_fox_kernel, seq=s),
        grid=(b, FOX_HEADS),
        in_specs=[
            pl.BlockSpec((TQ, TQ), lambda i, h: (0, 0)),
            pl.BlockSpec((1, 1, 1, s), lambda i, h: (i, h, 0, 0)),
            pl.BlockSpec((1, s, HEAD_DIM), lambda i, h: (i, 0, 0)),
        ] + _head_specs(s, FOX_HEADS),
        out_specs=pl.BlockSpec((1, s, HEAD_DIM), lambda i, h: (i, 0, h)),
        out_shape=jax.ShapeDtypeStruct((b, s, D_INNER), BF16),
        compiler_params=pltpu.CompilerParams(
            dimension_semantics=("arbitrary", "arbitrary"),
            vmem_limit_bytes=VMEM_LIMIT),
        name="fox_attention",
    )(mask, crow, cum_col, proj, proj, proj, proj)


def _col_scale():
    cs = np.ones((1, 4 * D_INNER), np.float32)
    cs[0, :D_INNER] = QK_SCALE * LOG2E
    return jnp.asarray(cs)


def kernel(x, rel_bias, w_in_0, lam_q1_0, lam_k1_0, lam_q2_0, lam_k2_0, subln_g_0, w_out_0, ln_g_0, ln_b_0, w_in_1, w_out_1, ln_g_1, ln_b_1, w_in_2, b_f_2, w_out_2, ln_g_2, ln_b_2, w_in_3, lam_q1_3, lam_k1_3, lam_q2_3, lam_k2_3, subln_g_3, w_out_3, ln_g_3, ln_b_3):
    b, s, d = x.shape
    m = b * s
    e = D_INNER
    col_scale = _col_scale()
    bias = _bias_tiles(rel_bias)

    h = x.reshape(m, d)
    h_bf = h.astype(BF16)
    layers = [
        (w_in_0, w_out_0, ln_g_0, ln_b_0),
        (w_in_1, w_out_1, ln_g_1, ln_b_1),
        (w_in_2, w_out_2, ln_g_2, ln_b_2),
        (w_in_3, w_out_3, ln_g_3, ln_b_3),
    ]
    diff_extra = {
        0: (lam_q1_0, lam_k1_0, lam_q2_0, lam_k2_0, subln_g_0),
        3: (lam_q1_3, lam_k1_3, lam_q2_3, lam_k2_3, subln_g_3),
    }
    for i, (w_in, w_out, g, beta) in enumerate(layers):
        proj = _in_proj(h_bf, w_in, col_scale, 4 * e).reshape(b, s, 4 * e)
        kind = i % 3
        if kind == 0:
            lq1, lk1, lq2, lk2, sg = diff_extra[i]
            lam_init = 0.8 - 0.6 * math.exp(-0.3 * i)
            a = _diff_attention(proj, jnp.stack([lq1, lk1, lq2, lk2]), sg, bias, lam_init)
        elif kind == 1:
            a = _moba_attention(proj, bias)
        else:
            wf = jnp.pad(w_in[:, 4 * e:].astype(BF16), ((0, 0), (0, HEAD_DIM - FOX_HEADS)))
            bf_row = jnp.pad(b_f_2, (0, HEAD_DIM - FOX_HEADS)).reshape(1, HEAD_DIM)
            cum_row, cum_col = _fox_gate(h_bf.reshape(b, s, d), wf, bf_row)
            a = _fox_attention(proj, cum_row, cum_col)
        h, h_bf = _out_proj_ln(a.reshape(m, e), w_out.astype(BF16), h, g, beta)
    return h.reshape(b, s, d)
```

```python
import functools
import math

import jax
import jax.numpy as jnp
import numpy as np
from jax import lax
from jax.experimental import pallas as pl
from jax.experimental.pallas import tpu as pltpu

D_MODEL = 2048
D_INNER = 2048
DEPTH = 4
DIFF_HEADS = 8
MOBA_HEADS = 16
MOBA_BLOCK = 256
MOBA_TOPK = 3
FOX_HEADS = 16
HEAD_DIM = 128
REL_BUCKETS = 32
REL_MAX_DIST = 128
REL_HEADS = 16
LN_EPS = 1e-5
RMS_EPS = 1e-5
NEG = -1e30
DN_ALPHA = (2.0 * DEPTH) ** 0.25
LOG2E = math.log2(math.e)
QK_SCALE = HEAD_DIM ** -0.5

TQ = 256
VMEM_LIMIT = 56 * 1024 * 1024

F32 = jnp.float32
BF16 = jnp.bfloat16


def _dot_nt(a, b):
    return lax.dot_general(a, b, (((1,), (1,)), ((), ())), preferred_element_type=F32)


def _silu(z):
    return z / (1.0 + jnp.exp(-z))


def _inproj_kernel(x_ref, w_ref, cs_ref, o_ref, wbf_ref, *, w_is_nk):
    @pl.when(pl.program_id(1) == 0)
    def _():
        wbf_ref[...] = w_ref[...].astype(BF16)

    if w_is_nk:
        acc = _dot_nt(x_ref[...], wbf_ref[...])
    else:
        acc = jnp.dot(x_ref[...], wbf_ref[...], preferred_element_type=F32)
    o_ref[...] = (acc * cs_ref[...]).astype(o_ref.dtype)


def _in_proj(x_bf, w, col_scale, n_out, w_is_nk=False, tm=1024, tn=1024):
    m, k = x_bf.shape
    if w_is_nk:
        w_block, w_spec = (tn, k), pl.BlockSpec((tn, k), lambda n, i: (n, 0))
    else:
        w_block, w_spec = (k, tn), pl.BlockSpec((k, tn), lambda n, i: (0, n))
    return pl.pallas_call(
        functools.partial(_inproj_kernel, w_is_nk=w_is_nk),
        grid=(n_out // tn, m // tm),
        in_specs=[
            pl.BlockSpec((tm, k), lambda n, i: (i, 0)),
            w_spec,
            pl.BlockSpec((1, tn), lambda n, i: (0, n)),
        ],
        out_specs=pl.BlockSpec((tm, tn), lambda n, i: (i, n)),
        out_shape=jax.ShapeDtypeStruct((m, n_out), BF16),
        scratch_shapes=[pltpu.VMEM(w_block, BF16)],
        compiler_params=pltpu.CompilerParams(
            dimension_semantics=("arbitrary", "arbitrary"),
            vmem_limit_bytes=VMEM_LIMIT),
        name="in_proj",
    )(x_bf, w, col_scale)


def _outproj_ln_kernel(a_ref, w_ref, h_ref, g_ref, b_ref, o_ref, obf_ref):
    y = jnp.dot(a_ref[...], w_ref[...], preferred_element_type=F32)
    r = DN_ALPHA * h_ref[...] + y
    mu = jnp.mean(r, axis=1, keepdims=True)
    d = r - mu
    var = jnp.mean(d * d, axis=1, keepdims=True)
    out = d * lax.rsqrt(var + LN_EPS) * g_ref[...] + b_ref[...]
    o_ref[...] = out
    obf_ref[...] = out.astype(BF16)


def _out_proj_ln(a_bf, w_bf, h, g, b, tm=512):
    m, k = a_bf.shape
    n = w_bf.shape[1]
    return pl.pallas_call(
        _outproj_ln_kernel,
        grid=(m // tm,),
        in_specs=[
            pl.BlockSpec((tm, k), lambda i: (i, 0)),
            pl.BlockSpec((k, n), lambda i: (0, 0)),
            pl.BlockSpec((tm, n), lambda i: (i, 0)),
            pl.BlockSpec((1, n), lambda i: (0, 0)),
            pl.BlockSpec((1, n), lambda i: (0, 0)),
        ],
        out_specs=[
            pl.BlockSpec((tm, n), lambda i: (i, 0)),
            pl.BlockSpec((tm, n), lambda i: (i, 0)),
        ],
        out_shape=[jax.ShapeDtypeStruct((m, n), F32),
                   jax.ShapeDtypeStruct((m, n), BF16)],
        compiler_params=pltpu.CompilerParams(
            dimension_semantics=("arbitrary",),
            vmem_limit_bytes=VMEM_LIMIT),
        name="out_proj_ln",
    )(a_bf, w_bf, h, g.reshape(1, n), b.reshape(1, n))


def _bucket_tiles():
    r = np.arange(TQ)[:, None]
    c = np.arange(TQ)[None, :]
    tiles = []
    for delta in (0, 1):
        n = np.maximum(r - c + delta * TQ, 0)
        max_exact = REL_BUCKETS // 2
        nf = np.maximum(n, 1).astype(np.float32)
        large = max_exact + (np.log(nf / max_exact) / math.log(REL_MAX_DIST / max_exact)
                             * (REL_BUCKETS - max_exact)).astype(np.int32)
        large = np.minimum(large, REL_BUCKETS - 1)
        tiles.append(np.where(n < max_exact, n, large).astype(np.int32))
    return np.stack(tiles)


def _bias_kernel(tab_ref, bkt_ref, o_ref):
    c = pl.program_id(0)
    far = tab_ref[c, REL_BUCKETS - 1]
    row = lax.broadcasted_iota(jnp.int32, (TQ, TQ), 0)
    col = lax.broadcasted_iota(jnp.int32, (TQ, TQ), 1)
    for t in range(2):
        bk = bkt_ref[t]
        acc = jnp.zeros((TQ, TQ), F32)
        for j in range(REL_BUCKETS):
            acc = jnp.where(bk == j, tab_ref[c, j], acc)
        val = (acc - far) * LOG2E
        if t == 0:
            val = jnp.where(col <= row, val, NEG)
        o_ref[0, t] = val


def _bias_tiles(rel_bias):
    tab = rel_bias.T
    bkt = jnp.asarray(_bucket_tiles())
    return pl.pallas_call(
        _bias_kernel,
        grid=(REL_HEADS,),
        in_specs=[
            pl.BlockSpec(memory_space=pltpu.SMEM),
            pl.BlockSpec((2, TQ, TQ), lambda c: (0, 0, 0)),
        ],
        out_specs=pl.BlockSpec((1, 2, TQ, TQ), lambda c: (c, 0, 0, 0)),
        out_shape=jax.ShapeDtypeStruct((REL_HEADS, 2, TQ, TQ), F32),
        name="rel_bias_tiles",
    )(tab, bkt)


def _fold_lanes(parts, op):
    acc = None
    for p in parts:
        for t in range(p.shape[1] // HEAD_DIM):
            blk = p[:, t * HEAD_DIM:(t + 1) * HEAD_DIM]
            acc = blk if acc is None else op(acc, blk)
    return acc


def _softmax_pv(parts, v, row_shift=None):
    m = _fold_lanes(parts, jnp.maximum).max(axis=1, keepdims=True)
    if row_shift is None:
        shift = -m
    else:
        m_full = m + row_shift
        shift = row_shift - m_full
    ps = [jnp.exp2(p + shift) for p in parts]
    pb = jnp.concatenate([p.astype(BF16) for p in ps], axis=1) if len(ps) > 1 else ps[0].astype(BF16)
    if v.shape[1] == HEAD_DIM:
        v1 = jnp.concatenate([v, jnp.ones(v.shape, v.dtype)], axis=1)
        acc = jnp.dot(pb, v1, preferred_element_type=F32)
        return acc[:, 0:HEAD_DIM] / acc[:, HEAD_DIM:HEAD_DIM + 1]
    l = _fold_lanes(ps, jnp.add).sum(axis=1, keepdims=True)
    acc = jnp.dot(pb, v, preferred_element_type=F32)
    return acc / l


def _diff_kernel(lamv_ref, g_ref, q_ref, k_ref, v_ref, z_ref, bias_ref, o_ref, *, lam_init, seq):
    lv = lamv_ref[...]
    s1 = jnp.sum(lv[0:1] * lv[1:2], axis=1, keepdims=True)
    s2 = jnp.sum(lv[2:3] * lv[3:4], axis=1, keepdims=True)
    lam = jnp.exp(s1) - jnp.exp(s2) + lam_init

    def logits(qi, j):
        r0 = qi * TQ
        c0 = j * HEAD_DIM
        q = q_ref[0, r0:r0 + TQ, c0:c0 + HEAD_DIM]
        parts = []
        if qi >= 2:
            parts.append(_dot_nt(q, k_ref[0, 0:r0 - TQ, c0:c0 + HEAD_DIM]))
        if qi >= 1:
            parts.append(_dot_nt(q, k_ref[0, r0 - TQ:r0, c0:c0 + HEAD_DIM]) + bias_ref[j, 1])
        parts.append(_dot_nt(q, k_ref[0, r0:r0 + TQ, c0:c0 + HEAD_DIM]) + bias_ref[j, 0])
        return parts

    tasks = [(qi, j) for qi in range(seq // TQ) for j in range(2)]
    nxt = logits(*tasks[0])
    outs = []
    for t, (qi, j) in enumerate(tasks):
        r0 = qi * TQ
        parts = nxt
        if t + 1 < len(tasks):
            nxt = logits(*tasks[t + 1])
        outs.append(_softmax_pv(parts, v_ref[0, 0:r0 + TQ, :]))
        if j == 0:
            continue
        o = outs[-2] - lam * outs[-1]
        ms = jnp.mean(o * o, axis=1, keepdims=True)
        o = o * lax.rsqrt(ms + RMS_EPS) * g_ref[...] * (1.0 - lam_init)
        z = z_ref[0, r0:r0 + TQ, :].astype(F32)
        o_ref[0, r0:r0 + TQ, :] = (o * _silu(z)).astype(BF16)


def _diff_attention(proj, lamv, subln_g, bias, lam_init):
    b, s, _ = proj.shape
    w = 2 * HEAD_DIM
    nh = D_INNER // w
    return pl.pallas_call(
        functools.partial(_diff_kernel, lam_init=lam_init, seq=s),
        grid=(b, DIFF_HEADS),
        in_specs=[
            pl.BlockSpec((4, HEAD_DIM), lambda i, h: (0, 0)),
            pl.BlockSpec((1, w), lambda i, h: (0, 0)),
            pl.BlockSpec((1, s, w), lambda i, h: (i, 0, h)),
            pl.BlockSpec((1, s, w), lambda i, h: (i, 0, nh + h)),
            pl.BlockSpec((1, s, w), lambda i, h: (i, 0, 2 * nh + h)),
            pl.BlockSpec((1, s, w), lambda i, h: (i, 0, 3 * nh + h)),
            pl.BlockSpec((2, 2, TQ, TQ), lambda i, h: (h, 0, 0, 0)),
        ],
        out_specs=pl.BlockSpec((1, s, w), lambda i, h: (i, 0, h)),
        out_shape=jax.ShapeDtypeStruct((b, s, D_INNER), BF16),
        compiler_params=pltpu.CompilerParams(
            dimension_semantics=("arbitrary", "arbitrary"),
            vmem_limit_bytes=VMEM_LIMIT),
        name="diff_attention",
    )(lamv, subln_g.reshape(1, w), proj, proj, proj, proj, bias)


def _block_indicator(seq):
    n_kb = seq // MOBA_BLOCK
    ind = np.zeros((HEAD_DIM, seq), np.float32)
    for j in range(n_kb):
        ind[j, j * MOBA_BLOCK:(j + 1) * MOBA_BLOCK] = 1.0
    return ind


def _moba_kernel(ind_ref, kind_ref, q_ref, k_ref, v_ref, z_ref, bias_ref, o_ref, *, seq):
    n_kb = seq // MOBA_BLOCK
    ksum = jnp.dot(ind_ref[...], k_ref[0], preferred_element_type=F32)
    kmean = ksum * (1.0 / MOBA_BLOCK)
    hi = kmean.astype(BF16)
    lo = (kmean - hi.astype(F32)).astype(BF16)
    km2 = jnp.concatenate([hi, lo], axis=1)
    sub = lax.broadcasted_iota(jnp.int32, (n_kb, TQ), 0)

    def block_mask(qi, q):
        gate = _dot_nt(km2, jnp.concatenate([q, q], axis=1))[0:n_kb]
        rank = jnp.zeros((n_kb, TQ), F32)
        for bp in range(qi):
            gb = gate[bp:bp + 1, :]
            rank = rank + jnp.where(sub > bp, jnp.where(gb >= gate, 1.0, 0.0), jnp.where(gb > gate, 1.0, 0.0))
        mt = jnp.where(sub >= qi, 0.0, jnp.where(rank < float(MOBA_TOPK), 0.0, NEG))
        mt = jnp.concatenate([mt, jnp.zeros((HEAD_DIM - n_kb, TQ), F32)], axis=0)
        return mt.T.astype(BF16)

    def logits(qi):
        r0 = qi * TQ
        q = q_ref[0, r0:r0 + TQ, :]
        masked = qi > MOBA_TOPK
        if masked:
            q = jnp.concatenate([q, block_mask(qi, q)], axis=1)
        parts = []
        for j in range(qi + 1):
            kj = k_ref[0, j * TQ:(j + 1) * TQ, :]
            if masked:
                kj = jnp.concatenate([kj, kind_ref[j * TQ:(j + 1) * TQ, :]], axis=1)
            sj = _dot_nt(q, kj)
            if j >= qi - 1:
                sj = sj + bias_ref[0, qi - j]
            parts.append(sj)
        return parts

    nxt = logits(0)
    for qi in range(n_kb):
        r0 = qi * TQ
        parts = nxt
        if qi + 1 < n_kb:
            nxt = logits(qi + 1)
        o = _softmax_pv(parts, v_ref[0, 0:r0 + TQ, :])
        z = z_ref[0, r0:r0 + TQ, :].astype(F32)
        o_ref[0, r0:r0 + TQ, :] = (o * _silu(z)).astype(BF16)


def _head_specs(s, nh):
    w = HEAD_DIM
    return [
        pl.BlockSpec((1, s, w), lambda i, h: (i, 0, h)),
        pl.BlockSpec((1, s, w), lambda i, h: (i, 0, nh + h)),
        pl.BlockSpec((1, s, w), lambda i, h: (i, 0, 2 * nh + h)),
        pl.BlockSpec((1, s, w), lambda i, h: (i, 0, 3 * nh + h)),
    ]


def _moba_attention(proj, bias):
    b, s, _ = proj.shape
    assert s // MOBA_BLOCK == 8 and MOBA_BLOCK == TQ
    ind = _block_indicator(s)
    return pl.pallas_call(
        functools.partial(_moba_kernel, seq=s),
        grid=(b, MOBA_HEADS),
        in_specs=[pl.BlockSpec((HEAD_DIM, s), lambda i, h: (0, 0)),
                  pl.BlockSpec((s, HEAD_DIM), lambda i, h: (0, 0))]
        + _head_specs(s, MOBA_HEADS)
        + [pl.BlockSpec((1, 2, TQ, TQ), lambda i, h: (h, 0, 0, 0))],
        out_specs=pl.BlockSpec((1, s, HEAD_DIM), lambda i, h: (i, 0, h)),
        out_shape=jax.ShapeDtypeStruct((b, s, D_INNER), BF16),
        compiler_params=pltpu.CompilerParams(
            dimension_semantics=("arbitrary", "arbitrary"),
            vmem_limit_bytes=VMEM_LIMIT),
        name="moba_attention",
    )(jnp.asarray(ind, BF16), jnp.asarray(ind.T, BF16), proj, proj, proj, proj, bias)


def _fox_gate_kernel(x_ref, wf_ref, bf_ref, row_ref, col_ref, *, seq):
    f = _dot_nt(x_ref[0], wf_ref[...]) + bf_ref[...]
    c = jnp.minimum(f, 0.0) - jnp.log1p(jnp.exp(-jnp.abs(f)))
    c = c.T[0:FOX_HEADS]
    lane = lax.broadcasted_iota(jnp.int32, c.shape, 1)
    d = 1
    while d < seq:
        c = c + jnp.where(lane >= d, pltpu.roll(c, d, 1), 0.0)
        d *= 2
    c = c * LOG2E
    row_ref[0] = c
    col_ref[0] = jnp.concatenate([c, jnp.zeros((HEAD_DIM - FOX_HEADS, seq), F32)], axis=0).T


def _fox_gate(h_bf3, wf, bf_row):
    b, s, d = h_bf3.shape
    return pl.pallas_call(
        functools.partial(_fox_gate_kernel, seq=s),
        grid=(b,),
        in_specs=[
            pl.BlockSpec((1, s, d), lambda i: (i, 0, 0)),
            pl.BlockSpec((HEAD_DIM, d), lambda i: (0, 0)),
            pl.BlockSpec((1, HEAD_DIM), lambda i: (0, 0)),
        ],
        out_specs=[pl.BlockSpec((1, FOX_HEADS, s), lambda i: (i, 0, 0)),
                   pl.BlockSpec((1, s, HEAD_DIM), lambda i: (i, 0, 0))],
        out_shape=[jax.ShapeDtypeStruct((b, FOX_HEADS, s), F32),
                   jax.ShapeDtypeStruct((b, s, HEAD_DIM), F32)],
        compiler_params=pltpu.CompilerParams(
            dimension_semantics=("arbitrary",),
            vmem_limit_bytes=VMEM_LIMIT),
        name="fox_gate",
    )(h_bf3, wf, bf_row)


def _causal_tile():
    r = np.arange(TQ)[:, None]
    c = np.arange(TQ)[None, :]
    return np.where(c <= r, 0.0, NEG).astype(np.float32)


def _fox_kernel(mask_ref, crow_ref, ccol_ref, q_ref, k_ref, v_ref, z_ref, o_ref, *, seq):
    def logits(qi):
        r0 = qi * TQ
        q = q_ref[0, r0:r0 + TQ, :]
        parts = []
        if qi >= 1:
            parts.append(_dot_nt(q, k_ref[0, 0:r0, :]) - crow_ref[0, 0, :, 0:r0])
        parts.append(_dot_nt(q, k_ref[0, r0:r0 + TQ, :]) - crow_ref[0, 0, :, r0:r0 + TQ] + mask_ref[...])
        return parts

    nq = seq // TQ
    head_lane = lax.broadcasted_iota(jnp.int32, (TQ, HEAD_DIM), 1) == pl.program_id(1)
    nxt = logits(0)
    for qi in range(nq):
        r0 = qi * TQ
        parts = nxt
        if qi + 1 < nq:
            nxt = logits(qi + 1)
        c_t = jnp.sum(jnp.where(head_lane, ccol_ref[0, r0:r0 + TQ, :], 0.0), axis=1, keepdims=True)
        o = _softmax_pv(parts, v_ref[0, 0:r0 + TQ, :], row_shift=c_t)
        z = z_ref[0, r0:r0 + TQ, :].astype(F32)
        o_ref[0, r0:r0 + TQ, :] = (o * _silu(z)).astype(BF16)


def _fox_attention(proj, cum_row, cum_col):
    b, s, _ = proj.shape
    crow = cum_row.reshape(b, FOX_HEADS, 1, s)
    mask = jnp.asarray(_causal_tile())
    return pl.pallas_call(
        functools.partial(_fox_kernel, seq=s),
        grid=(b, FOX_HEADS),
        in_specs=[
            pl.BlockSpec((TQ, TQ), lambda i, h: (0, 0)),
            pl.BlockSpec((1, 1, 1, s), lambda i, h: (i, h, 0, 0)),
            pl.BlockSpec((1, s, HEAD_DIM), lambda i, h: (i, 0, 0)),
        ] + _head_specs(s, FOX_HEADS),
        out_specs=pl.BlockSpec((1, s, HEAD_DIM), lambda i, h: (i, 0, h)),
        out_shape=jax.ShapeDtypeStruct((b, s, D_INNER), BF16),
        compiler_params=pltpu.CompilerParams(
            dimension_semantics=("arbitrary", "arbitrary"),
            vmem_limit_bytes=VMEM_LIMIT),
        name="fox_attention",
    )(mask, crow, cum_col, proj, proj, proj, proj)


def _col_scale():
    cs = np.ones((1, 4 * D_INNER), np.float32)
    cs[0, :D_INNER] = QK_SCALE * LOG2E
    return jnp.asarray(cs)


def kernel(x, rel_bias, w_in_0, lam_q1_0, lam_k1_0, lam_q2_0, lam_k2_0, subln_g_0, w_out_0, ln_g_0, ln_b_0, w_in_1, w_out_1, ln_g_1, ln_b_1, w_in_2, b_f_2, w_out_2, ln_g_2, ln_b_2, w_in_3, lam_q1_3, lam_k1_3, lam_q2_3, lam_k2_3, subln_g_3, w_out_3, ln_g_3, ln_b_3):
    b, s, d = x.shape
    m = b * s
    e = D_INNER
    col_scale = _col_scale()
    bias = _bias_tiles(rel_bias)

    h = x.reshape(m, d)
    h_bf = h.astype(BF16)
    layers = [
        (w_in_0, w_out_0, ln_g_0, ln_b_0),
        (w_in_1, w_out_1, ln_g_1, ln_b_1),
        (w_in_2, w_out_2, ln_g_2, ln_b_2),
        (w_in_3, w_out_3, ln_g_3, ln_b_3),
    ]
    diff_extra = {
        0: (lam_q1_0, lam_k1_0, lam_q2_0, lam_k2_0, subln_g_0),
        3: (lam_q1_3, lam_k1_3, lam_q2_3, lam_k2_3, subln_g_3),
    }
    for i, (w_in, w_out, g, beta) in enumerate(layers):
        kind = i % 3
        if kind == 2:
            w_t = w_in.T
            proj = _in_proj(h_bf, w_t, col_scale, 4 * e, w_is_nk=True)
        else:
            proj = _in_proj(h_bf, w_in, col_scale, 4 * e)
        proj = proj.reshape(b, s, 4 * e)
        if kind == 0:
            lq1, lk1, lq2, lk2, sg = diff_extra[i]
            lam_init = 0.8 - 0.6 * math.exp(-0.3 * i)
            a = _diff_attention(proj, jnp.stack([lq1, lk1, lq2, lk2]), sg, bias, lam_init)
        elif kind == 1:
            a = _moba_attention(proj, bias)
        else:
            wf = jnp.pad(w_t[4 * e:].astype(BF16), ((0, HEAD_DIM - FOX_HEADS), (0, 0)))
            bf_row = jnp.pad(b_f_2, (0, HEAD_DIM - FOX_HEADS)).reshape(1, HEAD_DIM)
            cum_row, cum_col = _fox_gate(h_bf.reshape(b, s, d), wf, bf_row)
            a = _fox_attention(proj, cum_row, cum_col)
        h, h_bf = _out_proj_ln(a.reshape(m, e), w_out.astype(BF16), h, g, beta)
    return h.reshape(b, s, d)
```

```python
import functools
import math

import jax
import jax.numpy as jnp
import numpy as np
from jax import lax
from jax.experimental import pallas as pl
from jax.experimental.pallas import tpu as pltpu

D_MODEL = 2048
D_INNER = 2048
DEPTH = 4
DIFF_HEADS = 8
MOBA_HEADS = 16
MOBA_BLOCK = 256
MOBA_TOPK = 3
FOX_HEADS = 16
HEAD_DIM = 128
REL_BUCKETS = 32
REL_MAX_DIST = 128
REL_HEADS = 16
LN_EPS = 1e-5
RMS_EPS = 1e-5
NEG = -1e30
DN_ALPHA = (2.0 * DEPTH) ** 0.25
LOG2E = math.log2(math.e)
QK_SCALE = HEAD_DIM ** -0.5

TQ = 256
VMEM_LIMIT = 56 * 1024 * 1024

F32 = jnp.float32
BF16 = jnp.bfloat16


def _dot_nt(a, b):
    return lax.dot_general(a, b, (((1,), (1,)), ((), ())), preferred_element_type=F32)


def _silu(z):
    return z / (1.0 + jnp.exp(-z))


def _inproj_kernel(x_ref, w_ref, cs_ref, o_ref, wbf_ref, *, w_is_nk):
    @pl.when(pl.program_id(1) == 0)
    def _():
        wbf_ref[...] = w_ref[...].astype(BF16)

    x = x_ref[...].astype(BF16)
    if w_is_nk:
        acc = _dot_nt(x, wbf_ref[...])
    else:
        acc = jnp.dot(x, wbf_ref[...], preferred_element_type=F32)
    o_ref[...] = (acc * cs_ref[...]).astype(o_ref.dtype)


def _in_proj(x, w, col_scale, n_out, w_is_nk=False, tn=1024):
    m, k = x.shape
    tm = 2048 if x.dtype == BF16 else 1024
    if w_is_nk:
        w_block, w_spec = (tn, k), pl.BlockSpec((tn, k), lambda n, i: (n, 0))
    else:
        w_block, w_spec = (k, tn), pl.BlockSpec((k, tn), lambda n, i: (0, n))
    return pl.pallas_call(
        functools.partial(_inproj_kernel, w_is_nk=w_is_nk),
        grid=(n_out // tn, m // tm),
        in_specs=[
            pl.BlockSpec((tm, k), lambda n, i: (i, 0)),
            w_spec,
            pl.BlockSpec((1, tn), lambda n, i: (0, n)),
        ],
        out_specs=pl.BlockSpec((tm, tn), lambda n, i: (i, n)),
        out_shape=jax.ShapeDtypeStruct((m, n_out), BF16),
        scratch_shapes=[pltpu.VMEM(w_block, BF16)],
        compiler_params=pltpu.CompilerParams(
            dimension_semantics=("arbitrary", "arbitrary"),
            vmem_limit_bytes=VMEM_LIMIT),
        name="in_proj",
    )(x, w, col_scale)


LN_CHUNKS = 4


def _data_zero(x):
    rows, n = x.shape
    x = jnp.max(x.reshape(rows // 8, 8, n), axis=0)
    bits = pltpu.bitcast(_fold_lanes([x], jnp.maximum), jnp.uint32)
    zero = lax.shift_right_logical(lax.shift_right_logical(bits, jnp.uint32(16)), jnp.uint32(16))
    return pltpu.bitcast(zero, F32)[0:1, 0:1]


def _outproj_ln_kernel(a_ref, w_ref, h_ref, g_ref, b_ref, o_ref, obf_ref, y0_ref, y1_ref, *, nblk):
    i = pl.program_id(0)
    ys = (y0_ref, y1_ref)
    tm, n = o_ref.shape
    rows, cols = tm // LN_CHUNKS, n // LN_CHUNKS

    def layer_norm(y_ref, c):
        rs = slice(c * rows, (c + 1) * rows)
        r = DN_ALPHA * h_ref[rs, :] + y_ref[rs, :]
        mu = jnp.mean(r, axis=1, keepdims=True)
        d = r - mu
        var = jnp.mean(d * d, axis=1, keepdims=True)
        out = d * lax.rsqrt(var + LN_EPS) * g_ref[...] + b_ref[...]
        o_ref[rs, :] = out
        obf_ref[rs, :] = out.astype(BF16)
        return out

    def matmul(y_ref, c, anchor=None):
        cs = slice(c * cols, (c + 1) * cols)
        if anchor is None:
            y = jnp.dot(a_ref[...], w_ref[:, cs], preferred_element_type=F32)
        else:
            head = a_ref[:, 0:HEAD_DIM] + anchor.astype(BF16)
            y = (jnp.dot(head, w_ref[0:HEAD_DIM, cs], preferred_element_type=F32)
                 + jnp.dot(a_ref[:, HEAD_DIM:], w_ref[HEAD_DIM:, cs], preferred_element_type=F32))
        y_ref[:, cs] = y

    @pl.when(i == 0)
    def _():
        for c in range(LN_CHUNKS):
            matmul(ys[0], c)

    for parity in range(2):
        @pl.when(jnp.logical_and(jnp.logical_and(i > 0, i < nblk), i % 2 == parity))
        def _():
            zero = None
            for c in range(LN_CHUNKS):
                matmul(ys[parity], c, anchor=zero)
                zero = _data_zero(layer_norm(ys[1 - parity], c))

    @pl.when(i == nblk)
    def _():
        for c in range(LN_CHUNKS):
            layer_norm(ys[(nblk - 1) % 2], c)


def _out_proj_ln(a_bf, w_bf, h, g, b, tm=512):
    m, k = a_bf.shape
    n = w_bf.shape[1]
    nblk = m // tm
    cur = lambda i: (jnp.minimum(i, nblk - 1), 0)
    prev = lambda i: (jnp.maximum(i - 1, 0), 0)
    return pl.pallas_call(
        functools.partial(_outproj_ln_kernel, nblk=nblk),
        grid=(nblk + 1,),
        in_specs=[
            pl.BlockSpec((tm, k), cur),
            pl.BlockSpec((k, n), lambda i: (0, 0)),
            pl.BlockSpec((tm, n), prev),
            pl.BlockSpec((1, n), lambda i: (0, 0)),
            pl.BlockSpec((1, n), lambda i: (0, 0)),
        ],
        out_specs=[
            pl.BlockSpec((tm, n), prev),
            pl.BlockSpec((tm, n), prev),
        ],
        out_shape=[jax.ShapeDtypeStruct((m, n), F32),
                   jax.ShapeDtypeStruct((m, n), BF16)],
        scratch_shapes=[pltpu.VMEM((tm, n), F32), pltpu.VMEM((tm, n), F32)],
        compiler_params=pltpu.CompilerParams(
            dimension_semantics=("arbitrary",),
            vmem_limit_bytes=VMEM_LIMIT),
        name="out_proj_ln",
    )(a_bf, w_bf, h, g.reshape(1, n), b.reshape(1, n))


def _bucket_tiles():
    r = np.arange(TQ)[:, None]
    c = np.arange(TQ)[None, :]
    tiles = []
    for delta in (0, 1):
        n = np.maximum(r - c + delta * TQ, 0)
        max_exact = REL_BUCKETS // 2
        nf = np.maximum(n, 1).astype(np.float32)
        large = max_exact + (np.log(nf / max_exact) / math.log(REL_MAX_DIST / max_exact)
                             * (REL_BUCKETS - max_exact)).astype(np.int32)
        large = np.minimum(large, REL_BUCKETS - 1)
        tiles.append(np.where(n < max_exact, n, large).astype(np.int32))
    return np.stack(tiles)


def _bias_kernel(tab_ref, bkt_ref, o_ref):
    c = pl.program_id(0)
    far = tab_ref[c, REL_BUCKETS - 1]
    row = lax.broadcasted_iota(jnp.int32, (TQ, TQ), 0)
    col = lax.broadcasted_iota(jnp.int32, (TQ, TQ), 1)
    for t in range(2):
        bk = bkt_ref[t]
        acc = jnp.zeros((TQ, TQ), F32)
        for j in range(REL_BUCKETS):
            acc = jnp.where(bk == j, tab_ref[c, j], acc)
        val = (acc - far) * LOG2E
        if t == 0:
            val = jnp.where(col <= row, val, NEG)
        o_ref[0, t] = val


def _bias_tiles(rel_bias):
    tab = rel_bias.T
    bkt = jnp.asarray(_bucket_tiles())
    return pl.pallas_call(
        _bias_kernel,
        grid=(REL_HEADS,),
        in_specs=[
            pl.BlockSpec(memory_space=pltpu.SMEM),
            pl.BlockSpec((2, TQ, TQ), lambda c: (0, 0, 0)),
        ],
        out_specs=pl.BlockSpec((1, 2, TQ, TQ), lambda c: (c, 0, 0, 0)),
        out_shape=jax.ShapeDtypeStruct((REL_HEADS, 2, TQ, TQ), F32),
        name="rel_bias_tiles",
    )(tab, bkt)


def _fold_lanes(parts, op):
    acc = None
    for p in parts:
        for t in range(p.shape[1] // HEAD_DIM):
            blk = p[:, t * HEAD_DIM:(t + 1) * HEAD_DIM]
            acc = blk if acc is None else op(acc, blk)
    return acc


def _softmax_pv(parts, v, row_shift=None):
    m = _fold_lanes(parts, jnp.maximum).max(axis=1, keepdims=True)
    if row_shift is None:
        shift = -m
    else:
        m_full = m + row_shift
        shift = row_shift - m_full
    ps = [jnp.exp2(p + shift) for p in parts]
    pb = jnp.concatenate([p.astype(BF16) for p in ps], axis=1) if len(ps) > 1 else ps[0].astype(BF16)
    if v.shape[1] == HEAD_DIM:
        v1 = jnp.concatenate([v, jnp.ones(v.shape, v.dtype)], axis=1)
        acc = jnp.dot(pb, v1, preferred_element_type=F32)
        return acc[:, 0:HEAD_DIM] / acc[:, HEAD_DIM:HEAD_DIM + 1]
    l = _fold_lanes(ps, jnp.add).sum(axis=1, keepdims=True)
    acc = jnp.dot(pb, v, preferred_element_type=F32)
    return acc / l


def _diff_kernel(lamv_ref, g_ref, q_ref, k_ref, v_ref, z_ref, bias_ref, o_ref, *, lam_init, seq):
    lv = lamv_ref[...]
    s1 = jnp.sum(lv[0:1] * lv[1:2], axis=1, keepdims=True)
    s2 = jnp.sum(lv[2:3] * lv[3:4], axis=1, keepdims=True)
    lam = jnp.exp(s1) - jnp.exp(s2) + lam_init

    def logits(qi, j):
        r0 = qi * TQ
        c0 = j * HEAD_DIM
        q = q_ref[0, r0:r0 + TQ, c0:c0 + HEAD_DIM]
        parts = []
        if qi >= 2:
            parts.append(_dot_nt(q, k_ref[0, 0:r0 - TQ, c0:c0 + HEAD_DIM]))
        if qi >= 1:
            parts.append(_dot_nt(q, k_ref[0, r0 - TQ:r0, c0:c0 + HEAD_DIM]) + bias_ref[j, 1])
        parts.append(_dot_nt(q, k_ref[0, r0:r0 + TQ, c0:c0 + HEAD_DIM]) + bias_ref[j, 0])
        return parts

    tasks = [(qi, j) for qi in range(seq // TQ) for j in range(2)]
    nxt = logits(*tasks[0])
    outs = []
    for t, (qi, j) in enumerate(tasks):
        r0 = qi * TQ
        parts = nxt
        if t + 1 < len(tasks):
            nxt = logits(*tasks[t + 1])
        outs.append(_softmax_pv(parts, v_ref[0, 0:r0 + TQ, :]))
        if j == 0:
            continue
        o = outs[-2] - lam * outs[-1]
        ms = jnp.mean(o * o, axis=1, keepdims=True)
        o = o * lax.rsqrt(ms + RMS_EPS) * g_ref[...] * (1.0 - lam_init)
        z = z_ref[0, r0:r0 + TQ, :].astype(F32)
        o_ref[0, r0:r0 + TQ, :] = (o * _silu(z)).astype(BF16)


def _diff_attention(proj, lamv, subln_g, bias, lam_init):
    b, s, _ = proj.shape
    w = 2 * HEAD_DIM
    nh = D_INNER // w
    return pl.pallas_call(
        functools.partial(_diff_kernel, lam_init=lam_init, seq=s),
        grid=(b, DIFF_HEADS),
        in_specs=[
            pl.BlockSpec((4, HEAD_DIM), lambda i, h: (0, 0)),
            pl.BlockSpec((1, w), lambda i, h: (0, 0)),
            pl.BlockSpec((1, s, w), lambda i, h: (i, 0, h)),
            pl.BlockSpec((1, s, w), lambda i, h: (i, 0, nh + h)),
            pl.BlockSpec((1, s, w), lambda i, h: (i, 0, 2 * nh + h)),
            pl.BlockSpec((1, s, w), lambda i, h: (i, 0, 3 * nh + h)),
            pl.BlockSpec((2, 2, TQ, TQ), lambda i, h: (h, 0, 0, 0)),
        ],
        out_specs=pl.BlockSpec((1, s, w), lambda i, h: (i, 0, h)),
        out_shape=jax.ShapeDtypeStruct((b, s, D_INNER), BF16),
        compiler_params=pltpu.CompilerParams(
            dimension_semantics=("arbitrary", "arbitrary"),
            vmem_limit_bytes=VMEM_LIMIT),
        name="diff_attention",
    )(lamv, subln_g.reshape(1, w), proj, proj, proj, proj, bias)


def _block_indicator(seq):
    n_kb = seq // MOBA_BLOCK
    ind = np.zeros((HEAD_DIM, seq), np.float32)
    for j in range(n_kb):
        ind[j, j * MOBA_BLOCK:(j + 1) * MOBA_BLOCK] = 1.0
    return ind


def _moba_kernel(ind_ref, kind_ref, q_ref, k_ref, v_ref, z_ref, bias_ref, o_ref, *, seq):
    n_kb = seq // MOBA_BLOCK
    ksum = jnp.dot(ind_ref[...], k_ref[0], preferred_element_type=F32)
    kmean = ksum * (1.0 / MOBA_BLOCK)
    hi = kmean.astype(BF16)
    lo = (kmean - hi.astype(F32)).astype(BF16)
    km2 = jnp.concatenate([hi, lo], axis=1)
    sub = lax.broadcasted_iota(jnp.int32, (n_kb, TQ), 0)

    def block_mask(qi, q):
        gate = _dot_nt(km2, jnp.concatenate([q, q], axis=1))[0:n_kb]
        rank = jnp.zeros((n_kb, TQ), F32)
        for bp in range(qi):
            gb = gate[bp:bp + 1, :]
            rank = rank + jnp.where(sub > bp, jnp.where(gb >= gate, 1.0, 0.0), jnp.where(gb > gate, 1.0, 0.0))
        mt = jnp.where(sub >= qi, 0.0, jnp.where(rank < float(MOBA_TOPK), 0.0, NEG))
        mt = jnp.concatenate([mt, jnp.zeros((HEAD_DIM - n_kb, TQ), F32)], axis=0)
        return mt.T.astype(BF16)

    def logits(qi):
        r0 = qi * TQ
        q = q_ref[0, r0:r0 + TQ, :]
        masked = qi > MOBA_TOPK
        if masked:
            q = jnp.concatenate([q, block_mask(qi, q)], axis=1)
        parts = []
        for j in range(qi + 1):
            kj = k_ref[0, j * TQ:(j + 1) * TQ, :]
            if masked:
                kj = jnp.concatenate([kj, kind_ref[j * TQ:(j + 1) * TQ, :]], axis=1)
            sj = _dot_nt(q, kj)
            if j >= qi - 1:
                sj = sj + bias_ref[0, qi - j]
            parts.append(sj)
        return parts

    nxt = logits(0)
    for qi in range(n_kb):
        r0 = qi * TQ
        parts = nxt
        if qi + 1 < n_kb:
            nxt = logits(qi + 1)
        o = _softmax_pv(parts, v_ref[0, 0:r0 + TQ, :])
        z = z_ref[0, r0:r0 + TQ, :].astype(F32)
        o_ref[0, r0:r0 + TQ, :] = (o * _silu(z)).astype(BF16)


def _head_specs(s, nh):
    w = HEAD_DIM
    return [
        pl.BlockSpec((1, s, w), lambda i, h: (i, 0, h)),
        pl.BlockSpec((1, s, w), lambda i, h: (i, 0, nh + h)),
        pl.BlockSpec((1, s, w), lambda i, h: (i, 0, 2 * nh + h)),
        pl.BlockSpec((1, s, w), lambda i, h: (i, 0, 3 * nh + h)),
    ]


def _moba_attention(proj, bias):
    b, s, _ = proj.shape
    assert s // MOBA_BLOCK == 8 and MOBA_BLOCK == TQ
    ind = _block_indicator(s)
    return pl.pallas_call(
        functools.partial(_moba_kernel, seq=s),
        grid=(b, MOBA_HEADS),
        in_specs=[pl.BlockSpec((HEAD_DIM, s), lambda i, h: (0, 0)),
                  pl.BlockSpec((s, HEAD_DIM), lambda i, h: (0, 0))]
        + _head_specs(s, MOBA_HEADS)
        + [pl.BlockSpec((1, 2, TQ, TQ), lambda i, h: (h, 0, 0, 0))],
        out_specs=pl.BlockSpec((1, s, HEAD_DIM), lambda i, h: (i, 0, h)),
        out_shape=jax.ShapeDtypeStruct((b, s, D_INNER), BF16),
        compiler_params=pltpu.CompilerParams(
            dimension_semantics=("arbitrary", "arbitrary"),
            vmem_limit_bytes=VMEM_LIMIT),
        name="moba_attention",
    )(jnp.asarray(ind, BF16), jnp.asarray(ind.T, BF16), proj, proj, proj, proj, bias)


def _fox_gate_kernel(x_ref, wf_ref, bf_ref, row_ref, col_ref, *, seq):
    f = _dot_nt(x_ref[0], wf_ref[...]) + bf_ref[...]
    c = jnp.minimum(f, 0.0) - jnp.log1p(jnp.exp(-jnp.abs(f)))
    c = c.T[0:FOX_HEADS]
    lane = lax.broadcasted_iota(jnp.int32, c.shape, 1)
    d = 1
    while d < seq:
        c = c + jnp.where(lane >= d, pltpu.roll(c, d, 1), 0.0)
        d *= 2
    c = c * LOG2E
    row_ref[0] = c
    col_ref[0] = jnp.concatenate([c, jnp.zeros((HEAD_DIM - FOX_HEADS, seq), F32)], axis=0).T


def _fox_gate(h_bf3, wf, bf_row):
    b, s, d = h_bf3.shape
    return pl.pallas_call(
        functools.partial(_fox_gate_kernel, seq=s),
        grid=(b,),
        in_specs=[
            pl.BlockSpec((1, s, d), lambda i: (i, 0, 0)),
            pl.BlockSpec((HEAD_DIM, d), lambda i: (0, 0)),
            pl.BlockSpec((1, HEAD_DIM), lambda i: (0, 0)),
        ],
        out_specs=[pl.BlockSpec((1, FOX_HEADS, s), lambda i: (i, 0, 0)),
                   pl.BlockSpec((1, s, HEAD_DIM), lambda i: (i, 0, 0))],
        out_shape=[jax.ShapeDtypeStruct((b, FOX_HEADS, s), F32),
                   jax.ShapeDtypeStruct((b, s, HEAD_DIM), F32)],
        compiler_params=pltpu.CompilerParams(
            dimension_semantics=("arbitrary",),
            vmem_limit_bytes=VMEM_LIMIT),
        name="fox_gate",
    )(h_bf3, wf, bf_row)


def _causal_tile():
    r = np.arange(TQ)[:, None]
    c = np.arange(TQ)[None, :]
    return np.where(c <= r, 0.0, NEG).astype(np.float32)


def _fox_kernel(mask_ref, crow_ref, ccol_ref, q_ref, k_ref, v_ref, z_ref, o_ref, *, seq):
    def logits(qi):
        r0 = qi * TQ
        q = q_ref[0, r0:r0 + TQ, :]
        parts = []
        if qi >= 1:
            parts.append(_dot_nt(q, k_ref[0, 0:r0, :]) - crow_ref[0, 0, :, 0:r0])
        parts.append(_dot_nt(q, k_ref[0, r0:r0 + TQ, :]) - crow_ref[0, 0, :, r0:r0 + TQ] + mask_ref[...])
        return parts

    nq = seq // TQ
    head_lane = lax.broadcasted_iota(jnp.int32, (TQ, HEAD_DIM), 1) == pl.program_id(1)
    nxt = logits(0)
    for qi in range(nq):
        r0 = qi * TQ
        parts = nxt
        if qi + 1 < nq:
            nxt = logits(qi + 1)
        c_t = jnp.sum(jnp.where(head_lane, ccol_ref[0, r0:r0 + TQ, :], 0.0), axis=1, keepdims=True)
        o = _softmax_pv(parts, v_ref[0, 0:r0 + TQ, :], row_shift=c_t)
        z = z_ref[0, r0:r0 + TQ, :].astype(F32)
        o_ref[0, r0:r0 + TQ, :] = (o * _silu(z)).astype(BF16)


def _fox_attention(proj, cum_row, cum_col):
    b, s, _ = proj.shape
    crow = cum_row.reshape(b, FOX_HEADS, 1, s)
    mask = jnp.asarray(_causal_tile())
    return pl.pallas_call(
        functools.partial(_fox_kernel, seq=s),
        grid=(b, FOX_HEADS),
        in_specs=[
            pl.BlockSpec((TQ, TQ), lambda i, h: (0, 0)),
            pl.BlockSpec((1, 1, 1, s), lambda i, h: (i, h, 0, 0)),
            pl.BlockSpec((1, s, HEAD_DIM), lambda i, h: (i, 0, 0)),
        ] + _head_specs(s, FOX_HEADS),
        out_specs=pl.BlockSpec((1, s, HEAD_DIM), lambda i, h: (i, 0, h)),
        out_shape=jax.ShapeDtypeStruct((b, s, D_INNER), BF16),
        compiler_params=pltpu.CompilerParams(
            dimension_semantics=("arbitrary", "arbitrary"),
            vmem_limit_bytes=VMEM_LIMIT),
        name="fox_attention",
    )(mask, crow, cum_col, proj, proj, proj, proj)


def _col_scale():
    cs = np.ones((1, 4 * D_INNER), np.float32)
    cs[0, :D_INNER] = QK_SCALE * LOG2E
    return jnp.asarray(cs)


def kernel(x, rel_bias, w_in_0, lam_q1_0, lam_k1_0, lam_q2_0, lam_k2_0, subln_g_0, w_out_0, ln_g_0, ln_b_0, w_in_1, w_out_1, ln_g_1, ln_b_1, w_in_2, b_f_2, w_out_2, ln_g_2, ln_b_2, w_in_3, lam_q1_3, lam_k1_3, lam_q2_3, lam_k2_3, subln_g_3, w_out_3, ln_g_3, ln_b_3):
    b, s, d = x.shape
    m = b * s
    e = D_INNER
    col_scale = _col_scale()
    bias = _bias_tiles(rel_bias)

    h = x.reshape(m, d)
    h_bf = h
    layers = [
        (w_in_0, w_out_0, ln_g_0, ln_b_0),
        (w_in_1, w_out_1, ln_g_1, ln_b_1),
        (w_in_2, w_out_2, ln_g_2, ln_b_2),
        (w_in_3, w_out_3, ln_g_3, ln_b_3),
    ]
    diff_extra = {
        0: (lam_q1_0, lam_k1_0, lam_q2_0, lam_k2_0, subln_g_0),
        3: (lam_q1_3, lam_k1_3, lam_q2_3, lam_k2_3, subln_g_3),
    }
    for i, (w_in, w_out, g, beta) in enumerate(layers):
        kind = i % 3
        if kind == 2:
            w_t = w_in.T
            proj = _in_proj(h_bf, w_t, col_scale, 4 * e, w_is_nk=True)
        else:
            proj = _in_proj(h_bf, w_in, col_scale, 4 * e)
        proj = proj.reshape(b, s, 4 * e)
        if kind == 0:
            lq1, lk1, lq2, lk2, sg = diff_extra[i]
            lam_init = 0.8 - 0.6 * math.exp(-0.3 * i)
            a = _diff_attention(proj, jnp.stack([lq1, lk1, lq2, lk2]), sg, bias, lam_init)
        elif kind == 1:
            a = _moba_attention(proj, bias)
        else:
            wf = jnp.pad(w_t[4 * e:].astype(BF16), ((0, HEAD_DIM - FOX_HEADS), (0, 0)))
            bf_row = jnp.pad(b_f_2, (0, HEAD_DIM - FOX_HEADS)).reshape(1, HEAD_DIM)
            cum_row, cum_col = _fox_gate(h_bf.reshape(b, s, d), wf, bf_row)
            a = _fox_attention(proj, cum_row, cum_col)
        h, h_bf = _out_proj_ln(a.reshape(m, e), w_out.astype(BF16), h, g, beta)
    return h.reshape(b, s, d)
```

```python
import functools
import math

import jax
import jax.numpy as jnp
import numpy as np
from jax import lax
from jax.experimental import pallas as pl
from jax.experimental.pallas import tpu as pltpu

D_MODEL = 2048
D_INNER = 2048
DEPTH = 4
DIFF_HEADS = 8
MOBA_HEADS = 16
MOBA_BLOCK = 256
MOBA_TOPK = 3
FOX_HEADS = 16
HEAD_DIM = 128
REL_BUCKETS = 32
REL_MAX_DIST = 128
REL_HEADS = 16
LN_EPS = 1e-5
RMS_EPS = 1e-5
NEG = -1e30
DN_ALPHA = (2.0 * DEPTH) ** 0.25
LOG2E = math.log2(math.e)
QK_SCALE = HEAD_DIM ** -0.5

TQ = 256
VMEM_LIMIT = 56 * 1024 * 1024

F32 = jnp.float32
BF16 = jnp.bfloat16


def _dot_nt(a, b):
    return lax.dot_general(a, b, (((1,), (1,)), ((), ())), preferred_element_type=F32)


def _silu(z):
    return z / (1.0 + jnp.exp(-z))


def _inproj_kernel(x_ref, w_ref, cs_ref, o_ref, wbf_ref, *, w_is_nk):
    @pl.when(pl.program_id(1) == 0)
    def _():
        wbf_ref[...] = w_ref[...].astype(BF16)

    x = x_ref[...].astype(BF16)
    if w_is_nk:
        acc = _dot_nt(x, wbf_ref[...])
    else:
        acc = jnp.dot(x, wbf_ref[...], preferred_element_type=F32)
    o_ref[...] = (acc * cs_ref[...]).astype(o_ref.dtype)


def _in_proj(x, w, col_scale, n_out, w_is_nk=False, tn=1024):
    m, k = x.shape
    tm = 2048 if x.dtype == BF16 else 1024
    if w_is_nk:
        w_block, w_spec = (tn, k), pl.BlockSpec((tn, k), lambda n, i: (n, 0))
    else:
        w_block, w_spec = (k, tn), pl.BlockSpec((k, tn), lambda n, i: (0, n))
    return pl.pallas_call(
        functools.partial(_inproj_kernel, w_is_nk=w_is_nk),
        grid=(n_out // tn, m // tm),
        in_specs=[
            pl.BlockSpec((tm, k), lambda n, i: (i, 0)),
            w_spec,
            pl.BlockSpec((1, tn), lambda n, i: (0, n)),
        ],
        out_specs=pl.BlockSpec((tm, tn), lambda n, i: (i, n)),
        out_shape=jax.ShapeDtypeStruct((m, n_out), BF16),
        scratch_shapes=[pltpu.VMEM(w_block, BF16)],
        compiler_params=pltpu.CompilerParams(
            dimension_semantics=("arbitrary", "arbitrary"),
            vmem_limit_bytes=VMEM_LIMIT),
        name="in_proj",
    )(x, w, col_scale)


LN_CHUNKS = 4


def _data_zero(x):
    rows, n = x.shape
    x = jnp.max(x.reshape(rows // 8, 8, n), axis=0)
    bits = pltpu.bitcast(_fold_lanes([x], jnp.maximum), jnp.uint32)
    zero = lax.shift_right_logical(lax.shift_right_logical(bits, jnp.uint32(16)), jnp.uint32(16))
    return pltpu.bitcast(zero, F32)[0:1, 0:1]


def _outproj_ln_kernel(a_ref, w_ref, h_ref, g_ref, b_ref, o_ref, obf_ref, y0_ref, y1_ref, *, nblk):
    i = pl.program_id(0)
    ys = (y0_ref, y1_ref)
    tm, n = o_ref.shape
    rows, cols = tm // LN_CHUNKS, n // LN_CHUNKS

    def layer_norm(y_ref, c):
        rs = slice(c * rows, (c + 1) * rows)
        r = DN_ALPHA * h_ref[rs, :] + y_ref[rs, :]
        mu = jnp.mean(r, axis=1, keepdims=True)
        d = r - mu
        var = jnp.mean(d * d, axis=1, keepdims=True)
        out = d * lax.rsqrt(var + LN_EPS) * g_ref[...] + b_ref[...]
        o_ref[rs, :] = out
        obf_ref[rs, :] = out.astype(BF16)
        return out

    def matmul(y_ref, c, anchor=None):
        cs = slice(c * cols, (c + 1) * cols)
        if anchor is None:
            y = jnp.dot(a_ref[...], w_ref[:, cs], preferred_element_type=F32)
        else:
            head = a_ref[:, 0:HEAD_DIM] + anchor.astype(BF16)
            y = (jnp.dot(head, w_ref[0:HEAD_DIM, cs], preferred_element_type=F32)
                 + jnp.dot(a_ref[:, HEAD_DIM:], w_ref[HEAD_DIM:, cs], preferred_element_type=F32))
        y_ref[:, cs] = y

    @pl.when(i == 0)
    def _():
        for c in range(LN_CHUNKS):
            matmul(ys[0], c)

    for parity in range(2):
        @pl.when(jnp.logical_and(jnp.logical_and(i > 0, i < nblk), i % 2 == parity))
        def _():
            zero = None
            for c in range(LN_CHUNKS):
                matmul(ys[parity], c, anchor=zero)
                zero = _data_zero(layer_norm(ys[1 - parity], c))

    @pl.when(i == nblk)
    def _():
        for c in range(LN_CHUNKS):
            layer_norm(ys[(nblk - 1) % 2], c)


def _out_proj_ln(a_bf, w_bf, h, g, b, tm=512):
    m, k = a_bf.shape
    n = w_bf.shape[1]
    nblk = m // tm
    cur = lambda i: (jnp.minimum(i, nblk - 1), 0)
    prev = lambda i: (jnp.maximum(i - 1, 0), 0)
    return pl.pallas_call(
        functools.partial(_outproj_ln_kernel, nblk=nblk),
        grid=(nblk + 1,),
        in_specs=[
            pl.BlockSpec((tm, k), cur),
            pl.BlockSpec((k, n), lambda i: (0, 0)),
            pl.BlockSpec((tm, n), prev),
            pl.BlockSpec((1, n), lambda i: (0, 0)),
            pl.BlockSpec((1, n), lambda i: (0, 0)),
        ],
        out_specs=[
            pl.BlockSpec((tm, n), prev),
            pl.BlockSpec((tm, n), prev),
        ],
        out_shape=[jax.ShapeDtypeStruct((m, n), F32),
                   jax.ShapeDtypeStruct((m, n), BF16)],
        scratch_shapes=[pltpu.VMEM((tm, n), F32), pltpu.VMEM((tm, n), F32)],
        compiler_params=pltpu.CompilerParams(
            dimension_semantics=("arbitrary",),
            vmem_limit_bytes=VMEM_LIMIT),
        name="out_proj_ln",
    )(a_bf, w_bf, h, g.reshape(1, n), b.reshape(1, n))


def _bucket_tiles():
    r = np.arange(TQ)[:, None]
    c = np.arange(TQ)[None, :]
    tiles = []
    for delta in (0, 1):
        n = np.maximum(r - c + delta * TQ, 0)
        max_exact = REL_BUCKETS // 2
        nf = np.maximum(n, 1).astype(np.float32)
        large = max_exact + (np.log(nf / max_exact) / math.log(REL_MAX_DIST / max_exact)
                             * (REL_BUCKETS - max_exact)).astype(np.int32)
        large = np.minimum(large, REL_BUCKETS - 1)
        tiles.append(np.where(n < max_exact, n, large).astype(np.int32))
    return np.stack(tiles)


def _bias_kernel(tab_ref, bkt_ref, o_ref):
    c = pl.program_id(0)
    far = tab_ref[c, REL_BUCKETS - 1]
    row = lax.broadcasted_iota(jnp.int32, (TQ, TQ), 0)
    col = lax.broadcasted_iota(jnp.int32, (TQ, TQ), 1)
    for t in range(2):
        bk = bkt_ref[t]
        acc = jnp.zeros((TQ, TQ), F32)
        for j in range(REL_BUCKETS):
            acc = jnp.where(bk == j, tab_ref[c, j], acc)
        val = (acc - far) * LOG2E
        if t == 0:
            val = jnp.where(col <= row, val, NEG)
        o_ref[0, t] = val


def _bias_tiles(rel_bias):
    tab = rel_bias.T
    bkt = jnp.asarray(_bucket_tiles())
    return pl.pallas_call(
        _bias_kernel,
        grid=(REL_HEADS,),
        in_specs=[
            pl.BlockSpec(memory_space=pltpu.SMEM),
            pl.BlockSpec((2, TQ, TQ), lambda c: (0, 0, 0)),
        ],
        out_specs=pl.BlockSpec((1, 2, TQ, TQ), lambda c: (c, 0, 0, 0)),
        out_shape=jax.ShapeDtypeStruct((REL_HEADS, 2, TQ, TQ), F32),
        name="rel_bias_tiles",
    )(tab, bkt)


def _fold_lanes(parts, op):
    acc = None
    for p in parts:
        for t in range(p.shape[1] // HEAD_DIM):
            blk = p[:, t * HEAD_DIM:(t + 1) * HEAD_DIM]
            acc = blk if acc is None else op(acc, blk)
    return acc


def _softmax_pv(parts, v, row_shift=None):
    m = _fold_lanes(parts, jnp.maximum).max(axis=1, keepdims=True)
    if row_shift is None:
        shift = -m
    else:
        m_full = m + row_shift
        shift = row_shift - m_full
    ps = [jnp.exp2(p + shift) for p in parts]
    pb = jnp.concatenate([p.astype(BF16) for p in ps], axis=1) if len(ps) > 1 else ps[0].astype(BF16)
    if v.shape[1] == HEAD_DIM:
        v1 = jnp.concatenate([v, jnp.ones(v.shape, v.dtype)], axis=1)
        acc = jnp.dot(pb, v1, preferred_element_type=F32)
        return acc[:, 0:HEAD_DIM] / acc[:, HEAD_DIM:HEAD_DIM + 1]
    l = _fold_lanes(ps, jnp.add).sum(axis=1, keepdims=True)
    acc = jnp.dot(pb, v, preferred_element_type=F32)
    return acc / l


def _diff_kernel(lamv_ref, g_ref, q_ref, k_ref, v_ref, z_ref, bias_ref, o_ref, *, lam_init, seq):
    lv = lamv_ref[...]
    s1 = jnp.sum(lv[0:1] * lv[1:2], axis=1, keepdims=True)
    s2 = jnp.sum(lv[2:3] * lv[3:4], axis=1, keepdims=True)
    lam = jnp.exp(s1) - jnp.exp(s2) + lam_init

    def logits(qi, j):
        r0 = qi * TQ
        c0 = j * HEAD_DIM
        q = q_ref[0, r0:r0 + TQ, c0:c0 + HEAD_DIM]
        parts = []
        if qi >= 2:
            parts.append(_dot_nt(q, k_ref[0, 0:r0 - TQ, c0:c0 + HEAD_DIM]))
        if qi >= 1:
            parts.append(_dot_nt(q, k_ref[0, r0 - TQ:r0, c0:c0 + HEAD_DIM]) + bias_ref[j, 1])
        parts.append(_dot_nt(q, k_ref[0, r0:r0 + TQ, c0:c0 + HEAD_DIM]) + bias_ref[j, 0])
        return parts

    tasks = [(qi, j) for qi in reversed(range(seq // TQ)) for j in range(2)]
    nxt = logits(*tasks[0])
    outs = []
    for t, (qi, j) in enumerate(tasks):
        r0 = qi * TQ
        parts = nxt
        if t + 1 < len(tasks):
            nxt = logits(*tasks[t + 1])
        outs.append(_softmax_pv(parts, v_ref[0, 0:r0 + TQ, :]))
        if j == 0:
            continue
        o = outs[-2] - lam * outs[-1]
        ms = jnp.mean(o * o, axis=1, keepdims=True)
        o = o * lax.rsqrt(ms + RMS_EPS) * g_ref[...] * (1.0 - lam_init)
        z = z_ref[0, r0:r0 + TQ, :].astype(F32)
        o_ref[0, r0:r0 + TQ, :] = (o * _silu(z)).astype(BF16)


def _diff_attention(proj, lamv, subln_g, bias, lam_init):
    b, s, _ = proj.shape
    w = 2 * HEAD_DIM
    nh = D_INNER // w
    return pl.pallas_call(
        functools.partial(_diff_kernel, lam_init=lam_init, seq=s),
        grid=(b, DIFF_HEADS),
        in_specs=[
            pl.BlockSpec((4, HEAD_DIM), lambda i, h: (0, 0)),
            pl.BlockSpec((1, w), lambda i, h: (0, 0)),
            pl.BlockSpec((1, s, w), lambda i, h: (i, 0, h)),
            pl.BlockSpec((1, s, w), lambda i, h: (i, 0, nh + h)),
            pl.BlockSpec((1, s, w), lambda i, h: (i, 0, 2 * nh + h)),
            pl.BlockSpec((1, s, w), lambda i, h: (i, 0, 3 * nh + h)),
            pl.BlockSpec((2, 2, TQ, TQ), lambda i, h: (h, 0, 0, 0)),
        ],
        out_specs=pl.BlockSpec((1, s, w), lambda i, h: (i, 0, h)),
        out_shape=jax.ShapeDtypeStruct((b, s, D_INNER), BF16),
        compiler_params=pltpu.CompilerParams(
            dimension_semantics=("arbitrary", "arbitrary"),
            vmem_limit_bytes=VMEM_LIMIT),
        name="diff_attention",
    )(lamv, subln_g.reshape(1, w), proj, proj, proj, proj, bias)


def _block_indicator(seq):
    n_kb = seq // MOBA_BLOCK
    ind = np.zeros((HEAD_DIM, seq), np.float32)
    for j in range(n_kb):
        ind[j, j * MOBA_BLOCK:(j + 1) * MOBA_BLOCK] = 1.0
    return ind


def _moba_kernel(ind_ref, kind_ref, q_ref, k_ref, v_ref, z_ref, bias_ref, o_ref, *, seq):
    n_kb = seq // MOBA_BLOCK
    ksum = jnp.dot(ind_ref[...], k_ref[0], preferred_element_type=F32)
    kmean = ksum * (1.0 / MOBA_BLOCK)
    hi = kmean.astype(BF16)
    lo = (kmean - hi.astype(F32)).astype(BF16)
    km2 = jnp.concatenate([hi, lo], axis=1)
    sub = lax.broadcasted_iota(jnp.int32, (n_kb, TQ), 0)

    def block_mask(qi, q):
        gate = _dot_nt(km2, jnp.concatenate([q, q], axis=1))[0:n_kb]
        rank = jnp.zeros((n_kb, TQ), F32)
        for bp in range(qi):
            gb = gate[bp:bp + 1, :]
            rank = rank + jnp.where(sub > bp, jnp.where(gb >= gate, 1.0, 0.0), jnp.where(gb > gate, 1.0, 0.0))
        mt = jnp.where(sub >= qi, 0.0, jnp.where(rank < float(MOBA_TOPK), 0.0, NEG))
        mt = jnp.concatenate([mt, jnp.zeros((HEAD_DIM - n_kb, TQ), F32)], axis=0)
        return mt.T.astype(BF16)

    def logits(qi):
        r0 = qi * TQ
        q = q_ref[0, r0:r0 + TQ, :]
        masked = qi > MOBA_TOPK
        if masked:
            q = jnp.concatenate([q, block_mask(qi, q)], axis=1)
        parts = []
        for j in range(qi + 1):
            kj = k_ref[0, j * TQ:(j + 1) * TQ, :]
            if masked:
                kj = jnp.concatenate([kj, kind_ref[j * TQ:(j + 1) * TQ, :]], axis=1)
            sj = _dot_nt(q, kj)
            if j >= qi - 1:
                sj = sj + bias_ref[0, qi - j]
            parts.append(sj)
        return parts

    nxt = logits(n_kb - 1)
    for qi in reversed(range(n_kb)):
        r0 = qi * TQ
        parts = nxt
        if qi > 0:
            nxt = logits(qi - 1)
        o = _softmax_pv(parts, v_ref[0, 0:r0 + TQ, :])
        z = z_ref[0, r0:r0 + TQ, :].astype(F32)
        o_ref[0, r0:r0 + TQ, :] = (o * _silu(z)).astype(BF16)


def _head_specs(s, nh):
    w = HEAD_DIM
    return [
        pl.BlockSpec((1, s, w), lambda i, h: (i, 0, h)),
        pl.BlockSpec((1, s, w), lambda i, h: (i, 0, nh + h)),
        pl.BlockSpec((1, s, w), lambda i, h: (i, 0, 2 * nh + h)),
        pl.BlockSpec((1, s, w), lambda i, h: (i, 0, 3 * nh + h)),
    ]


def _moba_attention(proj, bias):
    b, s, _ = proj.shape
    assert s // MOBA_BLOCK == 8 and MOBA_BLOCK == TQ
    ind = _block_indicator(s)
    return pl.pallas_call(
        functools.partial(_moba_kernel, seq=s),
        grid=(b, MOBA_HEADS),
        in_specs=[pl.BlockSpec((HEAD_DIM, s), lambda i, h: (0, 0)),
                  pl.BlockSpec((s, HEAD_DIM), lambda i, h: (0, 0))]
        + _head_specs(s, MOBA_HEADS)
        + [pl.BlockSpec((1, 2, TQ, TQ), lambda i, h: (h, 0, 0, 0))],
        out_specs=pl.BlockSpec((1, s, HEAD_DIM), lambda i, h: (i, 0, h)),
        out_shape=jax.ShapeDtypeStruct((b, s, D_INNER), BF16),
        compiler_params=pltpu.CompilerParams(
            dimension_semantics=("arbitrary", "arbitrary"),
            vmem_limit_bytes=VMEM_LIMIT),
        name="moba_attention",
    )(jnp.asarray(ind, BF16), jnp.asarray(ind.T, BF16), proj, proj, proj, proj, bias)


def _fox_gate_kernel(x_ref, wf_ref, bf_ref, row_ref, col_ref, *, seq):
    f = _dot_nt(x_ref[0], wf_ref[...]) + bf_ref[...]
    c = jnp.minimum(f, 0.0) - jnp.log1p(jnp.exp(-jnp.abs(f)))
    c = c.T[0:FOX_HEADS]
    lane = lax.broadcasted_iota(jnp.int32, c.shape, 1)
    d = 1
    while d < seq:
        c = c + jnp.where(lane >= d, pltpu.roll(c, d, 1), 0.0)
        d *= 2
    c = c * LOG2E
    row_ref[0] = c
    col_ref[0] = jnp.concatenate([c, jnp.zeros((HEAD_DIM - FOX_HEADS, seq), F32)], axis=0).T


def _fox_gate(h_bf3, wf, bf_row):
    b, s, d = h_bf3.shape
    return pl.pallas_call(
        functools.partial(_fox_gate_kernel, seq=s),
        grid=(b,),
        in_specs=[
            pl.BlockSpec((1, s, d), lambda i: (i, 0, 0)),
            pl.BlockSpec((HEAD_DIM, d), lambda i: (0, 0)),
            pl.BlockSpec((1, HEAD_DIM), lambda i: (0, 0)),
        ],
        out_specs=[pl.BlockSpec((1, FOX_HEADS, s), lambda i: (i, 0, 0)),
                   pl.BlockSpec((1, s, HEAD_DIM), lambda i: (i, 0, 0))],
        out_shape=[jax.ShapeDtypeStruct((b, FOX_HEADS, s), F32),
                   jax.ShapeDtypeStruct((b, s, HEAD_DIM), F32)],
        compiler_params=pltpu.CompilerParams(
            dimension_semantics=("arbitrary",),
            vmem_limit_bytes=VMEM_LIMIT),
        name="fox_gate",
    )(h_bf3, wf, bf_row)


def _causal_tile():
    key = np.arange(TQ)[:, None]
    qry = np.arange(TQ)[None, :]
    return np.where(key <= qry, 0.0, NEG).astype(np.float32)


ONES_ROWS = 16


def _softmax_pv_t(parts, vt, row_shift=None):
    m = parts[0].max(axis=0, keepdims=True)
    for p in parts[1:]:
        m = jnp.maximum(m, p.max(axis=0, keepdims=True))
    if row_shift is None:
        shift = -m
    else:
        m_full = m + row_shift
        shift = row_shift - m_full
    pb = [jnp.exp2(p + shift).astype(BF16) for p in parts]
    pb = jnp.concatenate(pb, axis=0) if len(pb) > 1 else pb[0]
    acc = jnp.dot(vt, pb, preferred_element_type=F32)
    return acc[0:HEAD_DIM] / acc[HEAD_DIM:HEAD_DIM + 1]


def _value_rows(v):
    vt = v.astype(F32).T.astype(BF16)
    return jnp.concatenate([vt, jnp.ones((ONES_ROWS, v.shape[0]), BF16)], axis=0)


def _fox_kernel(mask_ref, crow_ref, ccol_ref, q_ref, k_ref, v_ref, z_ref, o_ref, *, seq):
    head_lane = lax.broadcasted_iota(jnp.int32, (seq, HEAD_DIM), 1) == pl.program_id(1)
    c_key = jnp.sum(jnp.where(head_lane, ccol_ref[0], 0.0), axis=1, keepdims=True)
    c_key = jnp.broadcast_to(c_key, (seq, HEAD_DIM))
    c_key = jnp.concatenate([c_key, c_key], axis=1)
    vt = _value_rows(v_ref[0])

    def logits(qi):
        r0 = qi * TQ
        q = q_ref[0, r0:r0 + TQ, :]
        parts = []
        if qi >= 1:
            parts.append(_dot_nt(k_ref[0, 0:r0, :], q) - c_key[0:r0])
        parts.append(_dot_nt(k_ref[0, r0:r0 + TQ, :], q) - c_key[r0:r0 + TQ] + mask_ref[...])
        return parts

    nq = seq // TQ
    nxt = logits(nq - 1)
    for qi in reversed(range(nq)):
        r0 = qi * TQ
        parts = nxt
        if qi > 0:
            nxt = logits(qi - 1)
        o = _softmax_pv_t(parts, vt[:, 0:r0 + TQ], row_shift=crow_ref[0, 0, :, r0:r0 + TQ]).T
        z = z_ref[0, r0:r0 + TQ, :].astype(F32)
        o_ref[0, r0:r0 + TQ, :] = (o * _silu(z)).astype(BF16)


def _fox_attention(proj, cum_row, cum_col):
    b, s, _ = proj.shape
    crow = cum_row.reshape(b, FOX_HEADS, 1, s)
    mask = jnp.asarray(_causal_tile())
    return pl.pallas_call(
        functools.partial(_fox_kernel, seq=s),
        grid=(b, FOX_HEADS),
        in_specs=[
            pl.BlockSpec((TQ, TQ), lambda i, h: (0, 0)),
            pl.BlockSpec((1, 1, 1, s), lambda i, h: (i, h, 0, 0)),
            pl.BlockSpec((1, s, HEAD_DIM), lambda i, h: (i, 0, 0)),
        ] + _head_specs(s, FOX_HEADS),
        out_specs=pl.BlockSpec((1, s, HEAD_DIM), lambda i, h: (i, 0, h)),
        out_shape=jax.ShapeDtypeStruct((b, s, D_INNER), BF16),
        compiler_params=pltpu.CompilerParams(
            dimension_semantics=("arbitrary", "arbitrary"),
            vmem_limit_bytes=VMEM_LIMIT),
        name="fox_attention",
    )(mask, crow, cum_col, proj, proj, proj, proj)


def _col_scale():
    cs = np.ones((1, 4 * D_INNER), np.float32)
    cs[0, :D_INNER] = QK_SCALE * LOG2E
    return jnp.asarray(cs)


def kernel(x, rel_bias, w_in_0, lam_q1_0, lam_k1_0, lam_q2_0, lam_k2_0, subln_g_0, w_out_0, ln_g_0, ln_b_0, w_in_1, w_out_1, ln_g_1, ln_b_1, w_in_2, b_f_2, w_out_2, ln_g_2, ln_b_2, w_in_3, lam_q1_3, lam_k1_3, lam_q2_3, lam_k2_3, subln_g_3, w_out_3, ln_g_3, ln_b_3):
    b, s, d = x.shape
    m = b * s
    e = D_INNER
    col_scale = _col_scale()
    bias = _bias_tiles(rel_bias)

    h = x.reshape(m, d)
    h_bf = h
    layers = [
        (w_in_0, w_out_0, ln_g_0, ln_b_0),
        (w_in_1, w_out_1, ln_g_1, ln_b_1),
        (w_in_2, w_out_2, ln_g_2, ln_b_2),
        (w_in_3, w_out_3, ln_g_3, ln_b_3),
    ]
    diff_extra = {
        0: (lam_q1_0, lam_k1_0, lam_q2_0, lam_k2_0, subln_g_0),
        3: (lam_q1_3, lam_k1_3, lam_q2_3, lam_k2_3, subln_g_3),
    }
    for i, (w_in, w_out, g, beta) in enumerate(layers):
        kind = i % 3
        if kind == 2:
            w_t = w_in.T
            proj = _in_proj(h_bf, w_t, col_scale, 4 * e, w_is_nk=True)
        else:
            proj = _in_proj(h_bf, w_in, col_scale, 4 * e)
        proj = proj.reshape(b, s, 4 * e)
        if kind == 0:
            lq1, lk1, lq2, lk2, sg = diff_extra[i]
            lam_init = 0.8 - 0.6 * math.exp(-0.3 * i)
            a = _diff_attention(proj, jnp.stack([lq1, lk1, lq2, lk2]), sg, bias, lam_init)
        elif kind == 1:
            a = _moba_attention(proj, bias)
        else:
            wf = jnp.pad(w_t[4 * e:].astype(BF16), ((0, HEAD_DIM - FOX_HEADS), (0, 0)))
            bf_row = jnp.pad(b_f_2, (0, HEAD_DIM - FOX_HEADS)).reshape(1, HEAD_DIM)
            cum_row, cum_col = _fox_gate(h_bf.reshape(b, s, d), wf, bf_row)
            a = _fox_attention(proj, cum_row, cum_col)
        h, h_bf = _out_proj_ln(a.reshape(m, e), w_out.astype(BF16), h, g, beta)
    return h.reshape(b, s, d)
```

```python
import functools
import math

import jax
import jax.numpy as jnp
import numpy as np
from jax import lax
from jax.experimental import pallas as pl
from jax.experimental.pallas import tpu as pltpu

D_MODEL = 2048
D_INNER = 2048
DEPTH = 4
DIFF_HEADS = 8
MOBA_HEADS = 16
MOBA_BLOCK = 256
MOBA_TOPK = 3
FOX_HEADS = 16
HEAD_DIM = 128
REL_BUCKETS = 32
REL_MAX_DIST = 128
REL_HEADS = 16
LN_EPS = 1e-5
RMS_EPS = 1e-5
NEG = -1e30
DN_ALPHA = (2.0 * DEPTH) ** 0.25
LOG2E = math.log2(math.e)
QK_SCALE = HEAD_DIM ** -0.5

TQ = 256
VMEM_LIMIT = 60 * 1024 * 1024

F32 = jnp.float32
BF16 = jnp.bfloat16


def _dot_nt(a, b):
    return lax.dot_general(a, b, (((1,), (1,)), ((), ())), preferred_element_type=F32)


def _silu(z):
    return z / (1.0 + jnp.exp(-z))


def _inproj_kernel(x_ref, w_ref, cs_ref, wo_ref, o_ref, wo_bf_ref, wbf_ref, *, w_is_nk):
    @pl.when(pl.program_id(1) == 0)
    def _():
        wbf_ref[...] = w_ref[...].astype(BF16)

    x = x_ref[...].astype(BF16)
    if w_is_nk:
        acc = _dot_nt(x, wbf_ref[...])
    else:
        acc = jnp.dot(x, wbf_ref[...], preferred_element_type=F32)
    o_ref[...] = (acc * cs_ref[...]).astype(o_ref.dtype)
    wo_bf_ref[...] = wo_ref[...].astype(BF16)


def _in_proj(x, w, col_scale, n_out, w_out, w_is_nk=False, tn=1024):
    m, k = x.shape
    tm = 2048 if x.dtype == BF16 else 1024
    n_tiles, m_tiles = n_out // tn, m // tm
    ko, no = w_out.shape
    slab = ko // (n_tiles * m_tiles)
    if w_is_nk:
        w_block, w_spec = (tn, k), pl.BlockSpec((tn, k), lambda n, i: (n, 0))
    else:
        w_block, w_spec = (k, tn), pl.BlockSpec((k, tn), lambda n, i: (0, n))
    return pl.pallas_call(
        functools.partial(_inproj_kernel, w_is_nk=w_is_nk),
        grid=(n_tiles, m_tiles),
        in_specs=[
            pl.BlockSpec((tm, k), lambda n, i: (i, 0)),
            w_spec,
            pl.BlockSpec((1, tn), lambda n, i: (0, n)),
            pl.BlockSpec((slab, no), lambda n, i: (n * m_tiles + i, 0)),
        ],
        out_specs=[pl.BlockSpec((tm, tn), lambda n, i: (i, n)),
                   pl.BlockSpec((slab, no), lambda n, i: (n * m_tiles + i, 0))],
        out_shape=[jax.ShapeDtypeStruct((m, n_out), BF16),
                   jax.ShapeDtypeStruct((ko, no), BF16)],
        scratch_shapes=[pltpu.VMEM(w_block, BF16)],
        compiler_params=pltpu.CompilerParams(
            dimension_semantics=("arbitrary", "arbitrary"),
            vmem_limit_bytes=VMEM_LIMIT),
        name="in_proj",
    )(x, w, col_scale, w_out)


LN_CHUNKS = 4


def _data_zero(x):
    rows, n = x.shape
    x = jnp.max(x.reshape(rows // 8, 8, n), axis=0)
    bits = pltpu.bitcast(_fold_lanes([x], jnp.maximum), jnp.uint32)
    zero = lax.shift_right_logical(lax.shift_right_logical(bits, jnp.uint32(16)), jnp.uint32(16))
    return pltpu.bitcast(zero, F32)[0:1, 0:1]


def _outproj_ln_kernel(a_ref, w_ref, h_ref, g_ref, b_ref, o_ref, obf_ref, y0_ref, y1_ref, *, nblk):
    i = pl.program_id(0)
    ys = (y0_ref, y1_ref)
    tm, n = o_ref.shape
    rows, cols = tm // LN_CHUNKS, n // LN_CHUNKS

    def layer_norm(y_ref, c):
        rs = slice(c * rows, (c + 1) * rows)
        r = DN_ALPHA * h_ref[rs, :] + y_ref[rs, :]
        mu = jnp.mean(r, axis=1, keepdims=True)
        d = r - mu
        var = jnp.mean(d * d, axis=1, keepdims=True)
        out = d * lax.rsqrt(var + LN_EPS) * g_ref[...] + b_ref[...]
        o_ref[rs, :] = out
        obf_ref[rs, :] = out.astype(BF16)
        return out

    def matmul(y_ref, c, anchor=None):
        cs = slice(c * cols, (c + 1) * cols)
        if anchor is None:
            y = jnp.dot(a_ref[...], w_ref[:, cs], preferred_element_type=F32)
        else:
            head = a_ref[:, 0:HEAD_DIM] + anchor.astype(BF16)
            y = (jnp.dot(head, w_ref[0:HEAD_DIM, cs], preferred_element_type=F32)
                 + jnp.dot(a_ref[:, HEAD_DIM:], w_ref[HEAD_DIM:, cs], preferred_element_type=F32))
        y_ref[:, cs] = y

    @pl.when(i == 0)
    def _():
        for c in range(LN_CHUNKS):
            matmul(ys[0], c)

    for parity in range(2):
        @pl.when(jnp.logical_and(jnp.logical_and(i > 0, i < nblk), i % 2 == parity))
        def _():
            zero = None
            for c in range(LN_CHUNKS):
                matmul(ys[parity], c, anchor=zero)
                zero = _data_zero(layer_norm(ys[1 - parity], c))

    @pl.when(i == nblk)
    def _():
        for c in range(LN_CHUNKS):
            layer_norm(ys[(nblk - 1) % 2], c)


def _out_proj_ln(a_bf, w_bf, h, g, b, tm=512):
    m, k = a_bf.shape
    n = w_bf.shape[1]
    nblk = m // tm
    cur = lambda i: (jnp.minimum(i, nblk - 1), 0)
    prev = lambda i: (jnp.maximum(i - 1, 0), 0)
    return pl.pallas_call(
        functools.partial(_outproj_ln_kernel, nblk=nblk),
        grid=(nblk + 1,),
        in_specs=[
            pl.BlockSpec((tm, k), cur),
            pl.BlockSpec((k, n), lambda i: (0, 0)),
            pl.BlockSpec((tm, n), prev),
            pl.BlockSpec((1, n), lambda i: (0, 0)),
            pl.BlockSpec((1, n), lambda i: (0, 0)),
        ],
        out_specs=[
            pl.BlockSpec((tm, n), prev),
            pl.BlockSpec((tm, n), prev),
        ],
        out_shape=[jax.ShapeDtypeStruct((m, n), F32),
                   jax.ShapeDtypeStruct((m, n), BF16)],
        scratch_shapes=[pltpu.VMEM((tm, n), F32), pltpu.VMEM((tm, n), F32)],
        compiler_params=pltpu.CompilerParams(
            dimension_semantics=("arbitrary",),
            vmem_limit_bytes=VMEM_LIMIT),
        name="out_proj_ln",
    )(a_bf, w_bf, h, g.reshape(1, n), b.reshape(1, n))


def _bucket_tiles():
    r = np.arange(TQ)[:, None]
    c = np.arange(TQ)[None, :]
    tiles = []
    for delta in (0, 1):
        n = np.maximum(r - c + delta * TQ, 0)
        max_exact = REL_BUCKETS // 2
        nf = np.maximum(n, 1).astype(np.float32)
        large = max_exact + (np.log(nf / max_exact) / math.log(REL_MAX_DIST / max_exact)
                             * (REL_BUCKETS - max_exact)).astype(np.int32)
        large = np.minimum(large, REL_BUCKETS - 1)
        tiles.append(np.where(n < max_exact, n, large).astype(np.int32))
    return np.stack(tiles)


def _bias_kernel(tab_ref, bkt_ref, o_ref):
    c = pl.program_id(0)
    far = tab_ref[c, REL_BUCKETS - 1]
    row = lax.broadcasted_iota(jnp.int32, (TQ, TQ), 0)
    col = lax.broadcasted_iota(jnp.int32, (TQ, TQ), 1)
    for t in range(2):
        bk = bkt_ref[t]
        acc = jnp.zeros((TQ, TQ), F32)
        for j in range(REL_BUCKETS):
            acc = jnp.where(bk == j, tab_ref[c, j], acc)
        val = (acc - far) * LOG2E
        if t == 0:
            val = jnp.where(col <= row, val, NEG)
        o_ref[0, t] = val


def _bias_tiles(rel_bias):
    tab = rel_bias.T
    bkt = jnp.asarray(_bucket_tiles())
    return pl.pallas_call(
        _bias_kernel,
        grid=(REL_HEADS,),
        in_specs=[
            pl.BlockSpec(memory_space=pltpu.SMEM),
            pl.BlockSpec((2, TQ, TQ), lambda c: (0, 0, 0)),
        ],
        out_specs=pl.BlockSpec((1, 2, TQ, TQ), lambda c: (c, 0, 0, 0)),
        out_shape=jax.ShapeDtypeStruct((REL_HEADS, 2, TQ, TQ), F32),
        name="rel_bias_tiles",
    )(tab, bkt)


def _fold_lanes(parts, op):
    acc = None
    for p in parts:
        for t in range(p.shape[1] // HEAD_DIM):
            blk = p[:, t * HEAD_DIM:(t + 1) * HEAD_DIM]
            acc = blk if acc is None else op(acc, blk)
    return acc


def _softmax_pv(parts, v, row_shift=None):
    m = _fold_lanes(parts, jnp.maximum).max(axis=1, keepdims=True)
    if row_shift is None:
        shift = -m
    else:
        m_full = m + row_shift
        shift = row_shift - m_full
    ps = [jnp.exp2(p + shift) for p in parts]
    pb = jnp.concatenate([p.astype(BF16) for p in ps], axis=1) if len(ps) > 1 else ps[0].astype(BF16)
    if v.shape[1] == HEAD_DIM:
        v1 = jnp.concatenate([v, jnp.ones(v.shape, v.dtype)], axis=1)
        acc = jnp.dot(pb, v1, preferred_element_type=F32)
        return acc[:, 0:HEAD_DIM] / acc[:, HEAD_DIM:HEAD_DIM + 1]
    l = _fold_lanes(ps, jnp.add).sum(axis=1, keepdims=True)
    acc = jnp.dot(pb, v, preferred_element_type=F32)
    return acc / l


def _diff_kernel(lamv_ref, g_ref, q_ref, k_ref, v_ref, z_ref, bias_ref, o_ref, *, lam_init, seq):
    lv = lamv_ref[...]
    s1 = jnp.sum(lv[0:1] * lv[1:2], axis=1, keepdims=True)
    s2 = jnp.sum(lv[2:3] * lv[3:4], axis=1, keepdims=True)
    lam = jnp.exp(s1) - jnp.exp(s2) + lam_init

    def logits(qi, j):
        r0 = qi * TQ
        c0 = j * HEAD_DIM
        q = q_ref[0, r0:r0 + TQ, c0:c0 + HEAD_DIM]
        parts = []
        if qi >= 2:
            parts.append(_dot_nt(q, k_ref[0, 0:r0 - TQ, c0:c0 + HEAD_DIM]))
        if qi >= 1:
            parts.append(_dot_nt(q, k_ref[0, r0 - TQ:r0, c0:c0 + HEAD_DIM]) + bias_ref[j, 1])
        parts.append(_dot_nt(q, k_ref[0, r0:r0 + TQ, c0:c0 + HEAD_DIM]) + bias_ref[j, 0])
        return parts

    tasks = [(qi, j) for qi in reversed(range(seq // TQ)) for j in range(2)]
    nxt = logits(*tasks[0])
    outs = []
    for t, (qi, j) in enumerate(tasks):
        r0 = qi * TQ
        parts = nxt
        if t + 1 < len(tasks):
            nxt = logits(*tasks[t + 1])
        outs.append(_softmax_pv(parts, v_ref[0, 0:r0 + TQ, :]))
        if j == 0:
            continue
        o = outs[-2] - lam * outs[-1]
        ms = jnp.mean(o * o, axis=1, keepdims=True)
        o = o * lax.rsqrt(ms + RMS_EPS) * g_ref[...] * (1.0 - lam_init)
        z = z_ref[0, r0:r0 + TQ, :].astype(F32)
        o_ref[0, r0:r0 + TQ, :] = (o * _silu(z)).astype(BF16)


def _diff_attention(proj, lamv, subln_g, bias, lam_init):
    b, s, _ = proj.shape
    w = 2 * HEAD_DIM
    nh = D_INNER // w
    return pl.pallas_call(
        functools.partial(_diff_kernel, lam_init=lam_init, seq=s),
        grid=(b, DIFF_HEADS),
        in_specs=[
            pl.BlockSpec((4, HEAD_DIM), lambda i, h: (0, 0)),
            pl.BlockSpec((1, w), lambda i, h: (0, 0)),
            pl.BlockSpec((1, s, w), lambda i, h: (i, 0, h)),
            pl.BlockSpec((1, s, w), lambda i, h: (i, 0, nh + h)),
            pl.BlockSpec((1, s, w), lambda i, h: (i, 0, 2 * nh + h)),
            pl.BlockSpec((1, s, w), lambda i, h: (i, 0, 3 * nh + h)),
            pl.BlockSpec((2, 2, TQ, TQ), lambda i, h: (h, 0, 0, 0)),
        ],
        out_specs=pl.BlockSpec((1, s, w), lambda i, h: (i, 0, h)),
        out_shape=jax.ShapeDtypeStruct((b, s, D_INNER), BF16),
        compiler_params=pltpu.CompilerParams(
            dimension_semantics=("arbitrary", "arbitrary"),
            vmem_limit_bytes=VMEM_LIMIT),
        name="diff_attention",
    )(lamv, subln_g.reshape(1, w), proj, proj, proj, proj, bias)


def _block_indicator(seq):
    n_kb = seq // MOBA_BLOCK
    ind = np.zeros((HEAD_DIM, seq), np.float32)
    for j in range(n_kb):
        ind[j, j * MOBA_BLOCK:(j + 1) * MOBA_BLOCK] = 1.0
    return ind


def _moba_kernel(ind_ref, kind_ref, q_ref, k_ref, v_ref, z_ref, bias_ref, o_ref, *, seq):
    n_kb = seq // MOBA_BLOCK
    ksum = jnp.dot(ind_ref[...], k_ref[0], preferred_element_type=F32)
    kmean = ksum * (1.0 / MOBA_BLOCK)
    hi = kmean.astype(BF16)
    lo = (kmean - hi.astype(F32)).astype(BF16)
    km2 = jnp.concatenate([hi, lo], axis=1)
    sub = lax.broadcasted_iota(jnp.int32, (n_kb, TQ), 0)

    def block_mask(qi, q):
        gate = _dot_nt(km2, jnp.concatenate([q, q], axis=1))[0:n_kb]
        rank = jnp.zeros((n_kb, TQ), F32)
        for bp in range(qi):
            gb = gate[bp:bp + 1, :]
            rank = rank + jnp.where(sub > bp, jnp.where(gb >= gate, 1.0, 0.0), jnp.where(gb > gate, 1.0, 0.0))
        mt = jnp.where(sub >= qi, 0.0, jnp.where(rank < float(MOBA_TOPK), 0.0, NEG))
        mt = jnp.concatenate([mt, jnp.zeros((HEAD_DIM - n_kb, TQ), F32)], axis=0)
        return mt.T.astype(BF16)

    def logits(qi):
        r0 = qi * TQ
        q = q_ref[0, r0:r0 + TQ, :]
        masked = qi > MOBA_TOPK
        if masked:
            q = jnp.concatenate([q, block_mask(qi, q)], axis=1)
        parts = []
        for j in range(qi + 1):
            kj = k_ref[0, j * TQ:(j + 1) * TQ, :]
            if masked:
                kj = jnp.concatenate([kj, kind_ref[j * TQ:(j + 1) * TQ, :]], axis=1)
            sj = _dot_nt(q, kj)
            if j >= qi - 1:
                sj = sj + bias_ref[0, qi - j]
            parts.append(sj)
        return parts

    nxt = logits(n_kb - 1)
    for qi in reversed(range(n_kb)):
        r0 = qi * TQ
        parts = nxt
        if qi > 0:
            nxt = logits(qi - 1)
        o = _softmax_pv(parts, v_ref[0, 0:r0 + TQ, :])
        z = z_ref[0, r0:r0 + TQ, :].astype(F32)
        o_ref[0, r0:r0 + TQ, :] = (o * _silu(z)).astype(BF16)


def _head_specs(s, nh):
    w = HEAD_DIM
    return [
        pl.BlockSpec((1, s, w), lambda i, h: (i, 0, h)),
        pl.BlockSpec((1, s, w), lambda i, h: (i, 0, nh + h)),
        pl.BlockSpec((1, s, w), lambda i, h: (i, 0, 2 * nh + h)),
        pl.BlockSpec((1, s, w), lambda i, h: (i, 0, 3 * nh + h)),
    ]


def _moba_attention(proj, bias):
    b, s, _ = proj.shape
    assert s // MOBA_BLOCK == 8 and MOBA_BLOCK == TQ
    ind = _block_indicator(s)
    return pl.pallas_call(
        functools.partial(_moba_kernel, seq=s),
        grid=(b, MOBA_HEADS),
        in_specs=[pl.BlockSpec((HEAD_DIM, s), lambda i, h: (0, 0)),
                  pl.BlockSpec((s, HEAD_DIM), lambda i, h: (0, 0))]
        + _head_specs(s, MOBA_HEADS)
        + [pl.BlockSpec((1, 2, TQ, TQ), lambda i, h: (h, 0, 0, 0))],
        out_specs=pl.BlockSpec((1, s, HEAD_DIM), lambda i, h: (i, 0, h)),
        out_shape=jax.ShapeDtypeStruct((b, s, D_INNER), BF16),
        compiler_params=pltpu.CompilerParams(
            dimension_semantics=("arbitrary", "arbitrary"),
            vmem_limit_bytes=VMEM_LIMIT),
        name="moba_attention",
    )(jnp.asarray(ind, BF16), jnp.asarray(ind.T, BF16), proj, proj, proj, proj, bias)


def _fox_gate_kernel(x_ref, wf_ref, bf_ref, row_ref, col_ref, *, seq):
    f = _dot_nt(x_ref[0], wf_ref[...]) + bf_ref[...]
    c = jnp.minimum(f, 0.0) - jnp.log1p(jnp.exp(-jnp.abs(f)))
    c = c.T[0:FOX_HEADS]
    lane = lax.broadcasted_iota(jnp.int32, c.shape, 1)
    d = 1
    while d < seq:
        c = c + jnp.where(lane >= d, pltpu.roll(c, d, 1), 0.0)
        d *= 2
    c = c * LOG2E
    row_ref[0] = c
    col_ref[0] = jnp.concatenate([c, jnp.zeros((HEAD_DIM - FOX_HEADS, seq), F32)], axis=0).T


def _fox_gate(h_bf3, wf, bf_row):
    b, s, d = h_bf3.shape
    return pl.pallas_call(
        functools.partial(_fox_gate_kernel, seq=s),
        grid=(b,),
        in_specs=[
            pl.BlockSpec((1, s, d), lambda i: (i, 0, 0)),
            pl.BlockSpec((HEAD_DIM, d), lambda i: (0, 0)),
            pl.BlockSpec((1, HEAD_DIM), lambda i: (0, 0)),
        ],
        out_specs=[pl.BlockSpec((1, FOX_HEADS, s), lambda i: (i, 0, 0)),
                   pl.BlockSpec((1, s, HEAD_DIM), lambda i: (i, 0, 0))],
        out_shape=[jax.ShapeDtypeStruct((b, FOX_HEADS, s), F32),
                   jax.ShapeDtypeStruct((b, s, HEAD_DIM), F32)],
        compiler_params=pltpu.CompilerParams(
            dimension_semantics=("arbitrary",),
            vmem_limit_bytes=VMEM_LIMIT),
        name="fox_gate",
    )(h_bf3, wf, bf_row)


def _causal_tile():
    key = np.arange(TQ)[:, None]
    qry = np.arange(TQ)[None, :]
    return np.where(key <= qry, 0.0, NEG).astype(np.float32)


ONES_ROWS = 16


def _softmax_pv_t(parts, vt, row_shift=None):
    m = parts[0].max(axis=0, keepdims=True)
    for p in parts[1:]:
        m = jnp.maximum(m, p.max(axis=0, keepdims=True))
    if row_shift is None:
        shift = -m
    else:
        m_full = m + row_shift
        shift = row_shift - m_full
    pb = [jnp.exp2(p + shift).astype(BF16) for p in parts]
    pb = jnp.concatenate(pb, axis=0) if len(pb) > 1 else pb[0]
    acc = jnp.dot(vt, pb, preferred_element_type=F32)
    return acc[0:HEAD_DIM] / acc[HEAD_DIM:HEAD_DIM + 1]


def _value_rows(v):
    vt = v.astype(F32).T.astype(BF16)
    return jnp.concatenate([vt, jnp.ones((ONES_ROWS, v.shape[0]), BF16)], axis=0)


def _fox_kernel(mask_ref, crow_ref, ccol_ref, q_ref, k_ref, v_ref, z_ref, o_ref, *, seq):
    head_lane = lax.broadcasted_iota(jnp.int32, (seq, HEAD_DIM), 1) == pl.program_id(1)
    c_key = jnp.sum(jnp.where(head_lane, ccol_ref[0], 0.0), axis=1, keepdims=True)
    c_key = jnp.broadcast_to(c_key, (seq, HEAD_DIM))
    c_key = jnp.concatenate([c_key, c_key], axis=1)
    vt = _value_rows(v_ref[0])

    def logits(qi):
        r0 = qi * TQ
        q = q_ref[0, r0:r0 + TQ, :]
        parts = []
        if qi >= 1:
            parts.append(_dot_nt(k_ref[0, 0:r0, :], q) - c_key[0:r0])
        parts.append(_dot_nt(k_ref[0, r0:r0 + TQ, :], q) - c_key[r0:r0 + TQ] + mask_ref[...])
        return parts

    nq = seq // TQ
    nxt = logits(nq - 1)
    for qi in reversed(range(nq)):
        r0 = qi * TQ
        parts = nxt
        if qi > 0:
            nxt = logits(qi - 1)
        o = _softmax_pv_t(parts, vt[:, 0:r0 + TQ], row_shift=crow_ref[0, 0, :, r0:r0 + TQ]).T
        z = z_ref[0, r0:r0 + TQ, :].astype(F32)
        o_ref[0, r0:r0 + TQ, :] = (o * _silu(z)).astype(BF16)


def _fox_attention(proj, cum_row, cum_col):
    b, s, _ = proj.shape
    crow = cum_row.reshape(b, FOX_HEADS, 1, s)
    mask = jnp.asarray(_causal_tile())
    return pl.pallas_call(
        functools.partial(_fox_kernel, seq=s),
        grid=(b, FOX_HEADS),
        in_specs=[
            pl.BlockSpec((TQ, TQ), lambda i, h: (0, 0)),
            pl.BlockSpec((1, 1, 1, s), lambda i, h: (i, h, 0, 0)),
            pl.BlockSpec((1, s, HEAD_DIM), lambda i, h: (i, 0, 0)),
        ] + _head_specs(s, FOX_HEADS),
        out_specs=pl.BlockSpec((1, s, HEAD_DIM), lambda i, h: (i, 0, h)),
        out_shape=jax.ShapeDtypeStruct((b, s, D_INNER), BF16),
        compiler_params=pltpu.CompilerParams(
            dimension_semantics=("arbitrary", "arbitrary"),
            vmem_limit_bytes=VMEM_LIMIT),
        name="fox_attention",
    )(mask, crow, cum_col, proj, proj, proj, proj)


def _col_scale():
    cs = np.ones((1, 4 * D_INNER), np.float32)
    cs[0, :D_INNER] = QK_SCALE * LOG2E
    return jnp.asarray(cs)


def kernel(x, rel_bias, w_in_0, lam_q1_0, lam_k1_0, lam_q2_0, lam_k2_0, subln_g_0, w_out_0, ln_g_0, ln_b_0, w_in_1, w_out_1, ln_g_1, ln_b_1, w_in_2, b_f_2, w_out_2, ln_g_2, ln_b_2, w_in_3, lam_q1_3, lam_k1_3, lam_q2_3, lam_k2_3, subln_g_3, w_out_3, ln_g_3, ln_b_3):
    b, s, d = x.shape
    m = b * s
    e = D_INNER
    col_scale = _col_scale()
    bias = _bias_tiles(rel_bias)

    h = x.reshape(m, d)
    h_bf = h
    layers = [
        (w_in_0, w_out_0, ln_g_0, ln_b_0),
        (w_in_1, w_out_1, ln_g_1, ln_b_1),
        (w_in_2, w_out_2, ln_g_2, ln_b_2),
        (w_in_3, w_out_3, ln_g_3, ln_b_3),
    ]
    diff_extra = {
        0: (lam_q1_0, lam_k1_0, lam_q2_0, lam_k2_0, subln_g_0),
        3: (lam_q1_3, lam_k1_3, lam_q2_3, lam_k2_3, subln_g_3),
    }
    for i, (w_in, w_out, g, beta) in enumerate(layers):
        kind = i % 3
        if kind == 2:
            w_t = w_in.T
            proj, w_out_bf = _in_proj(h_bf, w_t, col_scale, 4 * e, w_out, w_is_nk=True)
        else:
            proj, w_out_bf = _in_proj(h_bf, w_in, col_scale, 4 * e, w_out)
        proj = proj.reshape(b, s, 4 * e)
        if kind == 0:
            lq1, lk1, lq2, lk2, sg = diff_extra[i]
            lam_init = 0.8 - 0.6 * math.exp(-0.3 * i)
            a = _diff_attention(proj, jnp.stack([lq1, lk1, lq2, lk2]), sg, bias, lam_init)
        elif kind == 1:
            a = _moba_attention(proj, bias)
        else:
            wf = jnp.pad(w_t[4 * e:].astype(BF16), ((0, HEAD_DIM - FOX_HEADS), (0, 0)))
            bf_row = jnp.pad(b_f_2, (0, HEAD_DIM - FOX_HEADS)).reshape(1, HEAD_DIM)
            cum_row, cum_col = _fox_gate(h_bf.reshape(b, s, d), wf, bf_row)
            a = _fox_attention(proj, cum_row, cum_col)
        h, h_bf = _out_proj_ln(a.reshape(m, e), w_out_bf, h, g, beta)
    return h.reshape(b, s, d)
```

```python
import functools
import math

import jax
import jax.numpy as jnp
import numpy as np
from jax import lax
from jax.experimental import pallas as pl
from jax.experimental.pallas import tpu as pltpu

D_MODEL = 2048
D_INNER = 2048
DEPTH = 4
DIFF_HEADS = 8
MOBA_HEADS = 16
MOBA_BLOCK = 256
MOBA_TOPK = 3
FOX_HEADS = 16
HEAD_DIM = 128
REL_BUCKETS = 32
REL_MAX_DIST = 128
REL_HEADS = 16
LN_EPS = 1e-5
RMS_EPS = 1e-5
NEG = -1e30
DN_ALPHA = (2.0 * DEPTH) ** 0.25
LOG2E = math.log2(math.e)
QK_SCALE = HEAD_DIM ** -0.5

TQ = 256
VMEM_LIMIT = 60 * 1024 * 1024

F32 = jnp.float32
BF16 = jnp.bfloat16


def _dot_nt(a, b):
    return lax.dot_general(a, b, (((1,), (1,)), ((), ())), preferred_element_type=F32)


def _silu(z):
    return z / (1.0 + jnp.exp(-z))


def _inproj_kernel(x_ref, w_ref, cs_ref, wo_ref, o_ref, wo_bf_ref, wbf_ref, *, w_is_nk):
    @pl.when(pl.program_id(1) == 0)
    def _():
        wbf_ref[...] = w_ref[...].astype(BF16)

    x = x_ref[...].astype(BF16)
    if w_is_nk:
        acc = _dot_nt(x, wbf_ref[...])
    else:
        acc = jnp.dot(x, wbf_ref[...], preferred_element_type=F32)
    o_ref[...] = (acc * cs_ref[...]).astype(o_ref.dtype)
    wo_bf_ref[...] = wo_ref[...].astype(BF16)


def _in_proj(x, w, col_scale, n_out, w_out, w_is_nk=False):
    m, k = x.shape
    tm, tn = (2048, 1024) if x.dtype == BF16 else (512, 2048)
    n_tiles, m_tiles = n_out // tn, m // tm
    ko, no = w_out.shape
    slab = ko // (n_tiles * m_tiles)
    if w_is_nk:
        w_block, w_spec = (tn, k), pl.BlockSpec((tn, k), lambda n, i: (n, 0))
    else:
        w_block, w_spec = (k, tn), pl.BlockSpec((k, tn), lambda n, i: (0, n))
    return pl.pallas_call(
        functools.partial(_inproj_kernel, w_is_nk=w_is_nk),
        grid=(n_tiles, m_tiles),
        in_specs=[
            pl.BlockSpec((tm, k), lambda n, i: (i, 0)),
            w_spec,
            pl.BlockSpec((1, tn), lambda n, i: (0, n)),
            pl.BlockSpec((slab, no), lambda n, i: (n * m_tiles + i, 0)),
        ],
        out_specs=[pl.BlockSpec((tm, tn), lambda n, i: (i, n)),
                   pl.BlockSpec((slab, no), lambda n, i: (n * m_tiles + i, 0))],
        out_shape=[jax.ShapeDtypeStruct((m, n_out), BF16),
                   jax.ShapeDtypeStruct((ko, no), BF16)],
        scratch_shapes=[pltpu.VMEM(w_block, BF16)],
        compiler_params=pltpu.CompilerParams(
            dimension_semantics=("arbitrary", "arbitrary"),
            vmem_limit_bytes=VMEM_LIMIT),
        name="in_proj",
    )(x, w, col_scale, w_out)


LN_CHUNKS = 4


def _data_zero(x):
    rows, n = x.shape
    x = jnp.max(x.reshape(rows // 8, 8, n), axis=0)
    bits = pltpu.bitcast(_fold_lanes([x], jnp.maximum), jnp.uint32)
    zero = lax.shift_right_logical(lax.shift_right_logical(bits, jnp.uint32(16)), jnp.uint32(16))
    return pltpu.bitcast(zero, F32)[0:1, 0:1]


def _outproj_ln_kernel(a_ref, w_ref, h_ref, g_ref, b_ref, o_ref, obf_ref, y0_ref, y1_ref, *, nblk):
    i = pl.program_id(0)
    ys = (y0_ref, y1_ref)
    tm, n = o_ref.shape
    rows, cols = tm // LN_CHUNKS, n // LN_CHUNKS

    def layer_norm(y_ref, c):
        rs = slice(c * rows, (c + 1) * rows)
        r = DN_ALPHA * h_ref[rs, :] + y_ref[rs, :]
        mu = jnp.mean(r, axis=1, keepdims=True)
        d = r - mu
        var = jnp.mean(d * d, axis=1, keepdims=True)
        out = d * lax.rsqrt(var + LN_EPS) * g_ref[...] + b_ref[...]
        o_ref[rs, :] = out
        obf_ref[rs, :] = out.astype(BF16)
        return out

    def matmul(y_ref, c, anchor=None):
        cs = slice(c * cols, (c + 1) * cols)
        if anchor is None:
            y = jnp.dot(a_ref[...], w_ref[:, cs], preferred_element_type=F32)
        else:
            head = a_ref[:, 0:HEAD_DIM] + anchor.astype(BF16)
            y = (jnp.dot(head, w_ref[0:HEAD_DIM, cs], preferred_element_type=F32)
                 + jnp.dot(a_ref[:, HEAD_DIM:], w_ref[HEAD_DIM:, cs], preferred_element_type=F32))
        y_ref[:, cs] = y

    @pl.when(i == 0)
    def _():
        for c in range(LN_CHUNKS):
            matmul(ys[0], c)

    for parity in range(2):
        @pl.when(jnp.logical_and(jnp.logical_and(i > 0, i < nblk), i % 2 == parity))
        def _():
            zero = None
            for c in range(LN_CHUNKS):
                matmul(ys[parity], c, anchor=zero)
                zero = _data_zero(layer_norm(ys[1 - parity], c))

    @pl.when(i == nblk)
    def _():
        for c in range(LN_CHUNKS):
            layer_norm(ys[(nblk - 1) % 2], c)


def _out_proj_ln(a_bf, w_bf, h, g, b, tm=512):
    m, k = a_bf.shape
    n = w_bf.shape[1]
    nblk = m // tm
    cur = lambda i: (jnp.minimum(i, nblk - 1), 0)
    prev = lambda i: (jnp.maximum(i - 1, 0), 0)
    return pl.pallas_call(
        functools.partial(_outproj_ln_kernel, nblk=nblk),
        grid=(nblk + 1,),
        in_specs=[
            pl.BlockSpec((tm, k), cur),
            pl.BlockSpec((k, n), lambda i: (0, 0)),
            pl.BlockSpec((tm, n), prev),
            pl.BlockSpec((1, n), lambda i: (0, 0)),
            pl.BlockSpec((1, n), lambda i: (0, 0)),
        ],
        out_specs=[
            pl.BlockSpec((tm, n), prev),
            pl.BlockSpec((tm, n), prev),
        ],
        out_shape=[jax.ShapeDtypeStruct((m, n), F32),
                   jax.ShapeDtypeStruct((m, n), BF16)],
        scratch_shapes=[pltpu.VMEM((tm, n), F32), pltpu.VMEM((tm, n), F32)],
        compiler_params=pltpu.CompilerParams(
            dimension_semantics=("arbitrary",),
            vmem_limit_bytes=VMEM_LIMIT),
        name="out_proj_ln",
    )(a_bf, w_bf, h, g.reshape(1, n), b.reshape(1, n))


BIAS_SPAN = 4 * TQ


def _bucket_profile():
    n = np.maximum(2 * TQ - np.arange(BIAS_SPAN), 0)
    max_exact = REL_BUCKETS // 2
    nf = np.maximum(n, 1).astype(np.float32)
    large = max_exact + (np.log(nf / max_exact) / math.log(REL_MAX_DIST / max_exact)
                         * (REL_BUCKETS - max_exact)).astype(np.int32)
    large = np.minimum(large, REL_BUCKETS - 1)
    return np.where(n < max_exact, n, large).astype(np.int32).reshape(1, BIAS_SPAN)


def _bias_kernel(tab_ref, bkt_ref, o_ref):
    c = pl.program_id(0)
    far = tab_ref[c, REL_BUCKETS - 1]
    bk = bkt_ref[...]
    acc = jnp.zeros((1, BIAS_SPAN), F32)
    for j in range(REL_BUCKETS):
        acc = jnp.where(bk == j, tab_ref[c, j], acc)
    lane = lax.broadcasted_iota(jnp.int32, (1, BIAS_SPAN), 1)
    prof = jnp.where(lane > 2 * TQ, NEG, (acc - far) * LOG2E)
    rolled = pltpu.roll(jnp.broadcast_to(prof, (TQ, BIAS_SPAN)), 0, 1, stride=1, stride_axis=0)
    o_ref[0, 0] = rolled[:, 2 * TQ:3 * TQ]
    o_ref[0, 1] = rolled[:, TQ:2 * TQ]


def _bias_tiles(rel_bias):
    tab = rel_bias.T
    bkt = jnp.asarray(_bucket_profile())
    return pl.pallas_call(
        _bias_kernel,
        grid=(REL_HEADS,),
        in_specs=[
            pl.BlockSpec(memory_space=pltpu.SMEM),
            pl.BlockSpec((1, BIAS_SPAN), lambda c: (0, 0)),
        ],
        out_specs=pl.BlockSpec((1, 2, TQ, TQ), lambda c: (c, 0, 0, 0)),
        out_shape=jax.ShapeDtypeStruct((REL_HEADS, 2, TQ, TQ), F32),
        name="rel_bias_tiles",
    )(tab, bkt)


def _fold_lanes(parts, op):
    acc = None
    for p in parts:
        for t in range(p.shape[1] // HEAD_DIM):
            blk = p[:, t * HEAD_DIM:(t + 1) * HEAD_DIM]
            acc = blk if acc is None else op(acc, blk)
    return acc


def _softmax_pv(parts, v, row_shift=None):
    m = _fold_lanes(parts, jnp.maximum).max(axis=1, keepdims=True)
    if row_shift is None:
        shift = -m
    else:
        m_full = m + row_shift
        shift = row_shift - m_full
    ps = [jnp.exp2(p + shift) for p in parts]
    pb = jnp.concatenate([p.astype(BF16) for p in ps], axis=1) if len(ps) > 1 else ps[0].astype(BF16)
    if v.shape[1] == HEAD_DIM:
        v1 = jnp.concatenate([v, jnp.ones(v.shape, v.dtype)], axis=1)
        acc = jnp.dot(pb, v1, preferred_element_type=F32)
        return acc[:, 0:HEAD_DIM] / acc[:, HEAD_DIM:HEAD_DIM + 1]
    l = _fold_lanes(ps, jnp.add).sum(axis=1, keepdims=True)
    acc = jnp.dot(pb, v, preferred_element_type=F32)
    return acc / l


def _diff_kernel(lamv_ref, g_ref, q_ref, k_ref, v_ref, z_ref, bias_ref, o_ref, *, lam_init, seq):
    lv = lamv_ref[...]
    s1 = jnp.sum(lv[0:1] * lv[1:2], axis=1, keepdims=True)
    s2 = jnp.sum(lv[2:3] * lv[3:4], axis=1, keepdims=True)
    lam = jnp.exp(s1) - jnp.exp(s2) + lam_init

    def logits(qi, j):
        r0 = qi * TQ
        c0 = j * HEAD_DIM
        q = q_ref[0, r0:r0 + TQ, c0:c0 + HEAD_DIM]
        parts = []
        if qi >= 2:
            parts.append(_dot_nt(q, k_ref[0, 0:r0 - TQ, c0:c0 + HEAD_DIM]))
        if qi >= 1:
            parts.append(_dot_nt(q, k_ref[0, r0 - TQ:r0, c0:c0 + HEAD_DIM]) + bias_ref[j, 1])
        parts.append(_dot_nt(q, k_ref[0, r0:r0 + TQ, c0:c0 + HEAD_DIM]) + bias_ref[j, 0])
        return parts

    tasks = [(qi, j) for qi in reversed(range(seq // TQ)) for j in range(2)]
    nxt = logits(*tasks[0])
    outs = []
    for t, (qi, j) in enumerate(tasks):
        r0 = qi * TQ
        parts = nxt
        if t + 1 < len(tasks):
            nxt = logits(*tasks[t + 1])
        outs.append(_softmax_pv(parts, v_ref[0, 0:r0 + TQ, :]))
        if j == 0:
            continue
        o = outs[-2] - lam * outs[-1]
        ms = jnp.mean(o * o, axis=1, keepdims=True)
        o = o * lax.rsqrt(ms + RMS_EPS) * g_ref[...] * (1.0 - lam_init)
        z = z_ref[0, r0:r0 + TQ, :].astype(F32)
        o_ref[0, r0:r0 + TQ, :] = (o * _silu(z)).astype(BF16)


def _diff_attention(proj, lamv, subln_g, bias, lam_init):
    b, s, _ = proj.shape
    w = 2 * HEAD_DIM
    nh = D_INNER // w
    return pl.pallas_call(
        functools.partial(_diff_kernel, lam_init=lam_init, seq=s),
        grid=(b, DIFF_HEADS),
        in_specs=[
            pl.BlockSpec((4, HEAD_DIM), lambda i, h: (0, 0)),
            pl.BlockSpec((1, w), lambda i, h: (0, 0)),
            pl.BlockSpec((1, s, w), lambda i, h: (i, 0, h)),
            pl.BlockSpec((1, s, w), lambda i, h: (i, 0, nh + h)),
            pl.BlockSpec((1, s, w), lambda i, h: (i, 0, 2 * nh + h)),
            pl.BlockSpec((1, s, w), lambda i, h: (i, 0, 3 * nh + h)),
            pl.BlockSpec((2, 2, TQ, TQ), lambda i, h: (h, 0, 0, 0)),
        ],
        out_specs=pl.BlockSpec((1, s, w), lambda i, h: (i, 0, h)),
        out_shape=jax.ShapeDtypeStruct((b, s, D_INNER), BF16),
        compiler_params=pltpu.CompilerParams(
            dimension_semantics=("arbitrary", "arbitrary"),
            vmem_limit_bytes=VMEM_LIMIT),
        name="diff_attention",
    )(lamv, subln_g.reshape(1, w), proj, proj, proj, proj, bias)


def _block_indicator(seq):
    n_kb = seq // MOBA_BLOCK
    ind = np.zeros((HEAD_DIM, seq), np.float32)
    for j in range(n_kb):
        ind[j, j * MOBA_BLOCK:(j + 1) * MOBA_BLOCK] = 1.0
    return ind


def _moba_kernel(ind_ref, kind_ref, q_ref, k_ref, v_ref, z_ref, bias_ref, o_ref, *, seq):
    n_kb = seq // MOBA_BLOCK
    ksum = jnp.dot(ind_ref[...], k_ref[0], preferred_element_type=F32)
    kmean = ksum * (1.0 / MOBA_BLOCK)
    hi = kmean.astype(BF16)
    lo = (kmean - hi.astype(F32)).astype(BF16)
    km2 = jnp.concatenate([hi, lo], axis=1)
    sub = lax.broadcasted_iota(jnp.int32, (n_kb, TQ), 0)

    def block_mask(qi, q):
        gate = _dot_nt(km2, jnp.concatenate([q, q], axis=1))[0:n_kb]
        rank = jnp.zeros((n_kb, TQ), F32)
        for bp in range(qi):
            gb = gate[bp:bp + 1, :]
            rank = rank + jnp.where(sub > bp, jnp.where(gb >= gate, 1.0, 0.0), jnp.where(gb > gate, 1.0, 0.0))
        mt = jnp.where(sub >= qi, 0.0, jnp.where(rank < float(MOBA_TOPK), 0.0, NEG))
        mt = jnp.concatenate([mt, jnp.zeros((HEAD_DIM - n_kb, TQ), F32)], axis=0)
        return mt.T.astype(BF16)

    def logits(qi):
        r0 = qi * TQ
        q = q_ref[0, r0:r0 + TQ, :]
        masked = qi > MOBA_TOPK
        if masked:
            q = jnp.concatenate([q, block_mask(qi, q)], axis=1)
        parts = []
        for j in range(qi + 1):
            kj = k_ref[0, j * TQ:(j + 1) * TQ, :]
            if masked:
                kj = jnp.concatenate([kj, kind_ref[j * TQ:(j + 1) * TQ, :]], axis=1)
            sj = _dot_nt(q, kj)
            if j >= qi - 1:
                sj = sj + bias_ref[0, qi - j]
            parts.append(sj)
        return parts

    nxt = logits(n_kb - 1)
    for qi in reversed(range(n_kb)):
        r0 = qi * TQ
        parts = nxt
        if qi > 0:
            nxt = logits(qi - 1)
        o = _softmax_pv(parts, v_ref[0, 0:r0 + TQ, :])
        z = z_ref[0, r0:r0 + TQ, :].astype(F32)
        o_ref[0, r0:r0 + TQ, :] = (o * _silu(z)).astype(BF16)


def _head_specs(s, nh):
    w = HEAD_DIM
    return [
        pl.BlockSpec((1, s, w), lambda i, h: (i, 0, h)),
        pl.BlockSpec((1, s, w), lambda i, h: (i, 0, nh + h)),
        pl.BlockSpec((1, s, w), lambda i, h: (i, 0, 2 * nh + h)),
        pl.BlockSpec((1, s, w), lambda i, h: (i, 0, 3 * nh + h)),
    ]


def _moba_attention(proj, bias):
    b, s, _ = proj.shape
    assert s // MOBA_BLOCK == 8 and MOBA_BLOCK == TQ
    ind = _block_indicator(s)
    return pl.pallas_call(
        functools.partial(_moba_kernel, seq=s),
        grid=(b, MOBA_HEADS),
        in_specs=[pl.BlockSpec((HEAD_DIM, s), lambda i, h: (0, 0)),
                  pl.BlockSpec((s, HEAD_DIM), lambda i, h: (0, 0))]
        + _head_specs(s, MOBA_HEADS)
        + [pl.BlockSpec((1, 2, TQ, TQ), lambda i, h: (h, 0, 0, 0))],
        out_specs=pl.BlockSpec((1, s, HEAD_DIM), lambda i, h: (i, 0, h)),
        out_shape=jax.ShapeDtypeStruct((b, s, D_INNER), BF16),
        compiler_params=pltpu.CompilerParams(
            dimension_semantics=("arbitrary", "arbitrary"),
            vmem_limit_bytes=VMEM_LIMIT),
        name="moba_attention",
    )(jnp.asarray(ind, BF16), jnp.asarray(ind.T, BF16), proj, proj, proj, proj, bias)


def _fox_gate_kernel(x_ref, wf_ref, bf_ref, row_ref, col_ref, *, seq):
    f = _dot_nt(x_ref[0], wf_ref[...]) + bf_ref[...]
    c = jnp.minimum(f, 0.0) - jnp.log1p(jnp.exp(-jnp.abs(f)))
    c = c.T[0:FOX_HEADS]
    lane = lax.broadcasted_iota(jnp.int32, c.shape, 1)
    d = 1
    while d < seq:
        c = c + jnp.where(lane >= d, pltpu.roll(c, d, 1), 0.0)
        d *= 2
    c = c * LOG2E
    row_ref[0] = c
    col_ref[0] = jnp.concatenate([c, jnp.zeros((HEAD_DIM - FOX_HEADS, seq), F32)], axis=0).T


def _fox_gate(h_bf3, wf, bf_row):
    b, s, d = h_bf3.shape
    return pl.pallas_call(
        functools.partial(_fox_gate_kernel, seq=s),
        grid=(b,),
        in_specs=[
            pl.BlockSpec((1, s, d), lambda i: (i, 0, 0)),
            pl.BlockSpec((HEAD_DIM, d), lambda i: (0, 0)),
            pl.BlockSpec((1, HEAD_DIM), lambda i: (0, 0)),
        ],
        out_specs=[pl.BlockSpec((1, FOX_HEADS, s), lambda i: (i, 0, 0)),
                   pl.BlockSpec((1, s, HEAD_DIM), lambda i: (i, 0, 0))],
        out_shape=[jax.ShapeDtypeStruct((b, FOX_HEADS, s), F32),
                   jax.ShapeDtypeStruct((b, s, HEAD_DIM), F32)],
        compiler_params=pltpu.CompilerParams(
            dimension_semantics=("arbitrary",),
            vmem_limit_bytes=VMEM_LIMIT),
        name="fox_gate",
    )(h_bf3, wf, bf_row)


def _causal_tile():
    key = np.arange(TQ)[:, None]
    qry = np.arange(TQ)[None, :]
    return np.where(key <= qry, 0.0, NEG).astype(np.float32)


ONES_ROWS = 16


def _softmax_pv_t(parts, vt, row_shift=None):
    m = parts[0].max(axis=0, keepdims=True)
    for p in parts[1:]:
        m = jnp.maximum(m, p.max(axis=0, keepdims=True))
    if row_shift is None:
        shift = -m
    else:
        m_full = m + row_shift
        shift = row_shift - m_full
    pb = [jnp.exp2(p + shift).astype(BF16) for p in parts]
    pb = jnp.concatenate(pb, axis=0) if len(pb) > 1 else pb[0]
    acc = jnp.dot(vt, pb, preferred_element_type=F32)
    return acc[0:HEAD_DIM] / acc[HEAD_DIM:HEAD_DIM + 1]


def _value_rows(v):
    vt = v.astype(F32).T.astype(BF16)
    return jnp.concatenate([vt, jnp.ones((ONES_ROWS, v.shape[0]), BF16)], axis=0)


def _fox_kernel(mask_ref, crow_ref, ccol_ref, q_ref, k_ref, v_ref, z_ref, o_ref, *, seq):
    head_lane = lax.broadcasted_iota(jnp.int32, (seq, HEAD_DIM), 1) == pl.program_id(1)
    c_key = jnp.sum(jnp.where(head_lane, ccol_ref[0], 0.0), axis=1, keepdims=True)
    c_key = jnp.broadcast_to(c_key, (seq, HEAD_DIM))
    c_key = jnp.concatenate([c_key, c_key], axis=1)
    vt = _value_rows(v_ref[0])

    def logits(qi):
        r0 = qi * TQ
        q = q_ref[0, r0:r0 + TQ, :]
        parts = []
        if qi >= 1:
            parts.append(_dot_nt(k_ref[0, 0:r0, :], q) - c_key[0:r0])
        parts.append(_dot_nt(k_ref[0, r0:r0 + TQ, :], q) - c_key[r0:r0 + TQ] + mask_ref[...])
        return parts

    nq = seq // TQ
    nxt = logits(nq - 1)
    for qi in reversed(range(nq)):
        r0 = qi * TQ
        parts = nxt
        if qi > 0:
            nxt = logits(qi - 1)
        o = _softmax_pv_t(parts, vt[:, 0:r0 + TQ], row_shift=crow_ref[0, 0, :, r0:r0 + TQ]).T
        z = z_ref[0, r0:r0 + TQ, :].astype(F32)
        o_ref[0, r0:r0 + TQ, :] = (o * _silu(z)).astype(BF16)


def _fox_attention(proj, cum_row, cum_col):
    b, s, _ = proj.shape
    crow = cum_row.reshape(b, FOX_HEADS, 1, s)
    mask = jnp.asarray(_causal_tile())
    return pl.pallas_call(
        functools.partial(_fox_kernel, seq=s),
        grid=(b, FOX_HEADS),
        in_specs=[
            pl.BlockSpec((TQ, TQ), lambda i, h: (0, 0)),
            pl.BlockSpec((1, 1, 1, s), lambda i, h: (i, h, 0, 0)),
            pl.BlockSpec((1, s, HEAD_DIM), lambda i, h: (i, 0, 0)),
        ] + _head_specs(s, FOX_HEADS),
        out_specs=pl.BlockSpec((1, s, HEAD_DIM), lambda i, h: (i, 0, h)),
        out_shape=jax.ShapeDtypeStruct((b, s, D_INNER), BF16),
        compiler_params=pltpu.CompilerParams(
            dimension_semantics=("arbitrary", "arbitrary"),
            vmem_limit_bytes=VMEM_LIMIT),
        name="fox_attention",
    )(mask, crow, cum_col, proj, proj, proj, proj)


def _col_scale():
    cs = np.ones((1, 4 * D_INNER), np.float32)
    cs[0, :D_INNER] = QK_SCALE * LOG2E
    return jnp.asarray(cs)


def kernel(x, rel_bias, w_in_0, lam_q1_0, lam_k1_0, lam_q2_0, lam_k2_0, subln_g_0, w_out_0, ln_g_0, ln_b_0, w_in_1, w_out_1, ln_g_1, ln_b_1, w_in_2, b_f_2, w_out_2, ln_g_2, ln_b_2, w_in_3, lam_q1_3, lam_k1_3, lam_q2_3, lam_k2_3, subln_g_3, w_out_3, ln_g_3, ln_b_3):
    b, s, d = x.shape
    m = b * s
    e = D_INNER
    col_scale = _col_scale()
    bias = _bias_tiles(rel_bias)

    h = x.reshape(m, d)
    h_bf = h
    layers = [
        (w_in_0, w_out_0, ln_g_0, ln_b_0),
        (w_in_1, w_out_1, ln_g_1, ln_b_1),
        (w_in_2, w_out_2, ln_g_2, ln_b_2),
        (w_in_3, w_out_3, ln_g_3, ln_b_3),
    ]
    diff_extra = {
        0: (lam_q1_0, lam_k1_0, lam_q2_0, lam_k2_0, subln_g_0),
        3: (lam_q1_3, lam_k1_3, lam_q2_3, lam_k2_3, subln_g_3),
    }
    for i, (w_in, w_out, g, beta) in enumerate(layers):
        kind = i % 3
        if kind == 2:
            w_t = w_in.T
            proj, w_out_bf = _in_proj(h_bf, w_t, col_scale, 4 * e, w_out, w_is_nk=True)
        else:
            proj, w_out_bf = _in_proj(h_bf, w_in, col_scale, 4 * e, w_out)
        proj = proj.reshape(b, s, 4 * e)
        if kind == 0:
            lq1, lk1, lq2, lk2, sg = diff_extra[i]
            lam_init = 0.8 - 0.6 * math.exp(-0.3 * i)
            a = _diff_attention(proj, jnp.stack([lq1, lk1, lq2, lk2]), sg, bias, lam_init)
        elif kind == 1:
            a = _moba_attention(proj, bias)
        else:
            wf = jnp.pad(w_t[4 * e:].astype(BF16), ((0, HEAD_DIM - FOX_HEADS), (0, 0)))
            bf_row = jnp.pad(b_f_2, (0, HEAD_DIM - FOX_HEADS)).reshape(1, HEAD_DIM)
            cum_row, cum_col = _fox_gate(h_bf.reshape(b, s, d), wf, bf_row)
            a = _fox_attention(proj, cum_row, cum_col)
        h, h_bf = _out_proj_ln(a.reshape(m, e), w_out_bf, h, g, beta)
    return h.reshape(b, s, d)
```

```python
import functools
import math

import jax
import jax.numpy as jnp
import numpy as np
from jax import lax
from jax.experimental import pallas as pl
from jax.experimental.pallas import tpu as pltpu

D_MODEL = 2048
D_INNER = 2048
DEPTH = 4
DIFF_HEADS = 8
MOBA_HEADS = 16
MOBA_BLOCK = 256
MOBA_TOPK = 3
FOX_HEADS = 16
HEAD_DIM = 128
REL_BUCKETS = 32
REL_MAX_DIST = 128
REL_HEADS = 16
LN_EPS = 1e-5
RMS_EPS = 1e-5
NEG = -1e30
DN_ALPHA = (2.0 * DEPTH) ** 0.25
LOG2E = math.log2(math.e)
QK_SCALE = HEAD_DIM ** -0.5

TQ = 256
VMEM_LIMIT = 60 * 1024 * 1024

F32 = jnp.float32
BF16 = jnp.bfloat16


def _dot_nt(a, b):
    return lax.dot_general(a, b, (((1,), (1,)), ((), ())), preferred_element_type=F32)


def _silu(z):
    return z / (1.0 + jnp.exp(-z))


def _inproj_kernel(x_ref, w_ref, cs_ref, wo_ref, o_ref, wo_bf_ref, wbf_ref, *, w_is_nk):
    @pl.when(pl.program_id(1) == 0)
    def _():
        wbf_ref[...] = w_ref[...].astype(BF16)

    x = x_ref[...].astype(BF16)
    if w_is_nk:
        acc = _dot_nt(x, wbf_ref[...])
    else:
        acc = jnp.dot(x, wbf_ref[...], preferred_element_type=F32)
    o_ref[...] = (acc * cs_ref[...]).astype(o_ref.dtype)
    wo_bf_ref[...] = wo_ref[...].astype(BF16)


def _in_proj(x, w, col_scale, n_out, w_out, w_is_nk=False):
    m, k = x.shape
    tm, tn = (2048, 1024) if x.dtype == BF16 else (512, 2048)
    n_tiles, m_tiles = n_out // tn, m // tm
    ko, no = w_out.shape
    slab = ko // (n_tiles * m_tiles)
    if w_is_nk:
        w_block, w_spec = (tn, k), pl.BlockSpec((tn, k), lambda n, i: (n, 0))
    else:
        w_block, w_spec = (k, tn), pl.BlockSpec((k, tn), lambda n, i: (0, n))
    return pl.pallas_call(
        functools.partial(_inproj_kernel, w_is_nk=w_is_nk),
        grid=(n_tiles, m_tiles),
        in_specs=[
            pl.BlockSpec((tm, k), lambda n, i: (i, 0)),
            w_spec,
            pl.BlockSpec((1, tn), lambda n, i: (0, n)),
            pl.BlockSpec((slab, no), lambda n, i: (n * m_tiles + i, 0)),
        ],
        out_specs=[pl.BlockSpec((tm, tn), lambda n, i: (i, n)),
                   pl.BlockSpec((slab, no), lambda n, i: (n * m_tiles + i, 0))],
        out_shape=[jax.ShapeDtypeStruct((m, n_out), BF16),
                   jax.ShapeDtypeStruct((ko, no), BF16)],
        scratch_shapes=[pltpu.VMEM(w_block, BF16)],
        compiler_params=pltpu.CompilerParams(
            dimension_semantics=("arbitrary", "arbitrary"),
            vmem_limit_bytes=VMEM_LIMIT),
        name="in_proj",
    )(x, w, col_scale, w_out)


LN_CHUNKS = 4


def _data_zero(x):
    rows, n = x.shape
    x = jnp.max(x.reshape(rows // 8, 8, n), axis=0)
    bits = pltpu.bitcast(_fold_lanes([x], jnp.maximum), jnp.uint32)
    zero = lax.shift_right_logical(lax.shift_right_logical(bits, jnp.uint32(16)), jnp.uint32(16))
    return pltpu.bitcast(zero, F32)[0:1, 0:1]


def _outproj_ln_kernel(a_ref, w_ref, h_ref, g_ref, b_ref, o_ref, obf_ref, y0_ref, y1_ref, *, nblk):
    i = pl.program_id(0)
    ys = (y0_ref, y1_ref)
    tm, n = o_ref.shape
    rows, cols = tm // LN_CHUNKS, n // LN_CHUNKS

    def layer_norm(y_ref, c):
        rs = slice(c * rows, (c + 1) * rows)
        r = DN_ALPHA * h_ref[rs, :] + y_ref[rs, :]
        mu = jnp.mean(r, axis=1, keepdims=True)
        d = r - mu
        var = jnp.mean(d * d, axis=1, keepdims=True)
        out = d * lax.rsqrt(var + LN_EPS) * g_ref[...] + b_ref[...]
        o_ref[rs, :] = out
        obf_ref[rs, :] = out.astype(BF16)
        return out

    def matmul(y_ref, c, anchor=None):
        cs = slice(c * cols, (c + 1) * cols)
        if anchor is None:
            y = jnp.dot(a_ref[...], w_ref[:, cs], preferred_element_type=F32)
        else:
            head = a_ref[:, 0:HEAD_DIM] + anchor.astype(BF16)
            y = (jnp.dot(head, w_ref[0:HEAD_DIM, cs], preferred_element_type=F32)
                 + jnp.dot(a_ref[:, HEAD_DIM:], w_ref[HEAD_DIM:, cs], preferred_element_type=F32))
        y_ref[:, cs] = y

    @pl.when(i == 0)
    def _():
        for c in range(LN_CHUNKS):
            matmul(ys[0], c)

    for parity in range(2):
        @pl.when(jnp.logical_and(jnp.logical_and(i > 0, i < nblk), i % 2 == parity))
        def _():
            zero = None
            for c in range(LN_CHUNKS):
                matmul(ys[parity], c, anchor=zero)
                zero = _data_zero(layer_norm(ys[1 - parity], c))

    @pl.when(i == nblk)
    def _():
        for c in range(LN_CHUNKS):
            layer_norm(ys[(nblk - 1) % 2], c)


def _out_proj_ln(a_bf, w_bf, h, g, b, tm=512):
    m, k = a_bf.shape
    n = w_bf.shape[1]
    nblk = m // tm
    cur = lambda i: (jnp.minimum(i, nblk - 1), 0)
    prev = lambda i: (jnp.maximum(i - 1, 0), 0)
    return pl.pallas_call(
        functools.partial(_outproj_ln_kernel, nblk=nblk),
        grid=(nblk + 1,),
        in_specs=[
            pl.BlockSpec((tm, k), cur),
            pl.BlockSpec((k, n), lambda i: (0, 0)),
            pl.BlockSpec((tm, n), prev),
            pl.BlockSpec((1, n), lambda i: (0, 0)),
            pl.BlockSpec((1, n), lambda i: (0, 0)),
        ],
        out_specs=[
            pl.BlockSpec((tm, n), prev),
            pl.BlockSpec((tm, n), prev),
        ],
        out_shape=[jax.ShapeDtypeStruct((m, n), F32),
                   jax.ShapeDtypeStruct((m, n), BF16)],
        scratch_shapes=[pltpu.VMEM((tm, n), F32), pltpu.VMEM((tm, n), F32)],
        compiler_params=pltpu.CompilerParams(
            dimension_semantics=("arbitrary",),
            vmem_limit_bytes=VMEM_LIMIT),
        name="out_proj_ln",
    )(a_bf, w_bf, h, g.reshape(1, n), b.reshape(1, n))


BIAS_SPAN = 4 * TQ


def _bucket_profile():
    n = np.maximum(2 * TQ - np.arange(BIAS_SPAN), 0)
    max_exact = REL_BUCKETS // 2
    nf = np.maximum(n, 1).astype(np.float32)
    large = max_exact + (np.log(nf / max_exact) / math.log(REL_MAX_DIST / max_exact)
                         * (REL_BUCKETS - max_exact)).astype(np.int32)
    large = np.minimum(large, REL_BUCKETS - 1)
    return np.where(n < max_exact, n, large).astype(np.int32).reshape(1, BIAS_SPAN)


def _bias_kernel(tab_ref, bkt_ref, o_ref):
    c = pl.program_id(0)
    far = tab_ref[c, REL_BUCKETS - 1]
    bk = bkt_ref[...]
    acc = jnp.zeros((1, BIAS_SPAN), F32)
    for j in range(REL_BUCKETS):
        acc = jnp.where(bk == j, tab_ref[c, j], acc)
    lane = lax.broadcasted_iota(jnp.int32, (1, BIAS_SPAN), 1)
    prof = jnp.where(lane > 2 * TQ, NEG, (acc - far) * LOG2E)
    rolled = pltpu.roll(jnp.broadcast_to(prof, (TQ, BIAS_SPAN)), 0, 1, stride=1, stride_axis=0)
    o_ref[0, 0] = rolled[:, 2 * TQ:3 * TQ]
    o_ref[0, 1] = rolled[:, TQ:2 * TQ]


def _bias_tiles(rel_bias):
    tab = rel_bias.T
    bkt = jnp.asarray(_bucket_profile())
    return pl.pallas_call(
        _bias_kernel,
        grid=(REL_HEADS,),
        in_specs=[
            pl.BlockSpec(memory_space=pltpu.SMEM),
            pl.BlockSpec((1, BIAS_SPAN), lambda c: (0, 0)),
        ],
        out_specs=pl.BlockSpec((1, 2, TQ, TQ), lambda c: (c, 0, 0, 0)),
        out_shape=jax.ShapeDtypeStruct((REL_HEADS, 2, TQ, TQ), F32),
        name="rel_bias_tiles",
    )(tab, bkt)


def _fold_lanes(parts, op):
    acc = None
    for p in parts:
        for t in range(p.shape[1] // HEAD_DIM):
            blk = p[:, t * HEAD_DIM:(t + 1) * HEAD_DIM]
            acc = blk if acc is None else op(acc, blk)
    return acc


def _softmax_pv(parts, v, row_shift=None):
    m = _fold_lanes(parts, jnp.maximum).max(axis=1, keepdims=True)
    if row_shift is None:
        shift = -m
    else:
        m_full = m + row_shift
        shift = row_shift - m_full
    ps = [jnp.exp2(p + shift) for p in parts]
    pb = jnp.concatenate([p.astype(BF16) for p in ps], axis=1) if len(ps) > 1 else ps[0].astype(BF16)
    if v.shape[1] == HEAD_DIM:
        v1 = jnp.concatenate([v, jnp.ones(v.shape, v.dtype)], axis=1)
        acc = jnp.dot(pb, v1, preferred_element_type=F32)
        return acc[:, 0:HEAD_DIM] / acc[:, HEAD_DIM:HEAD_DIM + 1]
    l = _fold_lanes(ps, jnp.add).sum(axis=1, keepdims=True)
    acc = jnp.dot(pb, v, preferred_element_type=F32)
    return acc / l


def _diff_kernel(lamv_ref, g_ref, q_ref, k_ref, v_ref, z_ref, bias_ref, o_ref, *, lam_init, seq):
    lv = lamv_ref[...]
    s1 = jnp.sum(lv[0:1] * lv[1:2], axis=1, keepdims=True)
    s2 = jnp.sum(lv[2:3] * lv[3:4], axis=1, keepdims=True)
    lam = jnp.exp(s1) - jnp.exp(s2) + lam_init

    def logits(qi, j):
        r0 = qi * TQ
        c0 = j * HEAD_DIM
        q = q_ref[0, r0:r0 + TQ, c0:c0 + HEAD_DIM]
        parts = []
        if qi >= 2:
            parts.append(_dot_nt(q, k_ref[0, 0:r0 - TQ, c0:c0 + HEAD_DIM]))
        if qi >= 1:
            parts.append(_dot_nt(q, k_ref[0, r0 - TQ:r0, c0:c0 + HEAD_DIM]) + bias_ref[j, 1])
        parts.append(_dot_nt(q, k_ref[0, r0:r0 + TQ, c0:c0 + HEAD_DIM]) + bias_ref[j, 0])
        return parts

    tasks = [(qi, j) for qi in reversed(range(seq // TQ)) for j in range(2)]
    nxt = logits(*tasks[0])
    outs = []
    for t, (qi, j) in enumerate(tasks):
        r0 = qi * TQ
        parts = nxt
        if t + 1 < len(tasks):
            nxt = logits(*tasks[t + 1])
        outs.append(_softmax_pv(parts, v_ref[0, 0:r0 + TQ, :]))
        if j == 0:
            continue
        o = outs[-2] - lam * outs[-1]
        ms = jnp.mean(o * o, axis=1, keepdims=True)
        o = o * lax.rsqrt(ms + RMS_EPS) * g_ref[...] * (1.0 - lam_init)
        z = z_ref[0, r0:r0 + TQ, :].astype(F32)
        o_ref[0, r0:r0 + TQ, :] = (o * _silu(z)).astype(BF16)


def _diff_attention(proj, lamv, subln_g, bias, lam_init):
    b, s, _ = proj.shape
    w = 2 * HEAD_DIM
    nh = D_INNER // w
    return pl.pallas_call(
        functools.partial(_diff_kernel, lam_init=lam_init, seq=s),
        grid=(b, DIFF_HEADS),
        in_specs=[
            pl.BlockSpec((4, HEAD_DIM), lambda i, h: (0, 0)),
            pl.BlockSpec((1, w), lambda i, h: (0, 0)),
            pl.BlockSpec((1, s, w), lambda i, h: (i, 0, h)),
            pl.BlockSpec((1, s, w), lambda i, h: (i, 0, nh + h)),
            pl.BlockSpec((1, s, w), lambda i, h: (i, 0, 2 * nh + h)),
            pl.BlockSpec((1, s, w), lambda i, h: (i, 0, 3 * nh + h)),
            pl.BlockSpec((2, 2, TQ, TQ), lambda i, h: (h, 0, 0, 0)),
        ],
        out_specs=pl.BlockSpec((1, s, w), lambda i, h: (i, 0, h)),
        out_shape=jax.ShapeDtypeStruct((b, s, D_INNER), BF16),
        compiler_params=pltpu.CompilerParams(
            dimension_semantics=("arbitrary", "arbitrary"),
            vmem_limit_bytes=VMEM_LIMIT),
        name="diff_attention",
    )(lamv, subln_g.reshape(1, w), proj, proj, proj, proj, bias)


def _block_indicator(seq):
    n_kb = seq // MOBA_BLOCK
    ind = np.zeros((HEAD_DIM, seq), np.float32)
    for j in range(n_kb):
        ind[j, j * MOBA_BLOCK:(j + 1) * MOBA_BLOCK] = 1.0
    return ind


def _moba_kernel(ind_ref, kind_ref, q_ref, k_ref, v_ref, z_ref, bias_ref, o_ref, *, seq):
    n_kb = seq // MOBA_BLOCK
    ksum = jnp.dot(ind_ref[...], k_ref[0], preferred_element_type=F32)
    kmean = ksum * (1.0 / MOBA_BLOCK)
    hi = kmean.astype(BF16)
    lo = (kmean - hi.astype(F32)).astype(BF16)
    km2 = jnp.concatenate([hi, lo], axis=1)
    sub = lax.broadcasted_iota(jnp.int32, (n_kb, TQ), 0)

    def block_mask(qi, q):
        gate = _dot_nt(km2, jnp.concatenate([q, q], axis=1))[0:n_kb]
        rank = jnp.zeros((n_kb, TQ), F32)
        for bp in range(qi):
            gb = gate[bp:bp + 1, :]
            rank = rank + jnp.where(sub > bp, jnp.where(gb >= gate, 1.0, 0.0), jnp.where(gb > gate, 1.0, 0.0))
        mt = jnp.where(sub >= qi, 0.0, jnp.where(rank < float(MOBA_TOPK), 0.0, NEG))
        mt = jnp.concatenate([mt, jnp.zeros((HEAD_DIM - n_kb, TQ), F32)], axis=0)
        return mt.T.astype(BF16)

    def logits(qi):
        r0 = qi * TQ
        q = q_ref[0, r0:r0 + TQ, :]
        masked = qi > MOBA_TOPK
        if masked:
            q = jnp.concatenate([q, block_mask(qi, q)], axis=1)
        parts = []
        for j in range(qi + 1):
            kj = k_ref[0, j * TQ:(j + 1) * TQ, :]
            if masked:
                kj = jnp.concatenate([kj, kind_ref[j * TQ:(j + 1) * TQ, :]], axis=1)
            sj = _dot_nt(q, kj)
            if j >= qi - 1:
                sj = sj + bias_ref[0, qi - j]
            parts.append(sj)
        return parts

    order = [MOBA_TOPK] + [qi for qi in reversed(range(n_kb)) if qi != MOBA_TOPK]
    nxt = logits(order[0])
    for t, qi in enumerate(order):
        r0 = qi * TQ
        parts = nxt
        if t + 1 < n_kb:
            nxt = logits(order[t + 1])
        o = _softmax_pv(parts, v_ref[0, 0:r0 + TQ, :])
        z = z_ref[0, r0:r0 + TQ, :].astype(F32)
        o_ref[0, r0:r0 + TQ, :] = (o * _silu(z)).astype(BF16)


def _head_specs(s, nh):
    w = HEAD_DIM
    return [
        pl.BlockSpec((1, s, w), lambda i, h: (i, 0, h)),
        pl.BlockSpec((1, s, w), lambda i, h: (i, 0, nh + h)),
        pl.BlockSpec((1, s, w), lambda i, h: (i, 0, 2 * nh + h)),
        pl.BlockSpec((1, s, w), lambda i, h: (i, 0, 3 * nh + h)),
    ]


def _moba_attention(proj, bias):
    b, s, _ = proj.shape
    assert s // MOBA_BLOCK == 8 and MOBA_BLOCK == TQ
    ind = _block_indicator(s)
    return pl.pallas_call(
        functools.partial(_moba_kernel, seq=s),
        grid=(b, MOBA_HEADS),
        in_specs=[pl.BlockSpec((HEAD_DIM, s), lambda i, h: (0, 0)),
                  pl.BlockSpec((s, HEAD_DIM), lambda i, h: (0, 0))]
        + _head_specs(s, MOBA_HEADS)
        + [pl.BlockSpec((1, 2, TQ, TQ), lambda i, h: (h, 0, 0, 0))],
        out_specs=pl.BlockSpec((1, s, HEAD_DIM), lambda i, h: (i, 0, h)),
        out_shape=jax.ShapeDtypeStruct((b, s, D_INNER), BF16),
        compiler_params=pltpu.CompilerParams(
            dimension_semantics=("arbitrary", "arbitrary"),
            vmem_limit_bytes=VMEM_LIMIT),
        name="moba_attention",
    )(jnp.asarray(ind, BF16), jnp.asarray(ind.T, BF16), proj, proj, proj, proj, bias)


def _fox_gate_kernel(x_ref, wf_ref, bf_ref, row_ref, col_ref, *, seq):
    f = _dot_nt(x_ref[0], wf_ref[...]) + bf_ref[...]
    c = jnp.minimum(f, 0.0) - jnp.log1p(jnp.exp(-jnp.abs(f)))
    c = c.T[0:FOX_HEADS]
    lane = lax.broadcasted_iota(jnp.int32, c.shape, 1)
    d = 1
    while d < seq:
        c = c + jnp.where(lane >= d, pltpu.roll(c, d, 1), 0.0)
        d *= 2
    c = c * LOG2E
    row_ref[0] = c
    col_ref[0] = jnp.concatenate([c, jnp.zeros((HEAD_DIM - FOX_HEADS, seq), F32)], axis=0).T


def _fox_gate(h_bf3, wf, bf_row):
    b, s, d = h_bf3.shape
    return pl.pallas_call(
        functools.partial(_fox_gate_kernel, seq=s),
        grid=(b,),
        in_specs=[
            pl.BlockSpec((1, s, d), lambda i: (i, 0, 0)),
            pl.BlockSpec((HEAD_DIM, d), lambda i: (0, 0)),
            pl.BlockSpec((1, HEAD_DIM), lambda i: (0, 0)),
        ],
        out_specs=[pl.BlockSpec((1, FOX_HEADS, s), lambda i: (i, 0, 0)),
                   pl.BlockSpec((1, s, HEAD_DIM), lambda i: (i, 0, 0))],
        out_shape=[jax.ShapeDtypeStruct((b, FOX_HEADS, s), F32),
                   jax.ShapeDtypeStruct((b, s, HEAD_DIM), F32)],
        compiler_params=pltpu.CompilerParams(
            dimension_semantics=("arbitrary",),
            vmem_limit_bytes=VMEM_LIMIT),
        name="fox_gate",
    )(h_bf3, wf, bf_row)


def _causal_tile():
    r = np.arange(TQ)[:, None]
    c = np.arange(TQ)[None, :]
    return np.where(c <= r, 0.0, NEG).astype(np.float32)


def _fox_kernel(mask_ref, crow_ref, ccol_ref, q_ref, k_ref, v_ref, z_ref, o_ref, *, seq):
    def logits(qi):
        r0 = qi * TQ
        q = q_ref[0, r0:r0 + TQ, :]
        parts = []
        if qi >= 1:
            parts.append(_dot_nt(q, k_ref[0, 0:r0, :]) - crow_ref[0, 0, :, 0:r0])
        parts.append(_dot_nt(q, k_ref[0, r0:r0 + TQ, :]) - crow_ref[0, 0, :, r0:r0 + TQ] + mask_ref[...])
        return parts

    nq = seq // TQ
    head_lane = lax.broadcasted_iota(jnp.int32, (TQ, HEAD_DIM), 1) == pl.program_id(1)
    nxt = logits(nq - 1)
    for qi in reversed(range(nq)):
        r0 = qi * TQ
        parts = nxt
        if qi > 0:
            nxt = logits(qi - 1)
        c_t = jnp.sum(jnp.where(head_lane, ccol_ref[0, r0:r0 + TQ, :], 0.0), axis=1, keepdims=True)
        o = _softmax_pv(parts, v_ref[0, 0:r0 + TQ, :], row_shift=c_t)
        z = z_ref[0, r0:r0 + TQ, :].astype(F32)
        o_ref[0, r0:r0 + TQ, :] = (o * _silu(z)).astype(BF16)


def _fox_attention(proj, cum_row, cum_col):
    b, s, _ = proj.shape
    crow = cum_row.reshape(b, FOX_HEADS, 1, s)
    mask = jnp.asarray(_causal_tile())
    return pl.pallas_call(
        functools.partial(_fox_kernel, seq=s),
        grid=(b, FOX_HEADS),
        in_specs=[
            pl.BlockSpec((TQ, TQ), lambda i, h: (0, 0)),
            pl.BlockSpec((1, 1, 1, s), lambda i, h: (i, h, 0, 0)),
            pl.BlockSpec((1, s, HEAD_DIM), lambda i, h: (i, 0, 0)),
        ] + _head_specs(s, FOX_HEADS),
        out_specs=pl.BlockSpec((1, s, HEAD_DIM), lambda i, h: (i, 0, h)),
        out_shape=jax.ShapeDtypeStruct((b, s, D_INNER), BF16),
        compiler_params=pltpu.CompilerParams(
            dimension_semantics=("arbitrary", "arbitrary"),
            vmem_limit_bytes=VMEM_LIMIT),
        name="fox_attention",
    )(mask, crow, cum_col, proj, proj, proj, proj)


def _col_scale():
    cs = np.ones((1, 4 * D_INNER), np.float32)
    cs[0, :D_INNER] = QK_SCALE * LOG2E
    return jnp.asarray(cs)


def kernel(x, rel_bias, w_in_0, lam_q1_0, lam_k1_0, lam_q2_0, lam_k2_0, subln_g_0, w_out_0, ln_g_0, ln_b_0, w_in_1, w_out_1, ln_g_1, ln_b_1, w_in_2, b_f_2, w_out_2, ln_g_2, ln_b_2, w_in_3, lam_q1_3, lam_k1_3, lam_q2_3, lam_k2_3, subln_g_3, w_out_3, ln_g_3, ln_b_3):
    b, s, d = x.shape
    m = b * s
    e = D_INNER
    col_scale = _col_scale()
    bias = _bias_tiles(rel_bias)

    h = x.reshape(m, d)
    h_bf = h
    layers = [
        (w_in_0, w_out_0, ln_g_0, ln_b_0),
        (w_in_1, w_out_1, ln_g_1, ln_b_1),
        (w_in_2, w_out_2, ln_g_2, ln_b_2),
        (w_in_3, w_out_3, ln_g_3, ln_b_3),
    ]
    diff_extra = {
        0: (lam_q1_0, lam_k1_0, lam_q2_0, lam_k2_0, subln_g_0),
        3: (lam_q1_3, lam_k1_3, lam_q2_3, lam_k2_3, subln_g_3),
    }
    for i, (w_in, w_out, g, beta) in enumerate(layers):
        kind = i % 3
        if kind == 2:
            w_t = w_in.T
            proj, w_out_bf = _in_proj(h_bf, w_t, col_scale, 4 * e, w_out, w_is_nk=True)
        else:
            proj, w_out_bf = _in_proj(h_bf, w_in, col_scale, 4 * e, w_out)
        proj = proj.reshape(b, s, 4 * e)
        if kind == 0:
            lq1, lk1, lq2, lk2, sg = diff_extra[i]
            lam_init = 0.8 - 0.6 * math.exp(-0.3 * i)
            a = _diff_attention(proj, jnp.stack([lq1, lk1, lq2, lk2]), sg, bias, lam_init)
        elif kind == 1:
            a = _moba_attention(proj, bias)
        else:
            wf = jnp.pad(w_t[4 * e:].astype(BF16), ((0, HEAD_DIM - FOX_HEADS), (0, 0)))
            bf_row = jnp.pad(b_f_2, (0, HEAD_DIM - FOX_HEADS)).reshape(1, HEAD_DIM)
            cum_row, cum_col = _fox_gate(h_bf.reshape(b, s, d), wf, bf_row)
            a = _fox_attention(proj, cum_row, cum_col)
        h, h_bf = _out_proj_ln(a.reshape(m, e), w_out_bf, h, g, beta)
    return h.reshape(b, s, d)
```

```python
import functools
import math

import jax
import jax.numpy as jnp
import numpy as np
from jax import lax
from jax.experimental import pallas as pl
from jax.experimental.pallas import tpu as pltpu

D_MODEL = 2048
D_INNER = 2048
DEPTH = 4
DIFF_HEADS = 8
MOBA_HEADS = 16
MOBA_BLOCK = 256
MOBA_TOPK = 3
FOX_HEADS = 16
HEAD_DIM = 128
REL_BUCKETS = 32
REL_MAX_DIST = 128
REL_HEADS = 16
LN_EPS = 1e-5
RMS_EPS = 1e-5
NEG = -1e30
DN_ALPHA = (2.0 * DEPTH) ** 0.25
LOG2E = math.log2(math.e)
QK_SCALE = HEAD_DIM ** -0.5

TQ = 256
VMEM_LIMIT = 60 * 1024 * 1024

F32 = jnp.float32
BF16 = jnp.bfloat16


def _dot_nt(a, b):
    return lax.dot_general(a, b, (((1,), (1,)), ((), ())), preferred_element_type=F32)


def _silu(z):
    return z / (1.0 + jnp.exp(-z))


def _inproj_kernel(x_ref, w_ref, cs_ref, wo_ref, o_ref, wo_bf_ref, wbf_ref, *, w_is_nk):
    @pl.when(pl.program_id(1) == 0)
    def _():
        wbf_ref[...] = w_ref[...].astype(BF16)

    x = x_ref[...].astype(BF16)
    if w_is_nk:
        acc = _dot_nt(x, wbf_ref[...])
    else:
        acc = jnp.dot(x, wbf_ref[...], preferred_element_type=F32)
    o_ref[...] = (acc * cs_ref[...]).astype(o_ref.dtype)
    wo_bf_ref[...] = wo_ref[...].astype(BF16)


def _in_proj(x, w, col_scale, n_out, w_out, w_is_nk=False):
    m, k = x.shape
    tm, tn = (2048, 1024) if x.dtype == BF16 else (512, 2048)
    n_tiles, m_tiles = n_out // tn, m // tm
    ko, no = w_out.shape
    slab = ko // (n_tiles * m_tiles)
    if w_is_nk:
        w_block, w_spec = (tn, k), pl.BlockSpec((tn, k), lambda n, i: (n, 0))
    else:
        w_block, w_spec = (k, tn), pl.BlockSpec((k, tn), lambda n, i: (0, n))
    return pl.pallas_call(
        functools.partial(_inproj_kernel, w_is_nk=w_is_nk),
        grid=(n_tiles, m_tiles),
        in_specs=[
            pl.BlockSpec((tm, k), lambda n, i: (i, 0)),
            w_spec,
            pl.BlockSpec((1, tn), lambda n, i: (0, n)),
            pl.BlockSpec((slab, no), lambda n, i: (n * m_tiles + i, 0)),
        ],
        out_specs=[pl.BlockSpec((tm, tn), lambda n, i: (i, n)),
                   pl.BlockSpec((slab, no), lambda n, i: (n * m_tiles + i, 0))],
        out_shape=[jax.ShapeDtypeStruct((m, n_out), BF16),
                   jax.ShapeDtypeStruct((ko, no), BF16)],
        scratch_shapes=[pltpu.VMEM(w_block, BF16)],
        compiler_params=pltpu.CompilerParams(
            dimension_semantics=("arbitrary", "arbitrary"),
            vmem_limit_bytes=VMEM_LIMIT),
        name="in_proj",
    )(x, w, col_scale, w_out)


LN_CHUNKS = 4


def _data_zero(x):
    rows, n = x.shape
    x = jnp.max(x.reshape(rows // 8, 8, n), axis=0)
    bits = pltpu.bitcast(_fold_lanes([x], jnp.maximum), jnp.uint32)
    zero = lax.shift_right_logical(lax.shift_right_logical(bits, jnp.uint32(16)), jnp.uint32(16))
    return pltpu.bitcast(zero, F32)[0:1, 0:1]


def _outproj_ln_kernel(a_ref, w_ref, h_ref, g_ref, b_ref, o_ref, *rest, nblk):
    obf_ref = rest[0] if len(rest) == 3 else None
    y0_ref, y1_ref = rest[-2:]
    i = pl.program_id(0)
    ys = (y0_ref, y1_ref)
    tm, n = o_ref.shape
    rows, cols = tm // LN_CHUNKS, n // LN_CHUNKS

    def layer_norm(y_ref, c):
        rs = slice(c * rows, (c + 1) * rows)
        r = DN_ALPHA * h_ref[rs, :] + y_ref[rs, :]
        mu = jnp.mean(r, axis=1, keepdims=True)
        d = r - mu
        var = jnp.mean(d * d, axis=1, keepdims=True)
        out = d * lax.rsqrt(var + LN_EPS) * g_ref[...] + b_ref[...]
        o_ref[rs, :] = out
        if obf_ref is not None:
            obf_ref[rs, :] = out.astype(BF16)
        return out

    def matmul(y_ref, c, anchor=None):
        cs = slice(c * cols, (c + 1) * cols)
        if anchor is None:
            y = jnp.dot(a_ref[...], w_ref[:, cs], preferred_element_type=F32)
        else:
            head = a_ref[:, 0:HEAD_DIM] + anchor.astype(BF16)
            y = (jnp.dot(head, w_ref[0:HEAD_DIM, cs], preferred_element_type=F32)
                 + jnp.dot(a_ref[:, HEAD_DIM:], w_ref[HEAD_DIM:, cs], preferred_element_type=F32))
        y_ref[:, cs] = y

    @pl.when(i == 0)
    def _():
        for c in range(LN_CHUNKS):
            matmul(ys[0], c)

    for parity in range(2):
        @pl.when(jnp.logical_and(jnp.logical_and(i > 0, i < nblk), i % 2 == parity))
        def _():
            zero = None
            for c in range(LN_CHUNKS):
                matmul(ys[parity], c, anchor=zero)
                zero = _data_zero(layer_norm(ys[1 - parity], c))

    @pl.when(i == nblk)
    def _():
        for c in range(LN_CHUNKS):
            layer_norm(ys[(nblk - 1) % 2], c)


def _out_proj_ln(a_bf, w_bf, h, g, b, emit_bf16, tm=512):
    m, k = a_bf.shape
    n = w_bf.shape[1]
    nblk = m // tm
    cur = lambda i: (jnp.minimum(i, nblk - 1), 0)
    prev = lambda i: (jnp.maximum(i - 1, 0), 0)
    n_out = 2 if emit_bf16 else 1
    return pl.pallas_call(
        functools.partial(_outproj_ln_kernel, nblk=nblk),
        grid=(nblk + 1,),
        in_specs=[
            pl.BlockSpec((tm, k), cur),
            pl.BlockSpec((k, n), lambda i: (0, 0)),
            pl.BlockSpec((tm, n), prev),
            pl.BlockSpec((1, n), lambda i: (0, 0)),
            pl.BlockSpec((1, n), lambda i: (0, 0)),
        ],
        out_specs=[pl.BlockSpec((tm, n), prev)] * n_out,
        out_shape=[jax.ShapeDtypeStruct((m, n), F32), jax.ShapeDtypeStruct((m, n), BF16)][:n_out],
        scratch_shapes=[pltpu.VMEM((tm, n), F32), pltpu.VMEM((tm, n), F32)],
        compiler_params=pltpu.CompilerParams(
            dimension_semantics=("arbitrary",),
            vmem_limit_bytes=VMEM_LIMIT),
        name="out_proj_ln",
    )(a_bf, w_bf, h, g.reshape(1, n), b.reshape(1, n))


BIAS_SPAN = 4 * TQ


def _bucket_profile():
    n = np.maximum(2 * TQ - np.arange(BIAS_SPAN), 0)
    max_exact = REL_BUCKETS // 2
    nf = np.maximum(n, 1).astype(np.float32)
    large = max_exact + (np.log(nf / max_exact) / math.log(REL_MAX_DIST / max_exact)
                         * (REL_BUCKETS - max_exact)).astype(np.int32)
    large = np.minimum(large, REL_BUCKETS - 1)
    return np.where(n < max_exact, n, large).astype(np.int32).reshape(1, BIAS_SPAN)


def _bias_kernel(tab_ref, bkt_ref, o_ref):
    c = pl.program_id(0)
    far = tab_ref[c, REL_BUCKETS - 1]
    bk = bkt_ref[...]
    acc = jnp.zeros((1, BIAS_SPAN), F32)
    for j in range(REL_BUCKETS):
        acc = jnp.where(bk == j, tab_ref[c, j], acc)
    lane = lax.broadcasted_iota(jnp.int32, (1, BIAS_SPAN), 1)
    prof = jnp.where(lane > 2 * TQ, NEG, (acc - far) * LOG2E)
    rolled = pltpu.roll(jnp.broadcast_to(prof, (TQ, BIAS_SPAN)), 0, 1, stride=1, stride_axis=0)
    o_ref[0, 0] = rolled[:, 2 * TQ:3 * TQ]
    o_ref[0, 1] = rolled[:, TQ:2 * TQ]


def _bias_tiles(rel_bias):
    tab = rel_bias.T
    bkt = jnp.asarray(_bucket_profile())
    return pl.pallas_call(
        _bias_kernel,
        grid=(REL_HEADS,),
        in_specs=[
            pl.BlockSpec(memory_space=pltpu.SMEM),
            pl.BlockSpec((1, BIAS_SPAN), lambda c: (0, 0)),
        ],
        out_specs=pl.BlockSpec((1, 2, TQ, TQ), lambda c: (c, 0, 0, 0)),
        out_shape=jax.ShapeDtypeStruct((REL_HEADS, 2, TQ, TQ), F32),
        name="rel_bias_tiles",
    )(tab, bkt)


def _fold_lanes(parts, op):
    acc = None
    for p in parts:
        for t in range(p.shape[1] // HEAD_DIM):
            blk = p[:, t * HEAD_DIM:(t + 1) * HEAD_DIM]
            acc = blk if acc is None else op(acc, blk)
    return acc


def _softmax_pv(parts, v, row_shift=None):
    m = _fold_lanes(parts, jnp.maximum).max(axis=1, keepdims=True)
    if row_shift is None:
        shift = -m
    else:
        m_full = m + row_shift
        shift = row_shift - m_full
    ps = [jnp.exp2(p + shift) for p in parts]
    pb = jnp.concatenate([p.astype(BF16) for p in ps], axis=1) if len(ps) > 1 else ps[0].astype(BF16)
    if v.shape[1] == HEAD_DIM:
        v1 = jnp.concatenate([v, jnp.ones(v.shape, v.dtype)], axis=1)
        acc = jnp.dot(pb, v1, preferred_element_type=F32)
        return acc[:, 0:HEAD_DIM] / acc[:, HEAD_DIM:HEAD_DIM + 1]
    l = _fold_lanes(ps, jnp.add).sum(axis=1, keepdims=True)
    acc = jnp.dot(pb, v, preferred_element_type=F32)
    return acc / l


def _diff_kernel(lamv_ref, g_ref, q_ref, k_ref, v_ref, z_ref, bias_ref, o_ref, *, lam_init, seq):
    lv = lamv_ref[...]
    s1 = jnp.sum(lv[0:1] * lv[1:2], axis=1, keepdims=True)
    s2 = jnp.sum(lv[2:3] * lv[3:4], axis=1, keepdims=True)
    lam = jnp.exp(s1) - jnp.exp(s2) + lam_init

    def logits(qi, j):
        r0 = qi * TQ
        c0 = j * HEAD_DIM
        q = q_ref[0, r0:r0 + TQ, c0:c0 + HEAD_DIM]
        parts = []
        if qi >= 2:
            parts.append(_dot_nt(q, k_ref[0, 0:r0 - TQ, c0:c0 + HEAD_DIM]))
        if qi >= 1:
            parts.append(_dot_nt(q, k_ref[0, r0 - TQ:r0, c0:c0 + HEAD_DIM]) + bias_ref[j, 1])
        parts.append(_dot_nt(q, k_ref[0, r0:r0 + TQ, c0:c0 + HEAD_DIM]) + bias_ref[j, 0])
        return parts

    tasks = [(qi, j) for qi in reversed(range(seq // TQ)) for j in range(2)]
    nxt = logits(*tasks[0])
    outs = []
    for t, (qi, j) in enumerate(tasks):
        r0 = qi * TQ
        parts = nxt
        if t + 1 < len(tasks):
            nxt = logits(*tasks[t + 1])
        outs.append(_softmax_pv(parts, v_ref[0, 0:r0 + TQ, :]))
        if j == 0:
            continue
        o = outs[-2] - lam * outs[-1]
        ms = jnp.mean(o * o, axis=1, keepdims=True)
        o = o * lax.rsqrt(ms + RMS_EPS) * g_ref[...] * (1.0 - lam_init)
        z = z_ref[0, r0:r0 + TQ, :].astype(F32)
        o_ref[0, r0:r0 + TQ, :] = (o * _silu(z)).astype(BF16)


def _diff_attention(proj, lamv, subln_g, bias, lam_init):
    b, s, _ = proj.shape
    w = 2 * HEAD_DIM
    nh = D_INNER // w
    return pl.pallas_call(
        functools.partial(_diff_kernel, lam_init=lam_init, seq=s),
        grid=(b, DIFF_HEADS),
        in_specs=[
            pl.BlockSpec((4, HEAD_DIM), lambda i, h: (0, 0)),
            pl.BlockSpec((1, w), lambda i, h: (0, 0)),
            pl.BlockSpec((1, s, w), lambda i, h: (i, 0, h)),
            pl.BlockSpec((1, s, w), lambda i, h: (i, 0, nh + h)),
            pl.BlockSpec((1, s, w), lambda i, h: (i, 0, 2 * nh + h)),
            pl.BlockSpec((1, s, w), lambda i, h: (i, 0, 3 * nh + h)),
            pl.BlockSpec((2, 2, TQ, TQ), lambda i, h: (h, 0, 0, 0)),
        ],
        out_specs=pl.BlockSpec((1, s, w), lambda i, h: (i, 0, h)),
        out_shape=jax.ShapeDtypeStruct((b, s, D_INNER), BF16),
        compiler_params=pltpu.CompilerParams(
            dimension_semantics=("arbitrary", "arbitrary"),
            vmem_limit_bytes=VMEM_LIMIT),
        name="diff_attention",
    )(lamv, subln_g.reshape(1, w), proj, proj, proj, proj, bias)


def _block_indicator(seq):
    n_kb = seq // MOBA_BLOCK
    ind = np.zeros((HEAD_DIM, seq), np.float32)
    for j in range(n_kb):
        ind[j, j * MOBA_BLOCK:(j + 1) * MOBA_BLOCK] = 1.0
    return ind


HEADS_PER_STEP = 2


def _moba_kernel(ind_ref, kind_ref, q_ref, k_ref, v_ref, z_ref, bias_ref, o_ref, *, seq):
    n_kb = seq // MOBA_BLOCK
    sub = lax.broadcasted_iota(jnp.int32, (n_kb, TQ), 0)

    def block_means(hd):
        c0 = hd * HEAD_DIM
        ksum = jnp.dot(ind_ref[...], k_ref[0, :, c0:c0 + HEAD_DIM], preferred_element_type=F32)
        kmean = ksum * (1.0 / MOBA_BLOCK)
        hi = kmean.astype(BF16)
        lo = (kmean - hi.astype(F32)).astype(BF16)
        return jnp.concatenate([hi, lo], axis=1)

    km2 = [block_means(hd) for hd in range(HEADS_PER_STEP)]

    def block_mask(hd, qi, q):
        gate = _dot_nt(km2[hd], jnp.concatenate([q, q], axis=1))[0:n_kb]
        rank = jnp.zeros((n_kb, TQ), F32)
        for bp in range(qi):
            gb = gate[bp:bp + 1, :]
            rank = rank + jnp.where(sub > bp, jnp.where(gb >= gate, 1.0, 0.0), jnp.where(gb > gate, 1.0, 0.0))
        mt = jnp.where(sub >= qi, 0.0, jnp.where(rank < float(MOBA_TOPK), 0.0, NEG))
        mt = jnp.concatenate([mt, jnp.zeros((HEAD_DIM - n_kb, TQ), F32)], axis=0)
        return mt.T.astype(BF16)

    def logits(hd, qi):
        r0 = qi * TQ
        c0 = hd * HEAD_DIM
        q = q_ref[0, r0:r0 + TQ, c0:c0 + HEAD_DIM]
        masked = qi > MOBA_TOPK
        if masked:
            q = jnp.concatenate([q, block_mask(hd, qi, q)], axis=1)
        parts = []
        for j in range(qi + 1):
            kj = k_ref[0, j * TQ:(j + 1) * TQ, c0:c0 + HEAD_DIM]
            if masked:
                kj = jnp.concatenate([kj, kind_ref[j * TQ:(j + 1) * TQ, :]], axis=1)
            sj = _dot_nt(q, kj)
            if j >= qi - 1:
                sj = sj + bias_ref[hd, qi - j]
            parts.append(sj)
        return parts

    order = [MOBA_TOPK] + [qi for qi in reversed(range(n_kb)) if qi != MOBA_TOPK]
    tasks = [(hd, qi) for hd in range(HEADS_PER_STEP) for qi in order]
    nxt = logits(*tasks[0])
    for t, (hd, qi) in enumerate(tasks):
        r0 = qi * TQ
        c0 = hd * HEAD_DIM
        parts = nxt
        if t + 1 < len(tasks):
            nxt = logits(*tasks[t + 1])
        o = _softmax_pv(parts, v_ref[0, 0:r0 + TQ, c0:c0 + HEAD_DIM])
        z = z_ref[0, r0:r0 + TQ, c0:c0 + HEAD_DIM].astype(F32)
        o_ref[0, r0:r0 + TQ, c0:c0 + HEAD_DIM] = (o * _silu(z)).astype(BF16)


def _head_specs(s, nh):
    w = HEADS_PER_STEP * HEAD_DIM
    ng = nh // HEADS_PER_STEP
    return [
        pl.BlockSpec((1, s, w), lambda i, h: (i, 0, h)),
        pl.BlockSpec((1, s, w), lambda i, h: (i, 0, ng + h)),
        pl.BlockSpec((1, s, w), lambda i, h: (i, 0, 2 * ng + h)),
        pl.BlockSpec((1, s, w), lambda i, h: (i, 0, 3 * ng + h)),
    ]


def _moba_attention(proj, bias):
    b, s, _ = proj.shape
    assert s // MOBA_BLOCK == 8 and MOBA_BLOCK == TQ
    ind = _block_indicator(s)
    return pl.pallas_call(
        functools.partial(_moba_kernel, seq=s),
        grid=(b, MOBA_HEADS // HEADS_PER_STEP),
        in_specs=[pl.BlockSpec((HEAD_DIM, s), lambda i, h: (0, 0)),
                  pl.BlockSpec((s, HEAD_DIM), lambda i, h: (0, 0))]
        + _head_specs(s, MOBA_HEADS)
        + [pl.BlockSpec((HEADS_PER_STEP, 2, TQ, TQ), lambda i, h: (h, 0, 0, 0))],
        out_specs=pl.BlockSpec((1, s, HEADS_PER_STEP * HEAD_DIM), lambda i, h: (i, 0, h)),
        out_shape=jax.ShapeDtypeStruct((b, s, D_INNER), BF16),
        compiler_params=pltpu.CompilerParams(
            dimension_semantics=("arbitrary", "arbitrary"),
            vmem_limit_bytes=VMEM_LIMIT),
        name="moba_attention",
    )(jnp.asarray(ind, BF16), jnp.asarray(ind.T, BF16), proj, proj, proj, proj, bias)


def _fox_gate_kernel(x_ref, wf_ref, bf_ref, row_ref, col_ref, *, seq):
    f = _dot_nt(x_ref[0], wf_ref[...]) + bf_ref[...]
    c = jnp.minimum(f, 0.0) - jnp.log1p(jnp.exp(-jnp.abs(f)))
    c = c.T[0:FOX_HEADS]
    lane = lax.broadcasted_iota(jnp.int32, c.shape, 1)
    d = 1
    while d < seq:
        c = c + jnp.where(lane >= d, pltpu.roll(c, d, 1), 0.0)
        d *= 2
    c = c * LOG2E
    row_ref[0] = c
    col_ref[0] = jnp.concatenate([c, jnp.zeros((HEAD_DIM - FOX_HEADS, seq), F32)], axis=0).T


def _fox_gate(h_bf3, wf, bf_row):
    b, s, d = h_bf3.shape
    return pl.pallas_call(
        functools.partial(_fox_gate_kernel, seq=s),
        grid=(b,),
        in_specs=[
            pl.BlockSpec((1, s, d), lambda i: (i, 0, 0)),
            pl.BlockSpec((HEAD_DIM, d), lambda i: (0, 0)),
            pl.BlockSpec((1, HEAD_DIM), lambda i: (0, 0)),
        ],
        out_specs=[pl.BlockSpec((1, FOX_HEADS, s), lambda i: (i, 0, 0)),
                   pl.BlockSpec((1, s, HEAD_DIM), lambda i: (i, 0, 0))],
        out_shape=[jax.ShapeDtypeStruct((b, FOX_HEADS, s), F32),
                   jax.ShapeDtypeStruct((b, s, HEAD_DIM), F32)],
        compiler_params=pltpu.CompilerParams(
            dimension_semantics=("arbitrary",),
            vmem_limit_bytes=VMEM_LIMIT),
        name="fox_gate",
    )(h_bf3, wf, bf_row)


def _causal_tile():
    r = np.arange(TQ)[:, None]
    c = np.arange(TQ)[None, :]
    return np.where(c <= r, 0.0, NEG).astype(np.float32)


def _fox_kernel(mask_ref, crow_ref, ccol_ref, q_ref, k_ref, v_ref, z_ref, o_ref, *, seq):
    def logits(hd, qi):
        r0 = qi * TQ
        c0 = hd * HEAD_DIM
        q = q_ref[0, r0:r0 + TQ, c0:c0 + HEAD_DIM]
        parts = []
        if qi >= 1:
            parts.append(_dot_nt(q, k_ref[0, 0:r0, c0:c0 + HEAD_DIM]) - crow_ref[0, hd, :, 0:r0])
        parts.append(_dot_nt(q, k_ref[0, r0:r0 + TQ, c0:c0 + HEAD_DIM]) - crow_ref[0, hd, :, r0:r0 + TQ]
                     + mask_ref[...])
        return parts

    nq = seq // TQ
    lane = lax.broadcasted_iota(jnp.int32, (TQ, HEAD_DIM), 1)
    tasks = [(hd, qi) for hd in range(HEADS_PER_STEP) for qi in reversed(range(nq))]
    nxt = logits(*tasks[0])
    for t, (hd, qi) in enumerate(tasks):
        r0 = qi * TQ
        c0 = hd * HEAD_DIM
        parts = nxt
        if t + 1 < len(tasks):
            nxt = logits(*tasks[t + 1])
        head_lane = lane == pl.program_id(1) * HEADS_PER_STEP + hd
        c_t = jnp.sum(jnp.where(head_lane, ccol_ref[0, r0:r0 + TQ, :], 0.0), axis=1, keepdims=True)
        o = _softmax_pv(parts, v_ref[0, 0:r0 + TQ, c0:c0 + HEAD_DIM], row_shift=c_t)
        z = z_ref[0, r0:r0 + TQ, c0:c0 + HEAD_DIM].astype(F32)
        o_ref[0, r0:r0 + TQ, c0:c0 + HEAD_DIM] = (o * _silu(z)).astype(BF16)


def _fox_attention(proj, cum_row, cum_col):
    b, s, _ = proj.shape
    crow = cum_row.reshape(b, FOX_HEADS, 1, s)
    mask = jnp.asarray(_causal_tile())
    return pl.pallas_call(
        functools.partial(_fox_kernel, seq=s),
        grid=(b, FOX_HEADS // HEADS_PER_STEP),
        in_specs=[
            pl.BlockSpec((TQ, TQ), lambda i, h: (0, 0)),
            pl.BlockSpec((1, HEADS_PER_STEP, 1, s), lambda i, h: (i, h, 0, 0)),
            pl.BlockSpec((1, s, HEAD_DIM), lambda i, h: (i, 0, 0)),
        ] + _head_specs(s, FOX_HEADS),
        out_specs=pl.BlockSpec((1, s, HEADS_PER_STEP * HEAD_DIM), lambda i, h: (i, 0, h)),
        out_shape=jax.ShapeDtypeStruct((b, s, D_INNER), BF16),
        compiler_params=pltpu.CompilerParams(
            dimension_semantics=("arbitrary", "arbitrary"),
            vmem_limit_bytes=VMEM_LIMIT),
        name="fox_attention",
    )(mask, crow, cum_col, proj, proj, proj, proj)


def _col_scale():
    cs = np.ones((1, 4 * D_INNER), np.float32)
    cs[0, :D_INNER] = QK_SCALE * LOG2E
    return jnp.asarray(cs)


def kernel(x, rel_bias, w_in_0, lam_q1_0, lam_k1_0, lam_q2_0, lam_k2_0, subln_g_0, w_out_0, ln_g_0, ln_b_0, w_in_1, w_out_1, ln_g_1, ln_b_1, w_in_2, b_f_2, w_out_2, ln_g_2, ln_b_2, w_in_3, lam_q1_3, lam_k1_3, lam_q2_3, lam_k2_3, subln_g_3, w_out_3, ln_g_3, ln_b_3):
    b, s, d = x.shape
    m = b * s
    e = D_INNER
    col_scale = _col_scale()
    bias = _bias_tiles(rel_bias)

    h = x.reshape(m, d)
    h_bf = h
    layers = [
        (w_in_0, w_out_0, ln_g_0, ln_b_0),
        (w_in_1, w_out_1, ln_g_1, ln_b_1),
        (w_in_2, w_out_2, ln_g_2, ln_b_2),
        (w_in_3, w_out_3, ln_g_3, ln_b_3),
    ]
    diff_extra = {
        0: (lam_q1_0, lam_k1_0, lam_q2_0, lam_k2_0, subln_g_0),
        3: (lam_q1_3, lam_k1_3, lam_q2_3, lam_k2_3, subln_g_3),
    }
    for i, (w_in, w_out, g, beta) in enumerate(layers):
        kind = i % 3
        if kind == 2:
            w_t = w_in.T
            proj, w_out_bf = _in_proj(h_bf, w_t, col_scale, 4 * e, w_out, w_is_nk=True)
        else:
            proj, w_out_bf = _in_proj(h_bf, w_in, col_scale, 4 * e, w_out)
        proj = proj.reshape(b, s, 4 * e)
        if kind == 0:
            lq1, lk1, lq2, lk2, sg = diff_extra[i]
            lam_init = 0.8 - 0.6 * math.exp(-0.3 * i)
            a = _diff_attention(proj, jnp.stack([lq1, lk1, lq2, lk2]), sg, bias, lam_init)
        elif kind == 1:
            a = _moba_attention(proj, bias)
        else:
            wf = jnp.pad(w_t[4 * e:].astype(BF16), ((0, HEAD_DIM - FOX_HEADS), (0, 0)))
            bf_row = jnp.pad(b_f_2, (0, HEAD_DIM - FOX_HEADS)).reshape(1, HEAD_DIM)
            cum_row, cum_col = _fox_gate(h_bf.reshape(b, s, d), wf, bf_row)
            a = _fox_attention(proj, cum_row, cum_col)
        outs = _out_proj_ln(a.reshape(m, e), w_out_bf, h, g, beta, emit_bf16=i + 1 < DEPTH)
        h, h_bf = outs[0], outs[-1]
    return h.reshape(b, s, d)
```

```python
import functools
import math

import jax
import jax.numpy as jnp
import numpy as np
from jax import lax
from jax.experimental import pallas as pl
from jax.experimental.pallas import tpu as pltpu

D_MODEL = 2048
D_INNER = 2048
DEPTH = 4
DIFF_HEADS = 8
MOBA_HEADS = 16
MOBA_BLOCK = 256
MOBA_TOPK = 3
FOX_HEADS = 16
HEAD_DIM = 128
REL_BUCKETS = 32
REL_MAX_DIST = 128
REL_HEADS = 16
LN_EPS = 1e-5
RMS_EPS = 1e-5
NEG = -1e30
DN_ALPHA = (2.0 * DEPTH) ** 0.25
LOG2E = math.log2(math.e)
QK_SCALE = HEAD_DIM ** -0.5

TQ = 256
DIFF_HEADS_PER_STEP = 2
VMEM_LIMIT = 60 * 1024 * 1024

F32 = jnp.float32
BF16 = jnp.bfloat16


def _dot_nt(a, b):
    return lax.dot_general(a, b, (((1,), (1,)), ((), ())), preferred_element_type=F32)


def _silu(z):
    return z / (1.0 + jnp.exp(-z))


def _inproj_kernel(x_ref, w_ref, cs_ref, wo_ref, o_ref, wo_bf_ref, wbf_ref, *, w_is_nk):
    @pl.when(pl.program_id(1) == 0)
    def _():
        wbf_ref[...] = w_ref[...].astype(BF16)

    x = x_ref[...].astype(BF16)
    if w_is_nk:
        acc = _dot_nt(x, wbf_ref[...])
    else:
        acc = jnp.dot(x, wbf_ref[...], preferred_element_type=F32)
    o_ref[...] = (acc * cs_ref[...]).astype(o_ref.dtype)
    wo_bf_ref[...] = wo_ref[...].astype(BF16)


def _in_proj(x, w, col_scale, n_out, w_out, w_is_nk=False):
    m, k = x.shape
    tm, tn = (2048, 1024) if x.dtype == BF16 else (512, 2048)
    n_tiles, m_tiles = n_out // tn, m // tm
    ko, no = w_out.shape
    slab = ko // (n_tiles * m_tiles)
    if w_is_nk:
        w_block, w_spec = (tn, k), pl.BlockSpec((tn, k), lambda n, i: (n, 0))
    else:
        w_block, w_spec = (k, tn), pl.BlockSpec((k, tn), lambda n, i: (0, n))
    return pl.pallas_call(
        functools.partial(_inproj_kernel, w_is_nk=w_is_nk),
        grid=(n_tiles, m_tiles),
        in_specs=[
            pl.BlockSpec((tm, k), lambda n, i: (i, 0)),
            w_spec,
            pl.BlockSpec((1, tn), lambda n, i: (0, n)),
            pl.BlockSpec((slab, no), lambda n, i: (n * m_tiles + i, 0)),
        ],
        out_specs=[pl.BlockSpec((tm, tn), lambda n, i: (i, n)),
                   pl.BlockSpec((slab, no), lambda n, i: (n * m_tiles + i, 0))],
        out_shape=[jax.ShapeDtypeStruct((m, n_out), BF16),
                   jax.ShapeDtypeStruct((ko, no), BF16)],
        scratch_shapes=[pltpu.VMEM(w_block, BF16)],
        compiler_params=pltpu.CompilerParams(
            dimension_semantics=("arbitrary", "arbitrary"),
            vmem_limit_bytes=VMEM_LIMIT),
        name="in_proj",
    )(x, w, col_scale, w_out)


LN_CHUNKS = 4


def _data_zero(x):
    rows, n = x.shape
    x = jnp.max(x.reshape(rows // 8, 8, n), axis=0)
    bits = pltpu.bitcast(_fold_lanes([x], jnp.maximum), jnp.uint32)
    zero = lax.shift_right_logical(lax.shift_right_logical(bits, jnp.uint32(16)), jnp.uint32(16))
    return pltpu.bitcast(zero, F32)[0:1, 0:1]


def _outproj_ln_kernel(a_ref, w_ref, h_ref, g_ref, b_ref, o_ref, *rest, nblk):
    obf_ref = rest[0] if len(rest) == 3 else None
    y0_ref, y1_ref = rest[-2:]
    i = pl.program_id(0)
    ys = (y0_ref, y1_ref)
    tm, n = o_ref.shape
    rows, cols = tm // LN_CHUNKS, n // LN_CHUNKS

    def layer_norm(y_ref, c):
        rs = slice(c * rows, (c + 1) * rows)
        r = DN_ALPHA * h_ref[rs, :] + y_ref[rs, :]
        mu = jnp.mean(r, axis=1, keepdims=True)
        d = r - mu
        var = jnp.mean(d * d, axis=1, keepdims=True)
        out = d * lax.rsqrt(var + LN_EPS) * g_ref[...] + b_ref[...]
        o_ref[rs, :] = out
        if obf_ref is not None:
            obf_ref[rs, :] = out.astype(BF16)
        return out

    def matmul(y_ref, c, anchor=None):
        cs = slice(c * cols, (c + 1) * cols)
        if anchor is None:
            y = jnp.dot(a_ref[...], w_ref[:, cs], preferred_element_type=F32)
        else:
            head = a_ref[:, 0:HEAD_DIM] + anchor.astype(BF16)
            y = (jnp.dot(head, w_ref[0:HEAD_DIM, cs], preferred_element_type=F32)
                 + jnp.dot(a_ref[:, HEAD_DIM:], w_ref[HEAD_DIM:, cs], preferred_element_type=F32))
        y_ref[:, cs] = y

    @pl.when(i == 0)
    def _():
        for c in range(LN_CHUNKS):
            matmul(ys[0], c)

    for parity in range(2):
        @pl.when(jnp.logical_and(jnp.logical_and(i > 0, i < nblk), i % 2 == parity))
        def _():
            zero = None
            for c in range(LN_CHUNKS):
                matmul(ys[parity], c, anchor=zero)
                zero = _data_zero(layer_norm(ys[1 - parity], c))

    @pl.when(i == nblk)
    def _():
        for c in range(LN_CHUNKS):
            layer_norm(ys[(nblk - 1) % 2], c)


def _out_proj_ln(a_bf, w_bf, h, g, b, emit_bf16, tm=512):
    m, k = a_bf.shape
    n = w_bf.shape[1]
    nblk = m // tm
    cur = lambda i: (jnp.minimum(i, nblk - 1), 0)
    prev = lambda i: (jnp.maximum(i - 1, 0), 0)
    n_out = 2 if emit_bf16 else 1
    return pl.pallas_call(
        functools.partial(_outproj_ln_kernel, nblk=nblk),
        grid=(nblk + 1,),
        in_specs=[
            pl.BlockSpec((tm, k), cur),
            pl.BlockSpec((k, n), lambda i: (0, 0)),
            pl.BlockSpec((tm, n), prev),
            pl.BlockSpec((1, n), lambda i: (0, 0)),
            pl.BlockSpec((1, n), lambda i: (0, 0)),
        ],
        out_specs=[pl.BlockSpec((tm, n), prev)] * n_out,
        out_shape=[jax.ShapeDtypeStruct((m, n), F32), jax.ShapeDtypeStruct((m, n), BF16)][:n_out],
        scratch_shapes=[pltpu.VMEM((tm, n), F32), pltpu.VMEM((tm, n), F32)],
        compiler_params=pltpu.CompilerParams(
            dimension_semantics=("arbitrary",),
            vmem_limit_bytes=VMEM_LIMIT),
        name="out_proj_ln",
    )(a_bf, w_bf, h, g.reshape(1, n), b.reshape(1, n))


BIAS_SPAN = 4 * TQ


def _bucket_profile():
    n = np.maximum(2 * TQ - np.arange(BIAS_SPAN), 0)
    max_exact = REL_BUCKETS // 2
    nf = np.maximum(n, 1).astype(np.float32)
    large = max_exact + (np.log(nf / max_exact) / math.log(REL_MAX_DIST / max_exact)
                         * (REL_BUCKETS - max_exact)).astype(np.int32)
    large = np.minimum(large, REL_BUCKETS - 1)
    return np.where(n < max_exact, n, large).astype(np.int32).reshape(1, BIAS_SPAN)


def _bias_kernel(tab_ref, bkt_ref, o_ref):
    c = pl.program_id(0)
    far = tab_ref[c, REL_BUCKETS - 1]
    bk = bkt_ref[...]
    acc = jnp.zeros((1, BIAS_SPAN), F32)
    for j in range(REL_BUCKETS):
        acc = jnp.where(bk == j, tab_ref[c, j], acc)
    lane = lax.broadcasted_iota(jnp.int32, (1, BIAS_SPAN), 1)
    prof = jnp.where(lane > 2 * TQ, NEG, (acc - far) * LOG2E)
    rolled = pltpu.roll(jnp.broadcast_to(prof, (TQ, BIAS_SPAN)), 0, 1, stride=1, stride_axis=0)
    o_ref[0, 0] = rolled[:, 2 * TQ:3 * TQ]
    o_ref[0, 1] = rolled[:, TQ:2 * TQ]


def _bias_tiles(rel_bias):
    tab = rel_bias.T
    bkt = jnp.asarray(_bucket_profile())
    return pl.pallas_call(
        _bias_kernel,
        grid=(REL_HEADS,),
        in_specs=[
            pl.BlockSpec(memory_space=pltpu.SMEM),
            pl.BlockSpec((1, BIAS_SPAN), lambda c: (0, 0)),
        ],
        out_specs=pl.BlockSpec((1, 2, TQ, TQ), lambda c: (c, 0, 0, 0)),
        out_shape=jax.ShapeDtypeStruct((REL_HEADS, 2, TQ, TQ), F32),
        name="rel_bias_tiles",
    )(tab, bkt)


def _fold_lanes(parts, op):
    acc = None
    for p in parts:
        for t in range(p.shape[1] // HEAD_DIM):
            blk = p[:, t * HEAD_DIM:(t + 1) * HEAD_DIM]
            acc = blk if acc is None else op(acc, blk)
    return acc


def _softmax_pv(parts, v, row_shift=None):
    m = _fold_lanes(parts, jnp.maximum).max(axis=1, keepdims=True)
    if row_shift is None:
        shift = -m
    else:
        m_full = m + row_shift
        shift = row_shift - m_full
    ps = [jnp.exp2(p + shift) for p in parts]
    pb = jnp.concatenate([p.astype(BF16) for p in ps], axis=1) if len(ps) > 1 else ps[0].astype(BF16)
    if v.shape[1] == HEAD_DIM:
        v1 = jnp.concatenate([v, jnp.ones(v.shape, v.dtype)], axis=1)
        acc = jnp.dot(pb, v1, preferred_element_type=F32)
        return acc[:, 0:HEAD_DIM] / acc[:, HEAD_DIM:HEAD_DIM + 1]
    l = _fold_lanes(ps, jnp.add).sum(axis=1, keepdims=True)
    acc = jnp.dot(pb, v, preferred_element_type=F32)
    return acc / l


def _diff_kernel(lamv_ref, g_ref, q_ref, k_ref, v_ref, z_ref, bias_ref, o_ref, *, lam_init, seq):
    lv = lamv_ref[...]
    s1 = jnp.sum(lv[0:1] * lv[1:2], axis=1, keepdims=True)
    s2 = jnp.sum(lv[2:3] * lv[3:4], axis=1, keepdims=True)
    lam = jnp.exp(s1) - jnp.exp(s2) + lam_init

    w = 2 * HEAD_DIM

    def logits(hd, qi, j):
        r0 = qi * TQ
        c0 = hd * w + j * HEAD_DIM
        q = q_ref[0, r0:r0 + TQ, c0:c0 + HEAD_DIM]
        parts = []
        if qi >= 2:
            parts.append(_dot_nt(q, k_ref[0, 0:r0 - TQ, c0:c0 + HEAD_DIM]))
        if qi >= 1:
            parts.append(_dot_nt(q, k_ref[0, r0 - TQ:r0, c0:c0 + HEAD_DIM]) + bias_ref[2 * hd + j, 1])
        parts.append(_dot_nt(q, k_ref[0, r0:r0 + TQ, c0:c0 + HEAD_DIM]) + bias_ref[2 * hd + j, 0])
        return parts

    tasks = [(hd, qi, j) for hd in range(DIFF_HEADS_PER_STEP) for qi in reversed(range(seq // TQ)) for j in range(2)]
    nxt = logits(*tasks[0])
    outs = []
    for t, (hd, qi, j) in enumerate(tasks):
        r0 = qi * TQ
        parts = nxt
        if t + 1 < len(tasks):
            nxt = logits(*tasks[t + 1])
        outs.append(_softmax_pv(parts, v_ref[0, 0:r0 + TQ, hd * w:(hd + 1) * w]))
        if j == 0:
            continue
        o = outs[-2] - lam * outs[-1]
        ms = jnp.mean(o * o, axis=1, keepdims=True)
        o = o * lax.rsqrt(ms + RMS_EPS) * g_ref[...] * (1.0 - lam_init)
        z = z_ref[0, r0:r0 + TQ, hd * w:(hd + 1) * w].astype(F32)
        o_ref[0, r0:r0 + TQ, hd * w:(hd + 1) * w] = (o * _silu(z)).astype(BF16)


def _diff_attention(proj, lamv, subln_g, bias, lam_init):
    b, s, _ = proj.shape
    w = 2 * HEAD_DIM
    wb = DIFF_HEADS_PER_STEP * w
    nh = D_INNER // wb
    return pl.pallas_call(
        functools.partial(_diff_kernel, lam_init=lam_init, seq=s),
        grid=(b, nh),
        in_specs=[
            pl.BlockSpec((4, HEAD_DIM), lambda i, h: (0, 0)),
            pl.BlockSpec((1, w), lambda i, h: (0, 0)),
            pl.BlockSpec((1, s, wb), lambda i, h: (i, 0, h)),
            pl.BlockSpec((1, s, wb), lambda i, h: (i, 0, nh + h)),
            pl.BlockSpec((1, s, wb), lambda i, h: (i, 0, 2 * nh + h)),
            pl.BlockSpec((1, s, wb), lambda i, h: (i, 0, 3 * nh + h)),
            pl.BlockSpec((2 * DIFF_HEADS_PER_STEP, 2, TQ, TQ), lambda i, h: (h, 0, 0, 0)),
        ],
        out_specs=pl.BlockSpec((1, s, wb), lambda i, h: (i, 0, h)),
        out_shape=jax.ShapeDtypeStruct((b, s, D_INNER), BF16),
        compiler_params=pltpu.CompilerParams(
            dimension_semantics=("arbitrary", "arbitrary"),
            vmem_limit_bytes=VMEM_LIMIT),
        name="diff_attention",
    )(lamv, subln_g.reshape(1, w), proj, proj, proj, proj, bias)


def _block_indicator(seq):
    n_kb = seq // MOBA_BLOCK
    ind = np.zeros((HEAD_DIM, seq), np.float32)
    for j in range(n_kb):
        ind[j, j * MOBA_BLOCK:(j + 1) * MOBA_BLOCK] = 1.0
    return ind


HEADS_PER_STEP = 4


def _moba_kernel(ind_ref, kind_ref, q_ref, k_ref, v_ref, z_ref, bias_ref, o_ref, *, seq):
    n_kb = seq // MOBA_BLOCK
    sub = lax.broadcasted_iota(jnp.int32, (n_kb, TQ), 0)

    def block_means(hd):
        c0 = hd * HEAD_DIM
        ksum = jnp.dot(ind_ref[...], k_ref[0, :, c0:c0 + HEAD_DIM], preferred_element_type=F32)
        kmean = ksum * (1.0 / MOBA_BLOCK)
        hi = kmean.astype(BF16)
        lo = (kmean - hi.astype(F32)).astype(BF16)
        return jnp.concatenate([hi, lo], axis=1)

    km2 = [block_means(hd) for hd in range(HEADS_PER_STEP)]

    def block_mask(hd, qi, q):
        gate = _dot_nt(km2[hd], jnp.concatenate([q, q], axis=1))[0:n_kb]
        rank = jnp.zeros((n_kb, TQ), F32)
        for bp in range(qi):
            gb = gate[bp:bp + 1, :]
            rank = rank + jnp.where(sub > bp, jnp.where(gb >= gate, 1.0, 0.0), jnp.where(gb > gate, 1.0, 0.0))
        mt = jnp.where(sub >= qi, 0.0, jnp.where(rank < float(MOBA_TOPK), 0.0, NEG))
        mt = jnp.concatenate([mt, jnp.zeros((HEAD_DIM - n_kb, TQ), F32)], axis=0)
        return mt.T.astype(BF16)

    def logits(hd, qi):
        r0 = qi * TQ
        c0 = hd * HEAD_DIM
        q = q_ref[0, r0:r0 + TQ, c0:c0 + HEAD_DIM]
        masked = qi > MOBA_TOPK
        if masked:
            q = jnp.concatenate([q, block_mask(hd, qi, q)], axis=1)
        parts = []
        for j in range(qi + 1):
            kj = k_ref[0, j * TQ:(j + 1) * TQ, c0:c0 + HEAD_DIM]
            if masked:
                kj = jnp.concatenate([kj, kind_ref[j * TQ:(j + 1) * TQ, :]], axis=1)
            sj = _dot_nt(q, kj)
            if j >= qi - 1:
                sj = sj + bias_ref[hd, qi - j]
            parts.append(sj)
        return parts

    order = [MOBA_TOPK] + [qi for qi in reversed(range(n_kb)) if qi != MOBA_TOPK]
    tasks = [(hd, qi) for hd in range(HEADS_PER_STEP) for qi in order]
    nxt = logits(*tasks[0])
    for t, (hd, qi) in enumerate(tasks):
        r0 = qi * TQ
        c0 = hd * HEAD_DIM
        parts = nxt
        if t + 1 < len(tasks):
            nxt = logits(*tasks[t + 1])
        o = _softmax_pv(parts, v_ref[0, 0:r0 + TQ, c0:c0 + HEAD_DIM])
        z = z_ref[0, r0:r0 + TQ, c0:c0 + HEAD_DIM].astype(F32)
        o_ref[0, r0:r0 + TQ, c0:c0 + HEAD_DIM] = (o * _silu(z)).astype(BF16)


def _head_specs(s, nh):
    w = HEADS_PER_STEP * HEAD_DIM
    ng = nh // HEADS_PER_STEP
    return [
        pl.BlockSpec((1, s, w), lambda i, h: (i, 0, h)),
        pl.BlockSpec((1, s, w), lambda i, h: (i, 0, ng + h)),
        pl.BlockSpec((1, s, w), lambda i, h: (i, 0, 2 * ng + h)),
        pl.BlockSpec((1, s, w), lambda i, h: (i, 0, 3 * ng + h)),
    ]


def _moba_attention(proj, bias):
    b, s, _ = proj.shape
    assert s // MOBA_BLOCK == 8 and MOBA_BLOCK == TQ
    ind = _block_indicator(s)
    return pl.pallas_call(
        functools.partial(_moba_kernel, seq=s),
        grid=(b, MOBA_HEADS // HEADS_PER_STEP),
        in_specs=[pl.BlockSpec((HEAD_DIM, s), lambda i, h: (0, 0)),
                  pl.BlockSpec((s, HEAD_DIM), lambda i, h: (0, 0))]
        + _head_specs(s, MOBA_HEADS)
        + [pl.BlockSpec((HEADS_PER_STEP, 2, TQ, TQ), lambda i, h: (h, 0, 0, 0))],
        out_specs=pl.BlockSpec((1, s, HEADS_PER_STEP * HEAD_DIM), lambda i, h: (i, 0, h)),
        out_shape=jax.ShapeDtypeStruct((b, s, D_INNER), BF16),
        compiler_params=pltpu.CompilerParams(
            dimension_semantics=("arbitrary", "arbitrary"),
            vmem_limit_bytes=VMEM_LIMIT),
        name="moba_attention",
    )(jnp.asarray(ind, BF16), jnp.asarray(ind.T, BF16), proj, proj, proj, proj, bias)


def _fox_gate_kernel(x_ref, wf_ref, bf_ref, row_ref, col_ref, *, seq):
    f = _dot_nt(x_ref[0], wf_ref[...]) + bf_ref[...]
    c = jnp.minimum(f, 0.0) - jnp.log1p(jnp.exp(-jnp.abs(f)))
    c = c.T[0:FOX_HEADS]
    lane = lax.broadcasted_iota(jnp.int32, c.shape, 1)
    d = 1
    while d < seq:
        c = c + jnp.where(lane >= d, pltpu.roll(c, d, 1), 0.0)
        d *= 2
    c = c * LOG2E
    row_ref[0] = c
    col_ref[0] = jnp.concatenate([c, jnp.zeros((HEAD_DIM - FOX_HEADS, seq), F32)], axis=0).T


def _fox_gate(h_bf3, wf, bf_row):
    b, s, d = h_bf3.shape
    return pl.pallas_call(
        functools.partial(_fox_gate_kernel, seq=s),
        grid=(b,),
        in_specs=[
            pl.BlockSpec((1, s, d), lambda i: (i, 0, 0)),
            pl.BlockSpec((HEAD_DIM, d), lambda i: (0, 0)),
            pl.BlockSpec((1, HEAD_DIM), lambda i: (0, 0)),
        ],
        out_specs=[pl.BlockSpec((1, FOX_HEADS, s), lambda i: (i, 0, 0)),
                   pl.BlockSpec((1, s, HEAD_DIM), lambda i: (i, 0, 0))],
        out_shape=[jax.ShapeDtypeStruct((b, FOX_HEADS, s), F32),
                   jax.ShapeDtypeStruct((b, s, HEAD_DIM), F32)],
        compiler_params=pltpu.CompilerParams(
            dimension_semantics=("arbitrary",),
            vmem_limit_bytes=VMEM_LIMIT),
        name="fox_gate",
    )(h_bf3, wf, bf_row)


def _causal_tile():
    r = np.arange(TQ)[:, None]
    c = np.arange(TQ)[None, :]
    return np.where(c <= r, 0.0, NEG).astype(np.float32)


def _fox_kernel(mask_ref, crow_ref, ccol_ref, q_ref, k_ref, v_ref, z_ref, o_ref, *, seq):
    def logits(hd, qi):
        r0 = qi * TQ
        c0 = hd * HEAD_DIM
        q = q_ref[0, r0:r0 + TQ, c0:c0 + HEAD_DIM]
        parts = []
        if qi >= 1:
            parts.append(_dot_nt(q, k_ref[0, 0:r0, c0:c0 + HEAD_DIM]) - crow_ref[0, hd, :, 0:r0])
        parts.append(_dot_nt(q, k_ref[0, r0:r0 + TQ, c0:c0 + HEAD_DIM]) - crow_ref[0, hd, :, r0:r0 + TQ]
                     + mask_ref[...])
        return parts

    nq = seq // TQ
    lane = lax.broadcasted_iota(jnp.int32, (TQ, HEAD_DIM), 1)
    tasks = [(hd, qi) for hd in range(HEADS_PER_STEP) for qi in reversed(range(nq))]
    nxt = logits(*tasks[0])
    for t, (hd, qi) in enumerate(tasks):
        r0 = qi * TQ
        c0 = hd * HEAD_DIM
        parts = nxt
        if t + 1 < len(tasks):
            nxt = logits(*tasks[t + 1])
        head_lane = lane == pl.program_id(1) * HEADS_PER_STEP + hd
        c_t = jnp.sum(jnp.where(head_lane, ccol_ref[0, r0:r0 + TQ, :], 0.0), axis=1, keepdims=True)
        o = _softmax_pv(parts, v_ref[0, 0:r0 + TQ, c0:c0 + HEAD_DIM], row_shift=c_t)
        z = z_ref[0, r0:r0 + TQ, c0:c0 + HEAD_DIM].astype(F32)
        o_ref[0, r0:r0 + TQ, c0:c0 + HEAD_DIM] = (o * _silu(z)).astype(BF16)


def _fox_attention(proj, cum_row, cum_col):
    b, s, _ = proj.shape
    crow = cum_row.reshape(b, FOX_HEADS, 1, s)
    mask = jnp.asarray(_causal_tile())
    return pl.pallas_call(
        functools.partial(_fox_kernel, seq=s),
        grid=(b, FOX_HEADS // HEADS_PER_STEP),
        in_specs=[
            pl.BlockSpec((TQ, TQ), lambda i, h: (0, 0)),
            pl.BlockSpec((1, HEADS_PER_STEP, 1, s), lambda i, h: (i, h, 0, 0)),
            pl.BlockSpec((1, s, HEAD_DIM), lambda i, h: (i, 0, 0)),
        ] + _head_specs(s, FOX_HEADS),
        out_specs=pl.BlockSpec((1, s, HEADS_PER_STEP * HEAD_DIM), lambda i, h: (i, 0, h)),
        out_shape=jax.ShapeDtypeStruct((b, s, D_INNER), BF16),
        compiler_params=pltpu.CompilerParams(
            dimension_semantics=("arbitrary", "arbitrary"),
            vmem_limit_bytes=VMEM_LIMIT),
        name="fox_attention",
    )(mask, crow, cum_col, proj, proj, proj, proj)


def _col_scale():
    cs = np.ones((1, 4 * D_INNER), np.float32)
    cs[0, :D_INNER] = QK_SCALE * LOG2E
    return jnp.asarray(cs)


def kernel(x, rel_bias, w_in_0, lam_q1_0, lam_k1_0, lam_q2_0, lam_k2_0, subln_g_0, w_out_0, ln_g_0, ln_b_0, w_in_1, w_out_1, ln_g_1, ln_b_1, w_in_2, b_f_2, w_out_2, ln_g_2, ln_b_2, w_in_3, lam_q1_3, lam_k1_3, lam_q2_3, lam_k2_3, subln_g_3, w_out_3, ln_g_3, ln_b_3):
    b, s, d = x.shape
    m = b * s
    e = D_INNER
    col_scale = _col_scale()
    bias = _bias_tiles(rel_bias)

    h = x.reshape(m, d)
    h_bf = h
    layers = [
        (w_in_0, w_out_0, ln_g_0, ln_b_0),
        (w_in_1, w_out_1, ln_g_1, ln_b_1),
        (w_in_2, w_out_2, ln_g_2, ln_b_2),
        (w_in_3, w_out_3, ln_g_3, ln_b_3),
    ]
    diff_extra = {
        0: (lam_q1_0, lam_k1_0, lam_q2_0, lam_k2_0, subln_g_0),
        3: (lam_q1_3, lam_k1_3, lam_q2_3, lam_k2_3, subln_g_3),
    }
    for i, (w_in, w_out, g, beta) in enumerate(layers):
        kind = i % 3
        if kind == 2:
            w_t = w_in.T
            proj, w_out_bf = _in_proj(h_bf, w_t, col_scale, 4 * e, w_out, w_is_nk=True)
        else:
            proj, w_out_bf = _in_proj(h_bf, w_in, col_scale, 4 * e, w_out)
        proj = proj.reshape(b, s, 4 * e)
        if kind == 0:
            lq1, lk1, lq2, lk2, sg = diff_extra[i]
            lam_init = 0.8 - 0.6 * math.exp(-0.3 * i)
            a = _diff_attention(proj, jnp.stack([lq1, lk1, lq2, lk2]), sg, bias, lam_init)
        elif kind == 1:
            a = _moba_attention(proj, bias)
        else:
            wf = jnp.pad(w_t[4 * e:].astype(BF16), ((0, HEAD_DIM - FOX_HEADS), (0, 0)))
            bf_row = jnp.pad(b_f_2, (0, HEAD_DIM - FOX_HEADS)).reshape(1, HEAD_DIM)
            cum_row, cum_col = _fox_gate(h_bf.reshape(b, s, d), wf, bf_row)
            a = _fox_attention(proj, cum_row, cum_col)
        outs = _out_proj_ln(a.reshape(m, e), w_out_bf, h, g, beta, emit_bf16=i + 1 < DEPTH)
        h, h_bf = outs[0], outs[-1]
    return h.reshape(b, s, d)
```

```python
import functools
import math

import jax
import jax.numpy as jnp
import numpy as np
from jax import lax
from jax.experimental import pallas as pl
from jax.experimental.pallas import tpu as pltpu

D_MODEL = 2048
D_INNER = 2048
DEPTH = 4
DIFF_HEADS = 8
MOBA_HEADS = 16
MOBA_BLOCK = 256
MOBA_TOPK = 3
FOX_HEADS = 16
HEAD_DIM = 128
REL_BUCKETS = 32
REL_MAX_DIST = 128
REL_HEADS = 16
LN_EPS = 1e-5
RMS_EPS = 1e-5
NEG = -1e30
DN_ALPHA = (2.0 * DEPTH) ** 0.25
LOG2E = math.log2(math.e)
QK_SCALE = HEAD_DIM ** -0.5

TQ = 256
DIFF_HEADS_PER_STEP = 1
VMEM_LIMIT = 60 * 1024 * 1024

F32 = jnp.float32
BF16 = jnp.bfloat16


def _dot_nt(a, b):
    return lax.dot_general(a, b, (((1,), (1,)), ((), ())), preferred_element_type=F32)


def _silu(z):
    return z / (1.0 + jnp.exp(-z))


def _inproj_kernel(x_ref, w_ref, cs_ref, wo_ref, o_ref, wo_bf_ref, wbf_ref, *, w_is_nk):
    @pl.when(pl.program_id(1) == 0)
    def _():
        wbf_ref[...] = w_ref[...].astype(BF16)

    x = x_ref[...].astype(BF16)
    if w_is_nk:
        acc = _dot_nt(x, wbf_ref[...])
    else:
        acc = jnp.dot(x, wbf_ref[...], preferred_element_type=F32)
    o_ref[...] = (acc * cs_ref[...]).astype(o_ref.dtype)
    wo_bf_ref[...] = wo_ref[...].astype(BF16)


def _in_proj(x, w, col_scale, n_out, w_out, w_is_nk=False):
    m, k = x.shape
    tm, tn = (2048, 1024) if x.dtype == BF16 else (512, 2048)
    n_tiles, m_tiles = n_out // tn, m // tm
    ko, no = w_out.shape
    slab = ko // (n_tiles * m_tiles)
    if w_is_nk:
        w_block, w_spec = (tn, k), pl.BlockSpec((tn, k), lambda n, i: (n, 0))
    else:
        w_block, w_spec = (k, tn), pl.BlockSpec((k, tn), lambda n, i: (0, n))
    return pl.pallas_call(
        functools.partial(_inproj_kernel, w_is_nk=w_is_nk),
        grid=(n_tiles, m_tiles),
        in_specs=[
            pl.BlockSpec((tm, k), lambda n, i: (i, 0)),
            w_spec,
            pl.BlockSpec((1, tn), lambda n, i: (0, n)),
            pl.BlockSpec((slab, no), lambda n, i: (n * m_tiles + i, 0)),
        ],
        out_specs=[pl.BlockSpec((tm, tn), lambda n, i: (i, n)),
                   pl.BlockSpec((slab, no), lambda n, i: (n * m_tiles + i, 0))],
        out_shape=[jax.ShapeDtypeStruct((m, n_out), BF16),
                   jax.ShapeDtypeStruct((ko, no), BF16)],
        scratch_shapes=[pltpu.VMEM(w_block, BF16)],
        compiler_params=pltpu.CompilerParams(
            dimension_semantics=("parallel", "arbitrary"),
            vmem_limit_bytes=VMEM_LIMIT),
        name="in_proj",
    )(x, w, col_scale, w_out)


LN_CHUNKS = 4


def _data_zero(x):
    rows, n = x.shape
    x = jnp.max(x.reshape(rows // 8, 8, n), axis=0)
    bits = pltpu.bitcast(_fold_lanes([x], jnp.maximum), jnp.uint32)
    zero = lax.shift_right_logical(lax.shift_right_logical(bits, jnp.uint32(16)), jnp.uint32(16))
    return pltpu.bitcast(zero, F32)[0:1, 0:1]


def _outproj_ln_kernel(a_ref, w_ref, h_ref, g_ref, b_ref, o_ref, *rest, nblk):
    obf_ref = rest[0] if len(rest) == 3 else None
    y0_ref, y1_ref = rest[-2:]
    i = pl.program_id(0)
    ys = (y0_ref, y1_ref)
    tm, n = o_ref.shape
    rows, cols = tm // LN_CHUNKS, n // LN_CHUNKS

    def layer_norm(y_ref, c):
        rs = slice(c * rows, (c + 1) * rows)
        r = DN_ALPHA * h_ref[rs, :] + y_ref[rs, :]
        mu = jnp.mean(r, axis=1, keepdims=True)
        d = r - mu
        var = jnp.mean(d * d, axis=1, keepdims=True)
        out = d * lax.rsqrt(var + LN_EPS) * g_ref[...] + b_ref[...]
        o_ref[rs, :] = out
        if obf_ref is not None:
            obf_ref[rs, :] = out.astype(BF16)
        return out

    def matmul(y_ref, c, anchor=None):
        cs = slice(c * cols, (c + 1) * cols)
        if anchor is None:
            y = jnp.dot(a_ref[...], w_ref[:, cs], preferred_element_type=F32)
        else:
            head = a_ref[:, 0:HEAD_DIM] + anchor.astype(BF16)
            y = (jnp.dot(head, w_ref[0:HEAD_DIM, cs], preferred_element_type=F32)
                 + jnp.dot(a_ref[:, HEAD_DIM:], w_ref[HEAD_DIM:, cs], preferred_element_type=F32))
        y_ref[:, cs] = y

    @pl.when(i == 0)
    def _():
        for c in range(LN_CHUNKS):
            matmul(ys[0], c)

    for parity in range(2):
        @pl.when(jnp.logical_and(jnp.logical_and(i > 0, i < nblk), i % 2 == parity))
        def _():
            zero = None
            for c in range(LN_CHUNKS):
                matmul(ys[parity], c, anchor=zero)
                zero = _data_zero(layer_norm(ys[1 - parity], c))

    @pl.when(i == nblk)
    def _():
        for c in range(LN_CHUNKS):
            layer_norm(ys[(nblk - 1) % 2], c)


def _out_proj_ln(a_bf, w_bf, h, g, b, emit_bf16, tm=512):
    m, k = a_bf.shape
    n = w_bf.shape[1]
    nblk = m // tm
    cur = lambda i: (jnp.minimum(i, nblk - 1), 0)
    prev = lambda i: (jnp.maximum(i - 1, 0), 0)
    n_out = 2 if emit_bf16 else 1
    return pl.pallas_call(
        functools.partial(_outproj_ln_kernel, nblk=nblk),
        grid=(nblk + 1,),
        in_specs=[
            pl.BlockSpec((tm, k), cur),
            pl.BlockSpec((k, n), lambda i: (0, 0)),
            pl.BlockSpec((tm, n), prev),
            pl.BlockSpec((1, n), lambda i: (0, 0)),
            pl.BlockSpec((1, n), lambda i: (0, 0)),
        ],
        out_specs=[pl.BlockSpec((tm, n), prev)] * n_out,
        out_shape=[jax.ShapeDtypeStruct((m, n), F32), jax.ShapeDtypeStruct((m, n), BF16)][:n_out],
        scratch_shapes=[pltpu.VMEM((tm, n), F32), pltpu.VMEM((tm, n), F32)],
        compiler_params=pltpu.CompilerParams(
            dimension_semantics=("arbitrary",),
            vmem_limit_bytes=VMEM_LIMIT),
        name="out_proj_ln",
    )(a_bf, w_bf, h, g.reshape(1, n), b.reshape(1, n))


BIAS_SPAN = 4 * TQ


def _bucket_profile():
    n = np.maximum(2 * TQ - np.arange(BIAS_SPAN), 0)
    max_exact = REL_BUCKETS // 2
    nf = np.maximum(n, 1).astype(np.float32)
    large = max_exact + (np.log(nf / max_exact) / math.log(REL_MAX_DIST / max_exact)
                         * (REL_BUCKETS - max_exact)).astype(np.int32)
    large = np.minimum(large, REL_BUCKETS - 1)
    return np.where(n < max_exact, n, large).astype(np.int32).reshape(1, BIAS_SPAN)


def _bias_kernel(tab_ref, bkt_ref, o_ref):
    c = pl.program_id(0)
    far = tab_ref[c, REL_BUCKETS - 1]
    bk = bkt_ref[...]
    acc = jnp.zeros((1, BIAS_SPAN), F32)
    for j in range(REL_BUCKETS):
        acc = jnp.where(bk == j, tab_ref[c, j], acc)
    lane = lax.broadcasted_iota(jnp.int32, (1, BIAS_SPAN), 1)
    prof = jnp.where(lane > 2 * TQ, NEG, (acc - far) * LOG2E)
    rolled = pltpu.roll(jnp.broadcast_to(prof, (TQ, BIAS_SPAN)), 0, 1, stride=1, stride_axis=0)
    o_ref[0, 0] = rolled[:, 2 * TQ:3 * TQ]
    o_ref[0, 1] = rolled[:, TQ:2 * TQ]


def _bias_tiles(rel_bias):
    tab = rel_bias.T
    bkt = jnp.asarray(_bucket_profile())
    return pl.pallas_call(
        _bias_kernel,
        grid=(REL_HEADS,),
        in_specs=[
            pl.BlockSpec(memory_space=pltpu.SMEM),
            pl.BlockSpec((1, BIAS_SPAN), lambda c: (0, 0)),
        ],
        out_specs=pl.BlockSpec((1, 2, TQ, TQ), lambda c: (c, 0, 0, 0)),
        out_shape=jax.ShapeDtypeStruct((REL_HEADS, 2, TQ, TQ), F32),
        name="rel_bias_tiles",
    )(tab, bkt)


def _fold_lanes(parts, op):
    acc = None
    for p in parts:
        for t in range(p.shape[1] // HEAD_DIM):
            blk = p[:, t * HEAD_DIM:(t + 1) * HEAD_DIM]
            acc = blk if acc is None else op(acc, blk)
    return acc


def _softmax_pv(parts, v, row_shift=None):
    m = _fold_lanes(parts, jnp.maximum).max(axis=1, keepdims=True)
    if row_shift is None:
        shift = -m
    else:
        m_full = m + row_shift
        shift = row_shift - m_full
    ps = [jnp.exp2(p + shift) for p in parts]
    pb = jnp.concatenate([p.astype(BF16) for p in ps], axis=1) if len(ps) > 1 else ps[0].astype(BF16)
    if v.shape[1] == HEAD_DIM:
        v1 = jnp.concatenate([v, jnp.ones(v.shape, v.dtype)], axis=1)
        acc = jnp.dot(pb, v1, preferred_element_type=F32)
        return acc[:, 0:HEAD_DIM] / acc[:, HEAD_DIM:HEAD_DIM + 1]
    l = _fold_lanes(ps, jnp.add).sum(axis=1, keepdims=True)
    acc = jnp.dot(pb, v, preferred_element_type=F32)
    return acc / l


def _diff_kernel(lamv_ref, g_ref, q_ref, k_ref, v_ref, z_ref, bias_ref, o_ref, *, lam_init, seq):
    lv = lamv_ref[...]
    s1 = jnp.sum(lv[0:1] * lv[1:2], axis=1, keepdims=True)
    s2 = jnp.sum(lv[2:3] * lv[3:4], axis=1, keepdims=True)
    lam = jnp.exp(s1) - jnp.exp(s2) + lam_init

    w = 2 * HEAD_DIM

    def logits(hd, qi, j):
        r0 = qi * TQ
        c0 = hd * w + j * HEAD_DIM
        q = q_ref[0, r0:r0 + TQ, c0:c0 + HEAD_DIM]
        parts = []
        if qi >= 2:
            parts.append(_dot_nt(q, k_ref[0, 0:r0 - TQ, c0:c0 + HEAD_DIM]))
        if qi >= 1:
            parts.append(_dot_nt(q, k_ref[0, r0 - TQ:r0, c0:c0 + HEAD_DIM]) + bias_ref[2 * hd + j, 1])
        parts.append(_dot_nt(q, k_ref[0, r0:r0 + TQ, c0:c0 + HEAD_DIM]) + bias_ref[2 * hd + j, 0])
        return parts

    tasks = [(hd, qi, j) for hd in range(DIFF_HEADS_PER_STEP) for qi in reversed(range(seq // TQ)) for j in range(2)]
    nxt = logits(*tasks[0])
    outs = []
    for t, (hd, qi, j) in enumerate(tasks):
        r0 = qi * TQ
        parts = nxt
        if t + 1 < len(tasks):
            nxt = logits(*tasks[t + 1])
        outs.append(_softmax_pv(parts, v_ref[0, 0:r0 + TQ, hd * w:(hd + 1) * w]))
        if j == 0:
            continue
        o = outs[-2] - lam * outs[-1]
        ms = jnp.mean(o * o, axis=1, keepdims=True)
        o = o * lax.rsqrt(ms + RMS_EPS) * g_ref[...] * (1.0 - lam_init)
        z = z_ref[0, r0:r0 + TQ, hd * w:(hd + 1) * w].astype(F32)
        o_ref[0, r0:r0 + TQ, hd * w:(hd + 1) * w] = (o * _silu(z)).astype(BF16)


def _diff_attention(proj, lamv, subln_g, bias, lam_init):
    b, s, _ = proj.shape
    w = 2 * HEAD_DIM
    wb = DIFF_HEADS_PER_STEP * w
    nh = D_INNER // wb
    return pl.pallas_call(
        functools.partial(_diff_kernel, lam_init=lam_init, seq=s),
        grid=(b, nh),
        in_specs=[
            pl.BlockSpec((4, HEAD_DIM), lambda i, h: (0, 0)),
            pl.BlockSpec((1, w), lambda i, h: (0, 0)),
            pl.BlockSpec((1, s, wb), lambda i, h: (i, 0, h)),
            pl.BlockSpec((1, s, wb), lambda i, h: (i, 0, nh + h)),
            pl.BlockSpec((1, s, wb), lambda i, h: (i, 0, 2 * nh + h)),
            pl.BlockSpec((1, s, wb), lambda i, h: (i, 0, 3 * nh + h)),
            pl.BlockSpec((2 * DIFF_HEADS_PER_STEP, 2, TQ, TQ), lambda i, h: (h, 0, 0, 0)),
        ],
        out_specs=pl.BlockSpec((1, s, wb), lambda i, h: (i, 0, h)),
        out_shape=jax.ShapeDtypeStruct((b, s, D_INNER), BF16),
        compiler_params=pltpu.CompilerParams(
            dimension_semantics=("parallel", "parallel"),
            vmem_limit_bytes=VMEM_LIMIT),
        name="diff_attention",
    )(lamv, subln_g.reshape(1, w), proj, proj, proj, proj, bias)


def _block_indicator(seq):
    n_kb = seq // MOBA_BLOCK
    ind = np.zeros((HEAD_DIM, seq), np.float32)
    for j in range(n_kb):
        ind[j, j * MOBA_BLOCK:(j + 1) * MOBA_BLOCK] = 1.0
    return ind


HEADS_PER_STEP = 2


def _moba_kernel(ind_ref, kind_ref, q_ref, k_ref, v_ref, z_ref, bias_ref, o_ref, *, seq):
    n_kb = seq // MOBA_BLOCK
    sub = lax.broadcasted_iota(jnp.int32, (n_kb, TQ), 0)

    def block_means(hd):
        c0 = hd * HEAD_DIM
        ksum = jnp.dot(ind_ref[...], k_ref[0, :, c0:c0 + HEAD_DIM], preferred_element_type=F32)
        kmean = ksum * (1.0 / MOBA_BLOCK)
        hi = kmean.astype(BF16)
        lo = (kmean - hi.astype(F32)).astype(BF16)
        return jnp.concatenate([hi, lo], axis=1)

    km2 = [block_means(hd) for hd in range(HEADS_PER_STEP)]

    def block_mask(hd, qi, q):
        gate = _dot_nt(km2[hd], jnp.concatenate([q, q], axis=1))[0:n_kb]
        rank = jnp.zeros((n_kb, TQ), F32)
        for bp in range(qi):
            gb = gate[bp:bp + 1, :]
            rank = rank + jnp.where(sub > bp, jnp.where(gb >= gate, 1.0, 0.0), jnp.where(gb > gate, 1.0, 0.0))
        mt = jnp.where(sub >= qi, 0.0, jnp.where(rank < float(MOBA_TOPK), 0.0, NEG))
        mt = jnp.concatenate([mt, jnp.zeros((HEAD_DIM - n_kb, TQ), F32)], axis=0)
        return mt.T.astype(BF16)

    def logits(hd, qi):
        r0 = qi * TQ
        c0 = hd * HEAD_DIM
        q = q_ref[0, r0:r0 + TQ, c0:c0 + HEAD_DIM]
        masked = qi > MOBA_TOPK
        if masked:
            q = jnp.concatenate([q, block_mask(hd, qi, q)], axis=1)
        parts = []
        for j in range(qi + 1):
            kj = k_ref[0, j * TQ:(j + 1) * TQ, c0:c0 + HEAD_DIM]
            if masked:
                kj = jnp.concatenate([kj, kind_ref[j * TQ:(j + 1) * TQ, :]], axis=1)
            sj = _dot_nt(q, kj)
            if j >= qi - 1:
                sj = sj + bias_ref[hd, qi - j]
            parts.append(sj)
        return parts

    order = [MOBA_TOPK] + [qi for qi in reversed(range(n_kb)) if qi != MOBA_TOPK]
    tasks = [(hd, qi) for hd in range(HEADS_PER_STEP) for qi in order]
    nxt = logits(*tasks[0])
    for t, (hd, qi) in enumerate(tasks):
        r0 = qi * TQ
        c0 = hd * HEAD_DIM
        parts = nxt
        if t + 1 < len(tasks):
            nxt = logits(*tasks[t + 1])
        o = _softmax_pv(parts, v_ref[0, 0:r0 + TQ, c0:c0 + HEAD_DIM])
        z = z_ref[0, r0:r0 + TQ, c0:c0 + HEAD_DIM].astype(F32)
        o_ref[0, r0:r0 + TQ, c0:c0 + HEAD_DIM] = (o * _silu(z)).astype(BF16)


def _head_specs(s, nh):
    w = HEADS_PER_STEP * HEAD_DIM
    ng = nh // HEADS_PER_STEP
    return [
        pl.BlockSpec((1, s, w), lambda i, h: (i, 0, h)),
        pl.BlockSpec((1, s, w), lambda i, h: (i, 0, ng + h)),
        pl.BlockSpec((1, s, w), lambda i, h: (i, 0, 2 * ng + h)),
        pl.BlockSpec((1, s, w), lambda i, h: (i, 0, 3 * ng + h)),
    ]


def _moba_attention(proj, bias):
    b, s, _ = proj.shape
    assert s // MOBA_BLOCK == 8 and MOBA_BLOCK == TQ
    ind = _block_indicator(s)
    return pl.pallas_call(
        functools.partial(_moba_kernel, seq=s),
        grid=(b, MOBA_HEADS // HEADS_PER_STEP),
        in_specs=[pl.BlockSpec((HEAD_DIM, s), lambda i, h: (0, 0)),
                  pl.BlockSpec((s, HEAD_DIM), lambda i, h: (0, 0))]
        + _head_specs(s, MOBA_HEADS)
        + [pl.BlockSpec((HEADS_PER_STEP, 2, TQ, TQ), lambda i, h: (h, 0, 0, 0))],
        out_specs=pl.BlockSpec((1, s, HEADS_PER_STEP * HEAD_DIM), lambda i, h: (i, 0, h)),
        out_shape=jax.ShapeDtypeStruct((b, s, D_INNER), BF16),
        compiler_params=pltpu.CompilerParams(
            dimension_semantics=("parallel", "parallel"),
            vmem_limit_bytes=VMEM_LIMIT),
        name="moba_attention",
    )(jnp.asarray(ind, BF16), jnp.asarray(ind.T, BF16), proj, proj, proj, proj, bias)


def _fox_gate_kernel(x_ref, wf_ref, bf_ref, row_ref, col_ref, *, seq):
    f = _dot_nt(x_ref[0], wf_ref[...]) + bf_ref[...]
    c = jnp.minimum(f, 0.0) - jnp.log1p(jnp.exp(-jnp.abs(f)))
    c = c.T[0:FOX_HEADS]
    lane = lax.broadcasted_iota(jnp.int32, c.shape, 1)
    d = 1
    while d < seq:
        c = c + jnp.where(lane >= d, pltpu.roll(c, d, 1), 0.0)
        d *= 2
    c = c * LOG2E
    row_ref[0] = c
    col_ref[0] = jnp.concatenate([c, jnp.zeros((HEAD_DIM - FOX_HEADS, seq), F32)], axis=0).T


def _fox_gate(h_bf3, wf, bf_row):
    b, s, d = h_bf3.shape
    return pl.pallas_call(
        functools.partial(_fox_gate_kernel, seq=s),
        grid=(b,),
        in_specs=[
            pl.BlockSpec((1, s, d), lambda i: (i, 0, 0)),
            pl.BlockSpec((HEAD_DIM, d), lambda i: (0, 0)),
            pl.BlockSpec((1, HEAD_DIM), lambda i: (0, 0)),
        ],
        out_specs=[pl.BlockSpec((1, FOX_HEADS, s), lambda i: (i, 0, 0)),
                   pl.BlockSpec((1, s, HEAD_DIM), lambda i: (i, 0, 0))],
        out_shape=[jax.ShapeDtypeStruct((b, FOX_HEADS, s), F32),
                   jax.ShapeDtypeStruct((b, s, HEAD_DIM), F32)],
        compiler_params=pltpu.CompilerParams(
            dimension_semantics=("arbitrary",),
            vmem_limit_bytes=VMEM_LIMIT),
        name="fox_gate",
    )(h_bf3, wf, bf_row)


def _causal_tile():
    r = np.arange(TQ)[:, None]
    c = np.arange(TQ)[None, :]
    return np.where(c <= r, 0.0, NEG).astype(np.float32)


def _fox_kernel(mask_ref, crow_ref, ccol_ref, q_ref, k_ref, v_ref, z_ref, o_ref, *, seq):
    def logits(hd, qi):
        r0 = qi * TQ
        c0 = hd * HEAD_DIM
        q = q_ref[0, r0:r0 + TQ, c0:c0 + HEAD_DIM]
        parts = []
        if qi >= 1:
            parts.append(_dot_nt(q, k_ref[0, 0:r0, c0:c0 + HEAD_DIM]) - crow_ref[0, hd, :, 0:r0])
        parts.append(_dot_nt(q, k_ref[0, r0:r0 + TQ, c0:c0 + HEAD_DIM]) - crow_ref[0, hd, :, r0:r0 + TQ]
                     + mask_ref[...])
        return parts

    nq = seq // TQ
    lane = lax.broadcasted_iota(jnp.int32, (TQ, HEAD_DIM), 1)
    tasks = [(hd, qi) for hd in range(HEADS_PER_STEP) for qi in reversed(range(nq))]
    nxt = logits(*tasks[0])
    for t, (hd, qi) in enumerate(tasks):
        r0 = qi * TQ
        c0 = hd * HEAD_DIM
        parts = nxt
        if t + 1 < len(tasks):
            nxt = logits(*tasks[t + 1])
        head_lane = lane == pl.program_id(1) * HEADS_PER_STEP + hd
        c_t = jnp.sum(jnp.where(head_lane, ccol_ref[0, r0:r0 + TQ, :], 0.0), axis=1, keepdims=True)
        o = _softmax_pv(parts, v_ref[0, 0:r0 + TQ, c0:c0 + HEAD_DIM], row_shift=c_t)
        z = z_ref[0, r0:r0 + TQ, c0:c0 + HEAD_DIM].astype(F32)
        o_ref[0, r0:r0 + TQ, c0:c0 + HEAD_DIM] = (o * _silu(z)).astype(BF16)


def _fox_attention(proj, cum_row, cum_col):
    b, s, _ = proj.shape
    crow = cum_row.reshape(b, FOX_HEADS, 1, s)
    mask = jnp.asarray(_causal_tile())
    return pl.pallas_call(
        functools.partial(_fox_kernel, seq=s),
        grid=(b, FOX_HEADS // HEADS_PER_STEP),
        in_specs=[
            pl.BlockSpec((TQ, TQ), lambda i, h: (0, 0)),
            pl.BlockSpec((1, HEADS_PER_STEP, 1, s), lambda i, h: (i, h, 0, 0)),
            pl.BlockSpec((1, s, HEAD_DIM), lambda i, h: (i, 0, 0)),
        ] + _head_specs(s, FOX_HEADS),
        out_specs=pl.BlockSpec((1, s, HEADS_PER_STEP * HEAD_DIM), lambda i, h: (i, 0, h)),
        out_shape=jax.ShapeDtypeStruct((b, s, D_INNER), BF16),
        compiler_params=pltpu.CompilerParams(
            dimension_semantics=("parallel", "parallel"),
            vmem_limit_bytes=VMEM_LIMIT),
        name="fox_attention",
    )(mask, crow, cum_col, proj, proj, proj, proj)


def _col_scale():
    cs = np.ones((1, 4 * D_INNER), np.float32)
    cs[0, :D_INNER] = QK_SCALE * LOG2E
    return jnp.asarray(cs)


def kernel(x, rel_bias, w_in_0, lam_q1_0, lam_k1_0, lam_q2_0, lam_k2_0, subln_g_0, w_out_0, ln_g_0, ln_b_0, w_in_1, w_out_1, ln_g_1, ln_b_1, w_in_2, b_f_2, w_out_2, ln_g_2, ln_b_2, w_in_3, lam_q1_3, lam_k1_3, lam_q2_3, lam_k2_3, subln_g_3, w_out_3, ln_g_3, ln_b_3):
    b, s, d = x.shape
    m = b * s
    e = D_INNER
    col_scale = _col_scale()
    bias = _bias_tiles(rel_bias)

    h = x.reshape(m, d)
    h_bf = h
    layers = [
        (w_in_0, w_out_0, ln_g_0, ln_b_0),
        (w_in_1, w_out_1, ln_g_1, ln_b_1),
        (w_in_2, w_out_2, ln_g_2, ln_b_2),
        (w_in_3, w_out_3, ln_g_3, ln_b_3),
    ]
    diff_extra = {
        0: (lam_q1_0, lam_k1_0, lam_q2_0, lam_k2_0, subln_g_0),
        3: (lam_q1_3, lam_k1_3, lam_q2_3, lam_k2_3, subln_g_3),
    }
    for i, (w_in, w_out, g, beta) in enumerate(layers):
        kind = i % 3
        if kind == 2:
            w_t = w_in.T
            proj, w_out_bf = _in_proj(h_bf, w_t, col_scale, 4 * e, w_out, w_is_nk=True)
        else:
            proj, w_out_bf = _in_proj(h_bf, w_in, col_scale, 4 * e, w_out)
        proj = proj.reshape(b, s, 4 * e)
        if kind == 0:
            lq1, lk1, lq2, lk2, sg = diff_extra[i]
            lam_init = 0.8 - 0.6 * math.exp(-0.3 * i)
            a = _diff_attention(proj, jnp.stack([lq1, lk1, lq2, lk2]), sg, bias, lam_init)
        elif kind == 1:
            a = _moba_attention(proj, bias)
        else:
            wf = jnp.pad(w_t[4 * e:].astype(BF16), ((0, HEAD_DIM - FOX_HEADS), (0, 0)))
            bf_row = jnp.pad(b_f_2, (0, HEAD_DIM - FOX_HEADS)).reshape(1, HEAD_DIM)
            cum_row, cum_col = _fox_gate(h_bf.reshape(b, s, d), wf, bf_row)
            a = _fox_attention(proj, cum_row, cum_col)
        outs = _out_proj_ln(a.reshape(m, e), w_out_bf, h, g, beta, emit_bf16=i + 1 < DEPTH)
        h, h_bf = outs[0], outs[-1]
    return h.reshape(b, s, d)
```

```python
import functools
import math

import jax
import jax.numpy as jnp
import numpy as np
from jax import lax
from jax.experimental import pallas as pl
from jax.experimental.pallas import tpu as pltpu

D_MODEL = 2048
D_INNER = 2048
DEPTH = 4
DIFF_HEADS = 8
MOBA_HEADS = 16
MOBA_BLOCK = 256
MOBA_TOPK = 3
FOX_HEADS = 16
HEAD_DIM = 128
REL_BUCKETS = 32
REL_MAX_DIST = 128
REL_HEADS = 16
LN_EPS = 1e-5
RMS_EPS = 1e-5
NEG = -1e30
DN_ALPHA = (2.0 * DEPTH) ** 0.25
LOG2E = math.log2(math.e)
QK_SCALE = HEAD_DIM ** -0.5

TQ = 256
DIFF_HEADS_PER_STEP = 1
VMEM_LIMIT = 60 * 1024 * 1024

F32 = jnp.float32
BF16 = jnp.bfloat16


def _dot_nt(a, b):
    return lax.dot_general(a, b, (((1,), (1,)), ((), ())), preferred_element_type=F32)


def _silu(z):
    return z / (1.0 + jnp.exp(-z))


def _inproj_kernel(x_ref, w_ref, cs_ref, wo_ref, o_ref, wo_bf_ref, wbf_ref, *, w_is_nk):
    @pl.when(pl.program_id(1) == 0)
    def _():
        wbf_ref[...] = w_ref[...].astype(BF16)

    x = x_ref[...].astype(BF16)
    if w_is_nk:
        acc = _dot_nt(x, wbf_ref[...])
    else:
        acc = jnp.dot(x, wbf_ref[...], preferred_element_type=F32)
    o_ref[...] = (acc * cs_ref[...]).astype(o_ref.dtype)
    wo_bf_ref[...] = wo_ref[...].astype(BF16)


def _in_proj(x, w, col_scale, n_out, w_out, w_is_nk=False):
    m, k = x.shape
    tm, tn = (2048, 1024) if x.dtype == BF16 else (512, 2048)
    n_tiles, m_tiles = n_out // tn, m // tm
    ko, no = w_out.shape
    slab = ko // (n_tiles * m_tiles)
    if w_is_nk:
        w_block, w_spec = (tn, k), pl.BlockSpec((tn, k), lambda n, i: (n, 0))
    else:
        w_block, w_spec = (k, tn), pl.BlockSpec((k, tn), lambda n, i: (0, n))
    return pl.pallas_call(
        functools.partial(_inproj_kernel, w_is_nk=w_is_nk),
        grid=(n_tiles, m_tiles),
        in_specs=[
            pl.BlockSpec((tm, k), lambda n, i: (i, 0)),
            w_spec,
            pl.BlockSpec((1, tn), lambda n, i: (0, n)),
            pl.BlockSpec((slab, no), lambda n, i: (n * m_tiles + i, 0)),
        ],
        out_specs=[pl.BlockSpec((tm, tn), lambda n, i: (i, n)),
                   pl.BlockSpec((slab, no), lambda n, i: (n * m_tiles + i, 0))],
        out_shape=[jax.ShapeDtypeStruct((m, n_out), BF16),
                   jax.ShapeDtypeStruct((ko, no), BF16)],
        scratch_shapes=[pltpu.VMEM(w_block, BF16)],
        compiler_params=pltpu.CompilerParams(
            dimension_semantics=("parallel", "arbitrary"),
            vmem_limit_bytes=VMEM_LIMIT),
        name="in_proj",
    )(x, w, col_scale, w_out)


LN_CHUNKS = 4


def _data_zero(x):
    rows, n = x.shape
    x = jnp.max(x.reshape(rows // 8, 8, n), axis=0)
    bits = pltpu.bitcast(_fold_lanes([x], jnp.maximum), jnp.uint32)
    zero = lax.shift_right_logical(lax.shift_right_logical(bits, jnp.uint32(16)), jnp.uint32(16))
    return pltpu.bitcast(zero, F32)[0:1, 0:1]


def _outproj_ln_kernel(a_ref, w_ref, h_ref, g_ref, b_ref, o_ref, *rest, nblk):
    obf_ref = rest[0] if len(rest) == 3 else None
    y0_ref, y1_ref = rest[-2:]
    i = pl.program_id(0)
    ys = (y0_ref, y1_ref)
    tm, n = o_ref.shape
    rows, cols = tm // LN_CHUNKS, n // LN_CHUNKS

    def layer_norm(y_ref, c):
        rs = slice(c * rows, (c + 1) * rows)
        r = DN_ALPHA * h_ref[rs, :] + y_ref[rs, :]
        mu = jnp.mean(r, axis=1, keepdims=True)
        d = r - mu
        var = jnp.mean(d * d, axis=1, keepdims=True)
        out = d * lax.rsqrt(var + LN_EPS) * g_ref[...] + b_ref[...]
        o_ref[rs, :] = out
        if obf_ref is not None:
            obf_ref[rs, :] = out.astype(BF16)
        return out

    def matmul(y_ref, c, anchor=None):
        cs = slice(c * cols, (c + 1) * cols)
        if anchor is None:
            y = jnp.dot(a_ref[...], w_ref[:, cs], preferred_element_type=F32)
        else:
            head = a_ref[:, 0:HEAD_DIM] + anchor.astype(BF16)
            y = (jnp.dot(head, w_ref[0:HEAD_DIM, cs], preferred_element_type=F32)
                 + jnp.dot(a_ref[:, HEAD_DIM:], w_ref[HEAD_DIM:, cs], preferred_element_type=F32))
        y_ref[:, cs] = y

    @pl.when(i == 0)
    def _():
        for c in range(LN_CHUNKS):
            matmul(ys[0], c)

    for parity in range(2):
        @pl.when(jnp.logical_and(jnp.logical_and(i > 0, i < nblk), i % 2 == parity))
        def _():
            zero = None
            for c in range(LN_CHUNKS):
                matmul(ys[parity], c, anchor=zero)
                zero = _data_zero(layer_norm(ys[1 - parity], c))

    @pl.when(i == nblk)
    def _():
        for c in range(LN_CHUNKS):
            layer_norm(ys[(nblk - 1) % 2], c)


def _out_proj_ln(a_bf, w_bf, h, g, b, emit_bf16, tm=512):
    m, k = a_bf.shape
    n = w_bf.shape[1]
    nblk = m // tm
    cur = lambda i: (jnp.minimum(i, nblk - 1), 0)
    prev = lambda i: (jnp.maximum(i - 1, 0), 0)
    n_out = 2 if emit_bf16 else 1
    return pl.pallas_call(
        functools.partial(_outproj_ln_kernel, nblk=nblk),
        grid=(nblk + 1,),
        in_specs=[
            pl.BlockSpec((tm, k), cur),
            pl.BlockSpec((k, n), lambda i: (0, 0)),
            pl.BlockSpec((tm, n), prev),
            pl.BlockSpec((1, n), lambda i: (0, 0)),
            pl.BlockSpec((1, n), lambda i: (0, 0)),
        ],
        out_specs=[pl.BlockSpec((tm, n), prev)] * n_out,
        out_shape=[jax.ShapeDtypeStruct((m, n), F32), jax.ShapeDtypeStruct((m, n), BF16)][:n_out],
        scratch_shapes=[pltpu.VMEM((tm, n), F32), pltpu.VMEM((tm, n), F32)],
        compiler_params=pltpu.CompilerParams(
            dimension_semantics=("arbitrary",),
            vmem_limit_bytes=VMEM_LIMIT),
        name="out_proj_ln",
    )(a_bf, w_bf, h, g.reshape(1, n), b.reshape(1, n))


BIAS_SPAN = 4 * TQ


def _bucket_profile():
    n = np.maximum(2 * TQ - np.arange(BIAS_SPAN), 0)
    max_exact = REL_BUCKETS // 2
    nf = np.maximum(n, 1).astype(np.float32)
    large = max_exact + (np.log(nf / max_exact) / math.log(REL_MAX_DIST / max_exact)
                         * (REL_BUCKETS - max_exact)).astype(np.int32)
    large = np.minimum(large, REL_BUCKETS - 1)
    return np.where(n < max_exact, n, large).astype(np.int32).reshape(1, BIAS_SPAN)


def _bias_kernel(tab_ref, bkt_ref, o_ref):
    c = pl.program_id(0)
    far = tab_ref[c, REL_BUCKETS - 1]
    bk = bkt_ref[...]
    acc = jnp.zeros((1, BIAS_SPAN), F32)
    for j in range(REL_BUCKETS):
        acc = jnp.where(bk == j, tab_ref[c, j], acc)
    lane = lax.broadcasted_iota(jnp.int32, (1, BIAS_SPAN), 1)
    prof = jnp.where(lane > 2 * TQ, NEG, (acc - far) * LOG2E)
    rolled = pltpu.roll(jnp.broadcast_to(prof, (TQ, BIAS_SPAN)), 0, 1, stride=1, stride_axis=0)
    o_ref[0, 0] = rolled[:, 2 * TQ:3 * TQ]
    o_ref[0, 1] = rolled[:, TQ:2 * TQ]


def _bias_tiles(rel_bias):
    tab = rel_bias.T
    bkt = jnp.asarray(_bucket_profile())
    return pl.pallas_call(
        _bias_kernel,
        grid=(REL_HEADS,),
        in_specs=[
            pl.BlockSpec(memory_space=pltpu.SMEM),
            pl.BlockSpec((1, BIAS_SPAN), lambda c: (0, 0)),
        ],
        out_specs=pl.BlockSpec((1, 2, TQ, TQ), lambda c: (c, 0, 0, 0)),
        out_shape=jax.ShapeDtypeStruct((REL_HEADS, 2, TQ, TQ), F32),
        name="rel_bias_tiles",
    )(tab, bkt)


def _fold_lanes(parts, op):
    acc = None
    for p in parts:
        for t in range(p.shape[1] // HEAD_DIM):
            blk = p[:, t * HEAD_DIM:(t + 1) * HEAD_DIM]
            acc = blk if acc is None else op(acc, blk)
    return acc


def _softmax_pv(parts, v, row_shift=None, row_scale=None):
    m = _fold_lanes(parts, jnp.maximum).max(axis=1, keepdims=True)
    if row_shift is None:
        shift = -m
    else:
        m_full = m + row_shift
        shift = row_shift - m_full
    ps = [jnp.exp2(p + shift) for p in parts]
    pb = jnp.concatenate([p.astype(BF16) for p in ps], axis=1) if len(ps) > 1 else ps[0].astype(BF16)
    if v.shape[1] == HEAD_DIM:
        v1 = jnp.concatenate([v, jnp.ones(v.shape, v.dtype)], axis=1)
        acc = jnp.dot(pb, v1, preferred_element_type=F32)
        return acc[:, 0:HEAD_DIM] / acc[:, HEAD_DIM:HEAD_DIM + 1]
    l = _fold_lanes(ps, jnp.add).sum(axis=1, keepdims=True)
    acc = jnp.dot(pb, v, preferred_element_type=F32)
    if row_scale is not None:
        return acc * (row_scale / l)
    return acc / l


def _diff_kernel(lamv_ref, g_ref, q_ref, k_ref, v_ref, z_ref, bias_ref, o_ref, *, lam_init, seq):
    lv = lamv_ref[...]
    s1 = jnp.sum(lv[0:1] * lv[1:2], axis=1, keepdims=True)
    s2 = jnp.sum(lv[2:3] * lv[3:4], axis=1, keepdims=True)
    lam = jnp.exp(s1) - jnp.exp(s2) + lam_init
    gain = g_ref[...] * (1.0 - lam_init)

    w = 2 * HEAD_DIM

    def logits(hd, qi, j):
        r0 = qi * TQ
        c0 = hd * w + j * HEAD_DIM
        q = q_ref[0, r0:r0 + TQ, c0:c0 + HEAD_DIM]
        parts = []
        if qi >= 2:
            parts.append(_dot_nt(q, k_ref[0, 0:r0 - TQ, c0:c0 + HEAD_DIM]))
        if qi >= 1:
            parts.append(_dot_nt(q, k_ref[0, r0 - TQ:r0, c0:c0 + HEAD_DIM]) + bias_ref[2 * hd + j, 1])
        parts.append(_dot_nt(q, k_ref[0, r0:r0 + TQ, c0:c0 + HEAD_DIM]) + bias_ref[2 * hd + j, 0])
        return parts

    tasks = [(hd, qi, j) for hd in range(DIFF_HEADS_PER_STEP) for qi in reversed(range(seq // TQ)) for j in range(2)]
    nxt = logits(*tasks[0])
    outs = []
    for t, (hd, qi, j) in enumerate(tasks):
        r0 = qi * TQ
        parts = nxt
        if t + 1 < len(tasks):
            nxt = logits(*tasks[t + 1])
        outs.append(_softmax_pv(parts, v_ref[0, 0:r0 + TQ, hd * w:(hd + 1) * w], row_scale=lam if j == 1 else None))
        if j == 0:
            continue
        o = outs[-2] - outs[-1]
        ms = jnp.mean(o * o, axis=1, keepdims=True)
        o = o * lax.rsqrt(ms + RMS_EPS) * gain
        z = z_ref[0, r0:r0 + TQ, hd * w:(hd + 1) * w].astype(F32)
        o_ref[0, r0:r0 + TQ, hd * w:(hd + 1) * w] = (o * _silu(z)).astype(BF16)


def _diff_attention(proj, lamv, subln_g, bias, lam_init):
    b, s, _ = proj.shape
    w = 2 * HEAD_DIM
    wb = DIFF_HEADS_PER_STEP * w
    nh = D_INNER // wb
    return pl.pallas_call(
        functools.partial(_diff_kernel, lam_init=lam_init, seq=s),
        grid=(b, nh),
        in_specs=[
            pl.BlockSpec((4, HEAD_DIM), lambda i, h: (0, 0)),
            pl.BlockSpec((1, w), lambda i, h: (0, 0)),
            pl.BlockSpec((1, s, wb), lambda i, h: (i, 0, h)),
            pl.BlockSpec((1, s, wb), lambda i, h: (i, 0, nh + h)),
            pl.BlockSpec((1, s, wb), lambda i, h: (i, 0, 2 * nh + h)),
            pl.BlockSpec((1, s, wb), lambda i, h: (i, 0, 3 * nh + h)),
            pl.BlockSpec((2 * DIFF_HEADS_PER_STEP, 2, TQ, TQ), lambda i, h: (h, 0, 0, 0)),
        ],
        out_specs=pl.BlockSpec((1, s, wb), lambda i, h: (i, 0, h)),
        out_shape=jax.ShapeDtypeStruct((b, s, D_INNER), BF16),
        compiler_params=pltpu.CompilerParams(
            dimension_semantics=("parallel", "parallel"),
            vmem_limit_bytes=VMEM_LIMIT),
        name="diff_attention",
    )(lamv, subln_g.reshape(1, w), proj, proj, proj, proj, bias)


def _block_indicator(seq):
    n_kb = seq // MOBA_BLOCK
    ind = np.zeros((HEAD_DIM, seq), np.float32)
    for j in range(n_kb):
        ind[j, j * MOBA_BLOCK:(j + 1) * MOBA_BLOCK] = 1.0
    return ind


HEADS_PER_STEP = 2


def _moba_kernel(ind_ref, kind_ref, q_ref, k_ref, v_ref, z_ref, bias_ref, o_ref, *, seq):
    n_kb = seq // MOBA_BLOCK
    sub = lax.broadcasted_iota(jnp.int32, (n_kb, TQ), 0)

    def block_means(hd):
        c0 = hd * HEAD_DIM
        ksum = jnp.dot(ind_ref[...], k_ref[0, :, c0:c0 + HEAD_DIM], preferred_element_type=F32)
        kmean = ksum * (1.0 / MOBA_BLOCK)
        hi = kmean.astype(BF16)
        lo = (kmean - hi.astype(F32)).astype(BF16)
        return jnp.concatenate([hi, lo], axis=1)

    km2 = [block_means(hd) for hd in range(HEADS_PER_STEP)]

    def block_mask(hd, qi, q):
        gate = _dot_nt(km2[hd], jnp.concatenate([q, q], axis=1))[0:n_kb]
        rank = jnp.zeros((n_kb, TQ), F32)
        for bp in range(qi):
            gb = gate[bp:bp + 1, :]
            rank = rank + jnp.where(sub > bp, jnp.where(gb >= gate, 1.0, 0.0), jnp.where(gb > gate, 1.0, 0.0))
        mt = jnp.where(sub >= qi, 0.0, jnp.where(rank < float(MOBA_TOPK), 0.0, NEG))
        mt = jnp.concatenate([mt, jnp.zeros((HEAD_DIM - n_kb, TQ), F32)], axis=0)
        return mt.T.astype(BF16)

    def logits(hd, qi):
        r0 = qi * TQ
        c0 = hd * HEAD_DIM
        q = q_ref[0, r0:r0 + TQ, c0:c0 + HEAD_DIM]
        masked = qi > MOBA_TOPK
        if masked:
            q = jnp.concatenate([q, block_mask(hd, qi, q)], axis=1)
        def keys(lo, hi):
            kk = k_ref[0, lo:hi, c0:c0 + HEAD_DIM]
            return jnp.concatenate([kk, kind_ref[lo:hi, :]], axis=1) if masked else kk

        parts = []
        if qi >= 2:
            parts.append(_dot_nt(q, keys(0, r0 - TQ)))
        if qi >= 1:
            parts.append(_dot_nt(q, keys(r0 - TQ, r0)) + bias_ref[hd, 1])
        parts.append(_dot_nt(q, keys(r0, r0 + TQ)) + bias_ref[hd, 0])
        return parts

    order = [MOBA_TOPK] + [qi for qi in reversed(range(n_kb)) if qi != MOBA_TOPK]
    tasks = [(hd, qi) for hd in range(HEADS_PER_STEP) for qi in order]
    nxt = logits(*tasks[0])
    for t, (hd, qi) in enumerate(tasks):
        r0 = qi * TQ
        c0 = hd * HEAD_DIM
        parts = nxt
        if t + 1 < len(tasks):
            nxt = logits(*tasks[t + 1])
        o = _softmax_pv(parts, v_ref[0, 0:r0 + TQ, c0:c0 + HEAD_DIM])
        z = z_ref[0, r0:r0 + TQ, c0:c0 + HEAD_DIM].astype(F32)
        o_ref[0, r0:r0 + TQ, c0:c0 + HEAD_DIM] = (o * _silu(z)).astype(BF16)


def _head_specs(s, nh):
    w = HEADS_PER_STEP * HEAD_DIM
    ng = nh // HEADS_PER_STEP
    return [
        pl.BlockSpec((1, s, w), lambda i, h: (i, 0, h)),
        pl.BlockSpec((1, s, w), lambda i, h: (i, 0, ng + h)),
        pl.BlockSpec((1, s, w), lambda i, h: (i, 0, 2 * ng + h)),
        pl.BlockSpec((1, s, w), lambda i, h: (i, 0, 3 * ng + h)),
    ]


def _moba_attention(proj, bias):
    b, s, _ = proj.shape
    assert s // MOBA_BLOCK == 8 and MOBA_BLOCK == TQ
    ind = _block_indicator(s)
    return pl.pallas_call(
        functools.partial(_moba_kernel, seq=s),
        grid=(b, MOBA_HEADS // HEADS_PER_STEP),
        in_specs=[pl.BlockSpec((HEAD_DIM, s), lambda i, h: (0, 0)),
                  pl.BlockSpec((s, HEAD_DIM), lambda i, h: (0, 0))]
        + _head_specs(s, MOBA_HEADS)
        + [pl.BlockSpec((HEADS_PER_STEP, 2, TQ, TQ), lambda i, h: (h, 0, 0, 0))],
        out_specs=pl.BlockSpec((1, s, HEADS_PER_STEP * HEAD_DIM), lambda i, h: (i, 0, h)),
        out_shape=jax.ShapeDtypeStruct((b, s, D_INNER), BF16),
        compiler_params=pltpu.CompilerParams(
            dimension_semantics=("parallel", "parallel"),
            vmem_limit_bytes=VMEM_LIMIT),
        name="moba_attention",
    )(jnp.asarray(ind, BF16), jnp.asarray(ind.T, BF16), proj, proj, proj, proj, bias)


def _fox_gate_kernel(x_ref, wf_ref, bf_ref, row_ref, col_ref, *, seq):
    f = _dot_nt(x_ref[0], wf_ref[...]) + bf_ref[...]
    c = jnp.minimum(f, 0.0) - jnp.log1p(jnp.exp(-jnp.abs(f)))
    c = c.T[0:FOX_HEADS]
    lane = lax.broadcasted_iota(jnp.int32, c.shape, 1)
    d = 1
    while d < seq:
        c = c + jnp.where(lane >= d, pltpu.roll(c, d, 1), 0.0)
        d *= 2
    c = c * LOG2E
    row_ref[0] = c
    col_ref[0] = jnp.concatenate([c, jnp.zeros((HEAD_DIM - FOX_HEADS, seq), F32)], axis=0).T


def _fox_gate(h_bf3, wf, bf_row):
    b, s, d = h_bf3.shape
    return pl.pallas_call(
        functools.partial(_fox_gate_kernel, seq=s),
        grid=(b,),
        in_specs=[
            pl.BlockSpec((1, s, d), lambda i: (i, 0, 0)),
            pl.BlockSpec((HEAD_DIM, d), lambda i: (0, 0)),
            pl.BlockSpec((1, HEAD_DIM), lambda i: (0, 0)),
        ],
        out_specs=[pl.BlockSpec((1, FOX_HEADS, s), lambda i: (i, 0, 0)),
                   pl.BlockSpec((1, s, HEAD_DIM), lambda i: (i, 0, 0))],
        out_shape=[jax.ShapeDtypeStruct((b, FOX_HEADS, s), F32),
                   jax.ShapeDtypeStruct((b, s, HEAD_DIM), F32)],
        compiler_params=pltpu.CompilerParams(
            dimension_semantics=("arbitrary",),
            vmem_limit_bytes=VMEM_LIMIT),
        name="fox_gate",
    )(h_bf3, wf, bf_row)


def _causal_tile():
    r = np.arange(TQ)[:, None]
    c = np.arange(TQ)[None, :]
    return np.where(c <= r, 0.0, NEG).astype(np.float32)


def _fox_kernel(mask_ref, crow_ref, ccol_ref, q_ref, k_ref, v_ref, z_ref, o_ref, *, seq):
    def logits(hd, qi):
        r0 = qi * TQ
        c0 = hd * HEAD_DIM
        q = q_ref[0, r0:r0 + TQ, c0:c0 + HEAD_DIM]
        parts = []
        if qi >= 1:
            parts.append(_dot_nt(q, k_ref[0, 0:r0, c0:c0 + HEAD_DIM]) - crow_ref[0, hd, :, 0:r0])
        parts.append(_dot_nt(q, k_ref[0, r0:r0 + TQ, c0:c0 + HEAD_DIM]) - crow_ref[0, hd, :, r0:r0 + TQ]
                     + mask_ref[...])
        return parts

    nq = seq // TQ
    lane = lax.broadcasted_iota(jnp.int32, (TQ, HEAD_DIM), 1)
    tasks = [(hd, qi) for hd in range(HEADS_PER_STEP) for qi in reversed(range(nq))]
    nxt = logits(*tasks[0])
    for t, (hd, qi) in enumerate(tasks):
        r0 = qi * TQ
        c0 = hd * HEAD_DIM
        parts = nxt
        if t + 1 < len(tasks):
            nxt = logits(*tasks[t + 1])
        head_lane = lane == pl.program_id(1) * HEADS_PER_STEP + hd
        c_t = jnp.sum(jnp.where(head_lane, ccol_ref[0, r0:r0 + TQ, :], 0.0), axis=1, keepdims=True)
        o = _softmax_pv(parts, v_ref[0, 0:r0 + TQ, c0:c0 + HEAD_DIM], row_shift=c_t)
        z = z_ref[0, r0:r0 + TQ, c0:c0 + HEAD_DIM].astype(F32)
        o_ref[0, r0:r0 + TQ, c0:c0 + HEAD_DIM] = (o * _silu(z)).astype(BF16)


def _fox_attention(proj, cum_row, cum_col):
    b, s, _ = proj.shape
    crow = cum_row.reshape(b, FOX_HEADS, 1, s)
    mask = jnp.asarray(_causal_tile())
    return pl.pallas_call(
        functools.partial(_fox_kernel, seq=s),
        grid=(b, FOX_HEADS // HEADS_PER_STEP),
        in_specs=[
            pl.BlockSpec((TQ, TQ), lambda i, h: (0, 0)),
            pl.BlockSpec((1, HEADS_PER_STEP, 1, s), lambda i, h: (i, h, 0, 0)),
            pl.BlockSpec((1, s, HEAD_DIM), lambda i, h: (i, 0, 0)),
        ] + _head_specs(s, FOX_HEADS),
        out_specs=pl.BlockSpec((1, s, HEADS_PER_STEP * HEAD_DIM), lambda i, h: (i, 0, h)),
        out_shape=jax.ShapeDtypeStruct((b, s, D_INNER), BF16),
        compiler_params=pltpu.CompilerParams(
            dimension_semantics=("parallel", "parallel"),
            vmem_limit_bytes=VMEM_LIMIT),
        name="fox_attention",
    )(mask, crow, cum_col, proj, proj, proj, proj)


def _col_scale():
    cs = np.ones((1, 4 * D_INNER), np.float32)
    cs[0, :D_INNER] = QK_SCALE * LOG2E
    return jnp.asarray(cs)


def kernel(x, rel_bias, w_in_0, lam_q1_0, lam_k1_0, lam_q2_0, lam_k2_0, subln_g_0, w_out_0, ln_g_0, ln_b_0, w_in_1, w_out_1, ln_g_1, ln_b_1, w_in_2, b_f_2, w_out_2, ln_g_2, ln_b_2, w_in_3, lam_q1_3, lam_k1_3, lam_q2_3, lam_k2_3, subln_g_3, w_out_3, ln_g_3, ln_b_3):
    b, s, d = x.shape
    m = b * s
    e = D_INNER
    col_scale = _col_scale()
    bias = _bias_tiles(rel_bias)

    h = x.reshape(m, d)
    h_bf = h
    layers = [
        (w_in_0, w_out_0, ln_g_0, ln_b_0),
        (w_in_1, w_out_1, ln_g_1, ln_b_1),
        (w_in_2, w_out_2, ln_g_2, ln_b_2),
        (w_in_3, w_out_3, ln_g_3, ln_b_3),
    ]
    diff_extra = {
        0: (lam_q1_0, lam_k1_0, lam_q2_0, lam_k2_0, subln_g_0),
        3: (lam_q1_3, lam_k1_3, lam_q2_3, lam_k2_3, subln_g_3),
    }
    for i, (w_in, w_out, g, beta) in enumerate(layers):
        kind = i % 3
        if kind == 2:
            w_t = w_in.T
            proj, w_out_bf = _in_proj(h_bf, w_t, col_scale, 4 * e, w_out, w_is_nk=True)
        else:
            proj, w_out_bf = _in_proj(h_bf, w_in, col_scale, 4 * e, w_out)
        proj = proj.reshape(b, s, 4 * e)
        if kind == 0:
            lq1, lk1, lq2, lk2, sg = diff_extra[i]
            lam_init = 0.8 - 0.6 * math.exp(-0.3 * i)
            a = _diff_attention(proj, jnp.stack([lq1, lk1, lq2, lk2]), sg, bias, lam_init)
        elif kind == 1:
            a = _moba_attention(proj, bias)
        else:
            wf = jnp.pad(w_t[4 * e:].astype(BF16), ((0, HEAD_DIM - FOX_HEADS), (0, 0)))
            bf_row = jnp.pad(b_f_2, (0, HEAD_DIM - FOX_HEADS)).reshape(1, HEAD_DIM)
            cum_row, cum_col = _fox_gate(h_bf.reshape(b, s, d), wf, bf_row)
            a = _fox_attention(proj, cum_row, cum_col)
        outs = _out_proj_ln(a.reshape(m, e), w_out_bf, h, g, beta, emit_bf16=i + 1 < DEPTH)
        h, h_bf = outs[0], outs[-1]
    return h.reshape(b, s, d)
```

```python
import functools
import math

import jax
import jax.numpy as jnp
import numpy as np
from jax import lax
from jax.experimental import pallas as pl
from jax.experimental.pallas import tpu as pltpu

D_MODEL = 2048
D_INNER = 2048
DEPTH = 4
DIFF_HEADS = 8
MOBA_HEADS = 16
MOBA_BLOCK = 256
MOBA_TOPK = 3
FOX_HEADS = 16
HEAD_DIM = 128
REL_BUCKETS = 32
REL_MAX_DIST = 128
REL_HEADS = 16
LN_EPS = 1e-5
RMS_EPS = 1e-5
NEG = -1e30
DN_ALPHA = (2.0 * DEPTH) ** 0.25
LOG2E = math.log2(math.e)
QK_SCALE = HEAD_DIM ** -0.5

TQ = 256
DIFF_BLOCKS_AHEAD = 2
VMEM_LIMIT = 60 * 1024 * 1024

F32 = jnp.float32
BF16 = jnp.bfloat16


def _dot_nt(a, b):
    return lax.dot_general(a, b, (((1,), (1,)), ((), ())), preferred_element_type=F32)


def _silu(z):
    return z / (1.0 + jnp.exp(-z))


def _inproj_kernel(x_ref, w_ref, cs_ref, wo_ref, o_ref, wo_bf_ref, wbf_ref, *, w_is_nk):
    @pl.when(pl.program_id(1) == 0)
    def _():
        wbf_ref[...] = w_ref[...].astype(BF16)

    x = x_ref[...].astype(BF16)
    if w_is_nk:
        acc = _dot_nt(x, wbf_ref[...])
    else:
        acc = jnp.dot(x, wbf_ref[...], preferred_element_type=F32)
    o_ref[...] = (acc * cs_ref[...]).astype(o_ref.dtype)
    wo_bf_ref[...] = wo_ref[...].astype(BF16)


def _in_proj(x, w, col_scale, n_out, w_out, w_is_nk=False):
    m, k = x.shape
    tm, tn = (2048, 1024) if x.dtype == BF16 else (512, 2048)
    n_tiles, m_tiles = n_out // tn, m // tm
    ko, no = w_out.shape
    slab = ko // (n_tiles * m_tiles)
    if w_is_nk:
        w_block, w_spec = (tn, k), pl.BlockSpec((tn, k), lambda n, i: (n, 0))
    else:
        w_block, w_spec = (k, tn), pl.BlockSpec((k, tn), lambda n, i: (0, n))
    return pl.pallas_call(
        functools.partial(_inproj_kernel, w_is_nk=w_is_nk),
        grid=(n_tiles, m_tiles),
        in_specs=[
            pl.BlockSpec((tm, k), lambda n, i: (i, 0)),
            w_spec,
            pl.BlockSpec((1, tn), lambda n, i: (0, n)),
            pl.BlockSpec((slab, no), lambda n, i: (n * m_tiles + i, 0)),
        ],
        out_specs=[pl.BlockSpec((tm, tn), lambda n, i: (i, n)),
                   pl.BlockSpec((slab, no), lambda n, i: (n * m_tiles + i, 0))],
        out_shape=[jax.ShapeDtypeStruct((m, n_out), BF16),
                   jax.ShapeDtypeStruct((ko, no), BF16)],
        scratch_shapes=[pltpu.VMEM(w_block, BF16)],
        compiler_params=pltpu.CompilerParams(
            dimension_semantics=("arbitrary", "arbitrary"),
            vmem_limit_bytes=VMEM_LIMIT),
        name="in_proj",
    )(x, w, col_scale, w_out)


LN_CHUNKS = 4


def _data_zero(x):
    rows, n = x.shape
    x = jnp.max(x.reshape(rows // 8, 8, n), axis=0)
    bits = pltpu.bitcast(_fold_lanes([x], jnp.maximum), jnp.uint32)
    zero = lax.shift_right_logical(lax.shift_right_logical(bits, jnp.uint32(16)), jnp.uint32(16))
    return pltpu.bitcast(zero, F32)[0:1, 0:1]


def _outproj_ln_kernel(a_ref, w_ref, h_ref, g_ref, b_ref, o_ref, *rest, nblk):
    obf_ref = rest[0] if len(rest) == 3 else None
    y0_ref, y1_ref = rest[-2:]
    i = pl.program_id(0)
    ys = (y0_ref, y1_ref)
    tm, n = o_ref.shape
    rows, cols = tm // LN_CHUNKS, n // LN_CHUNKS

    def layer_norm(y_ref, c):
        rs = slice(c * rows, (c + 1) * rows)
        r = DN_ALPHA * h_ref[rs, :] + y_ref[rs, :]
        mu = jnp.mean(r, axis=1, keepdims=True)
        d = r - mu
        var = jnp.mean(d * d, axis=1, keepdims=True)
        out = d * lax.rsqrt(var + LN_EPS) * g_ref[...] + b_ref[...]
        o_ref[rs, :] = out
        if obf_ref is not None:
            obf_ref[rs, :] = out.astype(BF16)
        return out

    def matmul(y_ref, c, anchor=None):
        cs = slice(c * cols, (c + 1) * cols)
        if anchor is None:
            y = jnp.dot(a_ref[...], w_ref[:, cs], preferred_element_type=F32)
        else:
            head = a_ref[:, 0:HEAD_DIM] + anchor.astype(BF16)
            y = (jnp.dot(head, w_ref[0:HEAD_DIM, cs], preferred_element_type=F32)
                 + jnp.dot(a_ref[:, HEAD_DIM:], w_ref[HEAD_DIM:, cs], preferred_element_type=F32))
        y_ref[:, cs] = y

    @pl.when(i == 0)
    def _():
        for c in range(LN_CHUNKS):
            matmul(ys[0], c)

    for parity in range(2):
        @pl.when(jnp.logical_and(jnp.logical_and(i > 0, i < nblk), i % 2 == parity))
        def _():
            zero = None
            for c in range(LN_CHUNKS):
                matmul(ys[parity], c, anchor=zero)
                zero = _data_zero(layer_norm(ys[1 - parity], c))

    @pl.when(i == nblk)
    def _():
        for c in range(LN_CHUNKS):
            layer_norm(ys[(nblk - 1) % 2], c)


def _out_proj_ln(a_bf, w_bf, h, g, b, emit_bf16, tm=512):
    m, k = a_bf.shape
    n = w_bf.shape[1]
    nblk = m // tm
    cur = lambda i: (jnp.minimum(i, nblk - 1), 0)
    prev = lambda i: (jnp.maximum(i - 1, 0), 0)
    n_out = 2 if emit_bf16 else 1
    return pl.pallas_call(
        functools.partial(_outproj_ln_kernel, nblk=nblk),
        grid=(nblk + 1,),
        in_specs=[
            pl.BlockSpec((tm, k), cur),
            pl.BlockSpec((k, n), lambda i: (0, 0)),
            pl.BlockSpec((tm, n), prev),
            pl.BlockSpec((1, n), lambda i: (0, 0)),
            pl.BlockSpec((1, n), lambda i: (0, 0)),
        ],
        out_specs=[pl.BlockSpec((tm, n), prev)] * n_out,
        out_shape=[jax.ShapeDtypeStruct((m, n), F32), jax.ShapeDtypeStruct((m, n), BF16)][:n_out],
        scratch_shapes=[pltpu.VMEM((tm, n), F32), pltpu.VMEM((tm, n), F32)],
        compiler_params=pltpu.CompilerParams(
            dimension_semantics=("arbitrary",),
            vmem_limit_bytes=VMEM_LIMIT),
        name="out_proj_ln",
    )(a_bf, w_bf, h, g.reshape(1, n), b.reshape(1, n))


BIAS_SPAN = 4 * TQ


def _bucket_profile():
    n = np.maximum(2 * TQ - np.arange(BIAS_SPAN), 0)
    max_exact = REL_BUCKETS // 2
    nf = np.maximum(n, 1).astype(np.float32)
    large = max_exact + (np.log(nf / max_exact) / math.log(REL_MAX_DIST / max_exact)
                         * (REL_BUCKETS - max_exact)).astype(np.int32)
    large = np.minimum(large, REL_BUCKETS - 1)
    return np.where(n < max_exact, n, large).astype(np.int32).reshape(1, BIAS_SPAN)


def _bias_kernel(tab_ref, bkt_ref, o_ref):
    c = pl.program_id(0)
    far = tab_ref[c, REL_BUCKETS - 1]
    bk = bkt_ref[...]
    acc = jnp.zeros((1, BIAS_SPAN), F32)
    for j in range(REL_BUCKETS):
        acc = jnp.where(bk == j, tab_ref[c, j], acc)
    lane = lax.broadcasted_iota(jnp.int32, (1, BIAS_SPAN), 1)
    prof = jnp.where(lane > 2 * TQ, NEG, (acc - far) * LOG2E)
    rolled = pltpu.roll(jnp.broadcast_to(prof, (TQ, BIAS_SPAN)), 0, 1, stride=1, stride_axis=0)
    o_ref[0, 0] = rolled[:, 2 * TQ:3 * TQ]
    o_ref[0, 1] = rolled[:, TQ:2 * TQ]


def _bias_tiles(rel_bias):
    tab = rel_bias.T
    bkt = jnp.asarray(_bucket_profile())
    return pl.pallas_call(
        _bias_kernel,
        grid=(REL_HEADS,),
        in_specs=[
            pl.BlockSpec(memory_space=pltpu.SMEM),
            pl.BlockSpec((1, BIAS_SPAN), lambda c: (0, 0)),
        ],
        out_specs=pl.BlockSpec((1, 2, TQ, TQ), lambda c: (c, 0, 0, 0)),
        out_shape=jax.ShapeDtypeStruct((REL_HEADS, 2, TQ, TQ), F32),
        name="rel_bias_tiles",
    )(tab, bkt)


def _fold_lanes(parts, op):
    acc = None
    for p in parts:
        for t in range(p.shape[1] // HEAD_DIM):
            blk = p[:, t * HEAD_DIM:(t + 1) * HEAD_DIM]
            acc = blk if acc is None else op(acc, blk)
    return acc


def _softmax_pv(parts, v, row_shift=None):
    m = _fold_lanes(parts, jnp.maximum).max(axis=1, keepdims=True)
    if row_shift is None:
        shift = -m
    else:
        m_full = m + row_shift
        shift = row_shift - m_full
    ps = [jnp.exp2(p + shift) for p in parts]
    pb = jnp.concatenate([p.astype(BF16) for p in ps], axis=1) if len(ps) > 1 else ps[0].astype(BF16)
    if v.shape[1] == HEAD_DIM:
        v1 = jnp.concatenate([v, jnp.ones(v.shape, v.dtype)], axis=1)
        acc = jnp.dot(pb, v1, preferred_element_type=F32)
        return acc[:, 0:HEAD_DIM] / acc[:, HEAD_DIM:HEAD_DIM + 1]
    l = _fold_lanes(ps, jnp.add).sum(axis=1, keepdims=True)
    acc = jnp.dot(pb, v, preferred_element_type=F32)
    return acc / l


def _diff_kernel(lamv_ref, g_ref, q_ref, k_ref, v_ref, z_ref, bias_ref, o_ref, *, lam_init, seq):
    lv = lamv_ref[...]
    s1 = jnp.sum(lv[0:1] * lv[1:2], axis=1, keepdims=True)
    s2 = jnp.sum(lv[2:3] * lv[3:4], axis=1, keepdims=True)
    lam = jnp.exp(s1) - jnp.exp(s2) + lam_init

    def logits(qi, j):
        r0 = qi * TQ
        c0 = j * HEAD_DIM
        q = q_ref[0, r0:r0 + TQ, c0:c0 + HEAD_DIM]
        parts = []
        if qi >= 2:
            parts.append(_dot_nt(q, k_ref[0, 0:r0 - TQ, c0:c0 + HEAD_DIM]))
        if qi >= 1:
            parts.append(_dot_nt(q, k_ref[0, r0 - TQ:r0, c0:c0 + HEAD_DIM]) + bias_ref[j, 1])
        parts.append(_dot_nt(q, k_ref[0, r0:r0 + TQ, c0:c0 + HEAD_DIM]) + bias_ref[j, 0])
        return parts

    def probs(parts):
        m = _fold_lanes(parts, jnp.maximum).max(axis=1, keepdims=True)
        ps = [jnp.exp2(p - m) for p in parts]
        l = _fold_lanes(ps, jnp.add).sum(axis=1, keepdims=True)
        return jnp.concatenate([p.astype(BF16) for p in ps], axis=1), l

    order = list(reversed(range(seq // TQ)))
    queue = [(logits(qi, 0), logits(qi, 1)) for qi in order[:DIFF_BLOCKS_AHEAD]]
    for t, qi in enumerate(order):
        r0 = qi * TQ
        parts0, parts1 = queue.pop(0)
        if t + DIFF_BLOCKS_AHEAD < len(order):
            nq = order[t + DIFF_BLOCKS_AHEAD]
            queue.append((logits(nq, 0), logits(nq, 1)))
        pb0, l0 = probs(parts0)
        pb1, l1 = probs(parts1)
        acc = jnp.dot(jnp.concatenate([pb0, pb1], axis=0), v_ref[0, 0:r0 + TQ, :], preferred_element_type=F32)
        o = acc[0:TQ] / l0 - lam * (acc[TQ:2 * TQ] / l1)
        ms = jnp.mean(o * o, axis=1, keepdims=True)
        o = o * lax.rsqrt(ms + RMS_EPS) * g_ref[...] * (1.0 - lam_init)
        z = z_ref[0, r0:r0 + TQ, :].astype(F32)
        o_ref[0, r0:r0 + TQ, :] = (o * _silu(z)).astype(BF16)


def _diff_attention(proj, lamv, subln_g, bias, lam_init):
    b, s, _ = proj.shape
    w = 2 * HEAD_DIM
    nh = D_INNER // w
    return pl.pallas_call(
        functools.partial(_diff_kernel, lam_init=lam_init, seq=s),
        grid=(b, DIFF_HEADS),
        in_specs=[
            pl.BlockSpec((4, HEAD_DIM), lambda i, h: (0, 0)),
            pl.BlockSpec((1, w), lambda i, h: (0, 0)),
            pl.BlockSpec((1, s, w), lambda i, h: (i, 0, h)),
            pl.BlockSpec((1, s, w), lambda i, h: (i, 0, nh + h)),
            pl.BlockSpec((1, s, w), lambda i, h: (i, 0, 2 * nh + h)),
            pl.BlockSpec((1, s, w), lambda i, h: (i, 0, 3 * nh + h)),
            pl.BlockSpec((2, 2, TQ, TQ), lambda i, h: (h, 0, 0, 0)),
        ],
        out_specs=pl.BlockSpec((1, s, w), lambda i, h: (i, 0, h)),
        out_shape=jax.ShapeDtypeStruct((b, s, D_INNER), BF16),
        compiler_params=pltpu.CompilerParams(
            dimension_semantics=("arbitrary", "arbitrary"),
            vmem_limit_bytes=VMEM_LIMIT),
        name="diff_attention",
    )(lamv, subln_g.reshape(1, w), proj, proj, proj, proj, bias)


def _block_indicator(seq):
    n_kb = seq // MOBA_BLOCK
    ind = np.zeros((HEAD_DIM, seq), np.float32)
    for j in range(n_kb):
        ind[j, j * MOBA_BLOCK:(j + 1) * MOBA_BLOCK] = 1.0
    return ind


HEADS_PER_STEP = 2


def _moba_kernel(ind_ref, kind_ref, q_ref, k_ref, v_ref, z_ref, bias_ref, o_ref, *, seq):
    n_kb = seq // MOBA_BLOCK
    sub = lax.broadcasted_iota(jnp.int32, (n_kb, TQ), 0)

    def block_means(hd):
        c0 = hd * HEAD_DIM
        ksum = jnp.dot(ind_ref[...], k_ref[0, :, c0:c0 + HEAD_DIM], preferred_element_type=F32)
        kmean = ksum * (1.0 / MOBA_BLOCK)
        hi = kmean.astype(BF16)
        lo = (kmean - hi.astype(F32)).astype(BF16)
        return jnp.concatenate([hi, lo], axis=1)

    km2 = [block_means(hd) for hd in range(HEADS_PER_STEP)]

    def block_mask(hd, qi, q):
        gate = _dot_nt(km2[hd], jnp.concatenate([q, q], axis=1))[0:n_kb]
        rank = jnp.zeros((n_kb, TQ), F32)
        for bp in range(qi):
            gb = gate[bp:bp + 1, :]
            rank = rank + jnp.where(sub > bp, jnp.where(gb >= gate, 1.0, 0.0), jnp.where(gb > gate, 1.0, 0.0))
        mt = jnp.where(sub >= qi, 0.0, jnp.where(rank < float(MOBA_TOPK), 0.0, NEG))
        mt = jnp.concatenate([mt, jnp.zeros((HEAD_DIM - n_kb, TQ), F32)], axis=0)
        return mt.T.astype(BF16)

    def logits(hd, qi):
        r0 = qi * TQ
        c0 = hd * HEAD_DIM
        q = q_ref[0, r0:r0 + TQ, c0:c0 + HEAD_DIM]
        masked = qi > MOBA_TOPK
        if masked:
            q = jnp.concatenate([q, block_mask(hd, qi, q)], axis=1)
        parts = []
        for j in range(qi + 1):
            kj = k_ref[0, j * TQ:(j + 1) * TQ, c0:c0 + HEAD_DIM]
            if masked:
                kj = jnp.concatenate([kj, kind_ref[j * TQ:(j + 1) * TQ, :]], axis=1)
            sj = _dot_nt(q, kj)
            if j >= qi - 1:
                sj = sj + bias_ref[hd, qi - j]
            parts.append(sj)
        return parts

    order = [MOBA_TOPK] + [qi for qi in reversed(range(n_kb)) if qi != MOBA_TOPK]
    tasks = [(hd, qi) for hd in range(HEADS_PER_STEP) for qi in order]
    nxt = logits(*tasks[0])
    for t, (hd, qi) in enumerate(tasks):
        r0 = qi * TQ
        c0 = hd * HEAD_DIM
        parts = nxt
        if t + 1 < len(tasks):
            nxt = logits(*tasks[t + 1])
        o = _softmax_pv(parts, v_ref[0, 0:r0 + TQ, c0:c0 + HEAD_DIM])
        z = z_ref[0, r0:r0 + TQ, c0:c0 + HEAD_DIM].astype(F32)
        o_ref[0, r0:r0 + TQ, c0:c0 + HEAD_DIM] = (o * _silu(z)).astype(BF16)


def _head_specs(s, nh):
    w = HEADS_PER_STEP * HEAD_DIM
    ng = nh // HEADS_PER_STEP
    return [
        pl.BlockSpec((1, s, w), lambda i, h: (i, 0, h)),
        pl.BlockSpec((1, s, w), lambda i, h: (i, 0, ng + h)),
        pl.BlockSpec((1, s, w), lambda i, h: (i, 0, 2 * ng + h)),
        pl.BlockSpec((1, s, w), lambda i, h: (i, 0, 3 * ng + h)),
    ]


def _moba_attention(proj, bias):
    b, s, _ = proj.shape
    assert s // MOBA_BLOCK == 8 and MOBA_BLOCK == TQ
    ind = _block_indicator(s)
    return pl.pallas_call(
        functools.partial(_moba_kernel, seq=s),
        grid=(b, MOBA_HEADS // HEADS_PER_STEP),
        in_specs=[pl.BlockSpec((HEAD_DIM, s), lambda i, h: (0, 0)),
                  pl.BlockSpec((s, HEAD_DIM), lambda i, h: (0, 0))]
        + _head_specs(s, MOBA_HEADS)
        + [pl.BlockSpec((HEADS_PER_STEP, 2, TQ, TQ), lambda i, h: (h, 0, 0, 0))],
        out_specs=pl.BlockSpec((1, s, HEADS_PER_STEP * HEAD_DIM), lambda i, h: (i, 0, h)),
        out_shape=jax.ShapeDtypeStruct((b, s, D_INNER), BF16),
        compiler_params=pltpu.CompilerParams(
            dimension_semantics=("arbitrary", "arbitrary"),
            vmem_limit_bytes=VMEM_LIMIT),
        name="moba_attention",
    )(jnp.asarray(ind, BF16), jnp.asarray(ind.T, BF16), proj, proj, proj, proj, bias)


def _fox_gate_kernel(x_ref, wf_ref, bf_ref, row_ref, col_ref, *, seq):
    f = _dot_nt(x_ref[0], wf_ref[...]) + bf_ref[...]
    c = jnp.minimum(f, 0.0) - jnp.log1p(jnp.exp(-jnp.abs(f)))
    c = c.T[0:FOX_HEADS]
    lane = lax.broadcasted_iota(jnp.int32, c.shape, 1)
    d = 1
    while d < seq:
        c = c + jnp.where(lane >= d, pltpu.roll(c, d, 1), 0.0)
        d *= 2
    c = c * LOG2E
    row_ref[0] = c
    col_ref[0] = jnp.concatenate([c, jnp.zeros((HEAD_DIM - FOX_HEADS, seq), F32)], axis=0).T


def _fox_gate(h_bf3, wf, bf_row):
    b, s, d = h_bf3.shape
    return pl.pallas_call(
        functools.partial(_fox_gate_kernel, seq=s),
        grid=(b,),
        in_specs=[
            pl.BlockSpec((1, s, d), lambda i: (i, 0, 0)),
            pl.BlockSpec((HEAD_DIM, d), lambda i: (0, 0)),
            pl.BlockSpec((1, HEAD_DIM), lambda i: (0, 0)),
        ],
        out_specs=[pl.BlockSpec((1, FOX_HEADS, s), lambda i: (i, 0, 0)),
                   pl.BlockSpec((1, s, HEAD_DIM), lambda i: (i, 0, 0))],
        out_shape=[jax.ShapeDtypeStruct((b, FOX_HEADS, s), F32),
                   jax.ShapeDtypeStruct((b, s, HEAD_DIM), F32)],
        compiler_params=pltpu.CompilerParams(
            dimension_semantics=("arbitrary",),
            vmem_limit_bytes=VMEM_LIMIT),
        name="fox_gate",
    )(h_bf3, wf, bf_row)


def _causal_tile():
    r = np.arange(TQ)[:, None]
    c = np.arange(TQ)[None, :]
    return np.where(c <= r, 0.0, NEG).astype(np.float32)


def _fox_kernel(mask_ref, crow_ref, ccol_ref, q_ref, k_ref, v_ref, z_ref, o_ref, *, seq):
    def logits(hd, qi):
        r0 = qi * TQ
        c0 = hd * HEAD_DIM
        q = q_ref[0, r0:r0 + TQ, c0:c0 + HEAD_DIM]
        parts = []
        if qi >= 1:
            parts.append(_dot_nt(q, k_ref[0, 0:r0, c0:c0 + HEAD_DIM]) - crow_ref[0, hd, :, 0:r0])
        parts.append(_dot_nt(q, k_ref[0, r0:r0 + TQ, c0:c0 + HEAD_DIM]) - crow_ref[0, hd, :, r0:r0 + TQ]
                     + mask_ref[...])
        return parts

    nq = seq // TQ
    lane = lax.broadcasted_iota(jnp.int32, (TQ, HEAD_DIM), 1)
    tasks = [(hd, qi) for hd in range(HEADS_PER_STEP) for qi in reversed(range(nq))]
    nxt = logits(*tasks[0])
    for t, (hd, qi) in enumerate(tasks):
        r0 = qi * TQ
        c0 = hd * HEAD_DIM
        parts = nxt
        if t + 1 < len(tasks):
            nxt = logits(*tasks[t + 1])
        head_lane = lane == pl.program_id(1) * HEADS_PER_STEP + hd
        c_t = jnp.sum(jnp.where(head_lane, ccol_ref[0, r0:r0 + TQ, :], 0.0), axis=1, keepdims=True)
        o = _softmax_pv(parts, v_ref[0, 0:r0 + TQ, c0:c0 + HEAD_DIM], row_shift=c_t)
        z = z_ref[0, r0:r0 + TQ, c0:c0 + HEAD_DIM].astype(F32)
        o_ref[0, r0:r0 + TQ, c0:c0 + HEAD_DIM] = (o * _silu(z)).astype(BF16)


def _fox_attention(proj, cum_row, cum_col):
    b, s, _ = proj.shape
    crow = cum_row.reshape(b, FOX_HEADS, 1, s)
    mask = jnp.asarray(_causal_tile())
    return pl.pallas_call(
        functools.partial(_fox_kernel, seq=s),
        grid=(b, FOX_HEADS // HEADS_PER_STEP),
        in_specs=[
            pl.BlockSpec((TQ, TQ), lambda i, h: (0, 0)),
            pl.BlockSpec((1, HEADS_PER_STEP, 1, s), lambda i, h: (i, h, 0, 0)),
            pl.BlockSpec((1, s, HEAD_DIM), lambda i, h: (i, 0, 0)),
        ] + _head_specs(s, FOX_HEADS),
        out_specs=pl.BlockSpec((1, s, HEADS_PER_STEP * HEAD_DIM), lambda i, h: (i, 0, h)),
        out_shape=jax.ShapeDtypeStruct((b, s, D_INNER), BF16),
        compiler_params=pltpu.CompilerParams(
            dimension_semantics=("arbitrary", "arbitrary"),
            vmem_limit_bytes=VMEM_LIMIT),
        name="fox_attention",
    )(mask, crow, cum_col, proj, proj, proj, proj)


def _col_scale():
    cs = np.ones((1, 4 * D_INNER), np.float32)
    cs[0, :D_INNER] = QK_SCALE * LOG2E
    return jnp.asarray(cs)


def kernel(x, rel_bias, w_in_0, lam_q1_0, lam_k1_0, lam_q2_0, lam_k2_0, subln_g_0, w_out_0, ln_g_0, ln_b_0, w_in_1, w_out_1, ln_g_1, ln_b_1, w_in_2, b_f_2, w_out_2, ln_g_2, ln_b_2, w_in_3, lam_q1_3, lam_k1_3, lam_q2_3, lam_k2_3, subln_g_3, w_out_3, ln_g_3, ln_b_3):
    b, s, d = x.shape
    m = b * s
    e = D_INNER
    col_scale = _col_scale()
    bias = _bias_tiles(rel_bias)

    h = x.reshape(m, d)
    h_bf = h
    layers = [
        (w_in_0, w_out_0, ln_g_0, ln_b_0),
        (w_in_1, w_out_1, ln_g_1, ln_b_1),
        (w_in_2, w_out_2, ln_g_2, ln_b_2),
        (w_in_3, w_out_3, ln_g_3, ln_b_3),
    ]
    diff_extra = {
        0: (lam_q1_0, lam_k1_0, lam_q2_0, lam_k2_0, subln_g_0),
        3: (lam_q1_3, lam_k1_3, lam_q2_3, lam_k2_3, subln_g_3),
    }
    for i, (w_in, w_out, g, beta) in enumerate(layers):
        kind = i % 3
        if kind == 2:
            w_t = w_in.T
            proj, w_out_bf = _in_proj(h_bf, w_t, col_scale, 4 * e, w_out, w_is_nk=True)
        else:
            proj, w_out_bf = _in_proj(h_bf, w_in, col_scale, 4 * e, w_out)
        proj = proj.reshape(b, s, 4 * e)
        if kind == 0:
            lq1, lk1, lq2, lk2, sg = diff_extra[i]
            lam_init = 0.8 - 0.6 * math.exp(-0.3 * i)
            a = _diff_attention(proj, jnp.stack([lq1, lk1, lq2, lk2]), sg, bias, lam_init)
        elif kind == 1:
            a = _moba_attention(proj, bias)
        else:
            wf = jnp.pad(w_t[4 * e:].astype(BF16), ((0, HEAD_DIM - FOX_HEADS), (0, 0)))
            bf_row = jnp.pad(b_f_2, (0, HEAD_DIM - FOX_HEADS)).reshape(1, HEAD_DIM)
            cum_row, cum_col = _fox_gate(h_bf.reshape(b, s, d), wf, bf_row)
            a = _fox_attention(proj, cum_row, cum_col)
        outs = _out_proj_ln(a.reshape(m, e), w_out_bf, h, g, beta, emit_bf16=i + 1 < DEPTH)
        h, h_bf = outs[0], outs[-1]
    return h.reshape(b, s, d)
```

```python
import functools
import math

import jax
import jax.numpy as jnp
import numpy as np
from jax import lax
from jax.experimental import pallas as pl
from jax.experimental.pallas import tpu as pltpu

D_MODEL = 2048
D_INNER = 2048
DEPTH = 4
DIFF_HEADS = 8
MOBA_HEADS = 16
MOBA_BLOCK = 256
MOBA_TOPK = 3
FOX_HEADS = 16
HEAD_DIM = 128
REL_BUCKETS = 32
REL_MAX_DIST = 128
REL_HEADS = 16
LN_EPS = 1e-5
RMS_EPS = 1e-5
NEG = -1e30
DN_ALPHA = (2.0 * DEPTH) ** 0.25
LOG2E = math.log2(math.e)
QK_SCALE = HEAD_DIM ** -0.5

TQ = 256
DIFF_BLOCKS_AHEAD = 2
VMEM_LIMIT = 60 * 1024 * 1024

F32 = jnp.float32
BF16 = jnp.bfloat16


def _dot_nt(a, b):
    return lax.dot_general(a, b, (((1,), (1,)), ((), ())), preferred_element_type=F32)


def _silu(z):
    return z / (1.0 + jnp.exp(-z))


def _inproj_kernel(x_ref, w_ref, cs_ref, wo_ref, o_ref, wo_bf_ref, wbf_ref, *, w_is_nk):
    @pl.when(pl.program_id(1) == 0)
    def _():
        wbf_ref[...] = w_ref[...].astype(BF16)

    x = x_ref[...].astype(BF16)
    if w_is_nk:
        acc = _dot_nt(x, wbf_ref[...])
    else:
        acc = jnp.dot(x, wbf_ref[...], preferred_element_type=F32)
    o_ref[...] = (acc * cs_ref[...]).astype(o_ref.dtype)
    wo_bf_ref[...] = wo_ref[...].astype(BF16)


def _in_proj(x, w, col_scale, n_out, w_out, w_is_nk=False):
    m, k = x.shape
    tm, tn = (2048, 1024) if x.dtype == BF16 else (512, 2048)
    n_tiles, m_tiles = n_out // tn, m // tm
    ko, no = w_out.shape
    slab = ko // (n_tiles * m_tiles)
    if w_is_nk:
        w_block, w_spec = (tn, k), pl.BlockSpec((tn, k), lambda n, i: (n, 0))
    else:
        w_block, w_spec = (k, tn), pl.BlockSpec((k, tn), lambda n, i: (0, n))
    return pl.pallas_call(
        functools.partial(_inproj_kernel, w_is_nk=w_is_nk),
        grid=(n_tiles, m_tiles),
        in_specs=[
            pl.BlockSpec((tm, k), lambda n, i: (i, 0)),
            w_spec,
            pl.BlockSpec((1, tn), lambda n, i: (0, n)),
            pl.BlockSpec((slab, no), lambda n, i: (n * m_tiles + i, 0)),
        ],
        out_specs=[pl.BlockSpec((tm, tn), lambda n, i: (i, n)),
                   pl.BlockSpec((slab, no), lambda n, i: (n * m_tiles + i, 0))],
        out_shape=[jax.ShapeDtypeStruct((m, n_out), BF16),
                   jax.ShapeDtypeStruct((ko, no), BF16)],
        scratch_shapes=[pltpu.VMEM(w_block, BF16)],
        compiler_params=pltpu.CompilerParams(
            dimension_semantics=("arbitrary", "arbitrary"),
            vmem_limit_bytes=VMEM_LIMIT),
        name="in_proj",
    )(x, w, col_scale, w_out)


LN_CHUNKS = 4


def _data_zero(x):
    rows, n = x.shape
    x = jnp.max(x.reshape(rows // 8, 8, n), axis=0)
    bits = pltpu.bitcast(_fold_lanes([x], jnp.maximum), jnp.uint32)
    zero = lax.shift_right_logical(lax.shift_right_logical(bits, jnp.uint32(16)), jnp.uint32(16))
    return pltpu.bitcast(zero, F32)[0:1, 0:1]


def _outproj_ln_kernel(a_ref, w_ref, h_ref, g_ref, b_ref, o_ref, *rest, nblk):
    obf_ref = rest[0] if len(rest) == 3 else None
    y0_ref, y1_ref = rest[-2:]
    i = pl.program_id(0)
    ys = (y0_ref, y1_ref)
    tm, n = o_ref.shape
    rows, cols = tm // LN_CHUNKS, n // LN_CHUNKS

    def layer_norm(y_ref, c):
        rs = slice(c * rows, (c + 1) * rows)
        r = DN_ALPHA * h_ref[rs, :] + y_ref[rs, :]
        mu = jnp.mean(r, axis=1, keepdims=True)
        d = r - mu
        var = jnp.mean(d * d, axis=1, keepdims=True)
        out = d * lax.rsqrt(var + LN_EPS) * g_ref[...] + b_ref[...]
        o_ref[rs, :] = out
        if obf_ref is not None:
            obf_ref[rs, :] = out.astype(BF16)
        return out

    def matmul(y_ref, c, anchor=None):
        cs = slice(c * cols, (c + 1) * cols)
        if anchor is None:
            y = jnp.dot(a_ref[...], w_ref[:, cs], preferred_element_type=F32)
        else:
            head = a_ref[:, 0:HEAD_DIM] + anchor.astype(BF16)
            y = (jnp.dot(head, w_ref[0:HEAD_DIM, cs], preferred_element_type=F32)
                 + jnp.dot(a_ref[:, HEAD_DIM:], w_ref[HEAD_DIM:, cs], preferred_element_type=F32))
        y_ref[:, cs] = y

    @pl.when(i == 0)
    def _():
        for c in range(LN_CHUNKS):
            matmul(ys[0], c)

    for parity in range(2):
        @pl.when(jnp.logical_and(jnp.logical_and(i > 0, i < nblk), i % 2 == parity))
        def _():
            zero = None
            for c in range(LN_CHUNKS):
                matmul(ys[parity], c, anchor=zero)
                zero = _data_zero(layer_norm(ys[1 - parity], c))

    @pl.when(i == nblk)
    def _():
        for c in range(LN_CHUNKS):
            layer_norm(ys[(nblk - 1) % 2], c)


def _out_proj_ln(a_bf, w_bf, h, g, b, emit_bf16, tm=512):
    m, k = a_bf.shape
    n = w_bf.shape[1]
    nblk = m // tm
    cur = lambda i: (jnp.minimum(i, nblk - 1), 0)
    prev = lambda i: (jnp.maximum(i - 1, 0), 0)
    n_out = 2 if emit_bf16 else 1
    return pl.pallas_call(
        functools.partial(_outproj_ln_kernel, nblk=nblk),
        grid=(nblk + 1,),
        in_specs=[
            pl.BlockSpec((tm, k), cur),
            pl.BlockSpec((k, n), lambda i: (0, 0)),
            pl.BlockSpec((tm, n), prev),
            pl.BlockSpec((1, n), lambda i: (0, 0)),
            pl.BlockSpec((1, n), lambda i: (0, 0)),
        ],
        out_specs=[pl.BlockSpec((tm, n), prev)] * n_out,
        out_shape=[jax.ShapeDtypeStruct((m, n), F32), jax.ShapeDtypeStruct((m, n), BF16)][:n_out],
        scratch_shapes=[pltpu.VMEM((tm, n), F32), pltpu.VMEM((tm, n), F32)],
        compiler_params=pltpu.CompilerParams(
            dimension_semantics=("arbitrary",),
            vmem_limit_bytes=VMEM_LIMIT),
        name="out_proj_ln",
    )(a_bf, w_bf, h, g.reshape(1, n), b.reshape(1, n))


BIAS_SPAN = 4 * TQ


def _bucket_profile():
    n = np.maximum(2 * TQ - np.arange(BIAS_SPAN), 0)
    max_exact = REL_BUCKETS // 2
    nf = np.maximum(n, 1).astype(np.float32)
    large = max_exact + (np.log(nf / max_exact) / math.log(REL_MAX_DIST / max_exact)
                         * (REL_BUCKETS - max_exact)).astype(np.int32)
    large = np.minimum(large, REL_BUCKETS - 1)
    return np.where(n < max_exact, n, large).astype(np.int32).reshape(1, BIAS_SPAN)


def _bias_kernel(tab_ref, bkt_ref, o_ref):
    c = pl.program_id(0)
    far = tab_ref[c, REL_BUCKETS - 1]
    bk = bkt_ref[...]
    acc = jnp.zeros((1, BIAS_SPAN), F32)
    for j in range(REL_BUCKETS):
        acc = jnp.where(bk == j, tab_ref[c, j], acc)
    lane = lax.broadcasted_iota(jnp.int32, (1, BIAS_SPAN), 1)
    prof = jnp.where(lane > 2 * TQ, NEG, (acc - far) * LOG2E)
    rolled = pltpu.roll(jnp.broadcast_to(prof, (TQ, BIAS_SPAN)), 0, 1, stride=1, stride_axis=0)
    o_ref[0, 0] = rolled[:, 2 * TQ:3 * TQ]
    o_ref[0, 1] = rolled[:, TQ:2 * TQ]


def _bias_tiles(rel_bias):
    tab = rel_bias.T
    bkt = jnp.asarray(_bucket_profile())
    return pl.pallas_call(
        _bias_kernel,
        grid=(REL_HEADS,),
        in_specs=[
            pl.BlockSpec(memory_space=pltpu.SMEM),
            pl.BlockSpec((1, BIAS_SPAN), lambda c: (0, 0)),
        ],
        out_specs=pl.BlockSpec((1, 2, TQ, TQ), lambda c: (c, 0, 0, 0)),
        out_shape=jax.ShapeDtypeStruct((REL_HEADS, 2, TQ, TQ), F32),
        name="rel_bias_tiles",
    )(tab, bkt)


def _fold_lanes(parts, op):
    acc = None
    for p in parts:
        for t in range(p.shape[1] // HEAD_DIM):
            blk = p[:, t * HEAD_DIM:(t + 1) * HEAD_DIM]
            acc = blk if acc is None else op(acc, blk)
    return acc


def _softmax_pv(parts, v, row_shift=None):
    m = _fold_lanes(parts, jnp.maximum).max(axis=1, keepdims=True)
    if row_shift is None:
        shift = -m
    else:
        m_full = m + row_shift
        shift = row_shift - m_full
    ps = [jnp.exp2(p + shift) for p in parts]
    pb = jnp.concatenate([p.astype(BF16) for p in ps], axis=1) if len(ps) > 1 else ps[0].astype(BF16)
    if v.shape[1] == HEAD_DIM:
        v1 = jnp.concatenate([v, jnp.ones(v.shape, v.dtype)], axis=1)
        acc = jnp.dot(pb, v1, preferred_element_type=F32)
        return acc[:, 0:HEAD_DIM] / acc[:, HEAD_DIM:HEAD_DIM + 1]
    l = _fold_lanes(ps, jnp.add).sum(axis=1, keepdims=True)
    acc = jnp.dot(pb, v, preferred_element_type=F32)
    return acc / l


def _diff_kernel(lamv_ref, g_ref, q_ref, k_ref, v_ref, z_ref, bias_ref, o_ref, *, lam_init, seq):
    lv = lamv_ref[...]
    s1 = jnp.sum(lv[0:1] * lv[1:2], axis=1, keepdims=True)
    s2 = jnp.sum(lv[2:3] * lv[3:4], axis=1, keepdims=True)
    lam = jnp.exp(s1) - jnp.exp(s2) + lam_init

    def logits(qi, j):
        r0 = qi * TQ
        c0 = j * HEAD_DIM
        q = q_ref[0, r0:r0 + TQ, c0:c0 + HEAD_DIM]
        parts = []
        if qi >= 2:
            parts.append(_dot_nt(q, k_ref[0, 0:r0 - TQ, c0:c0 + HEAD_DIM]))
        if qi >= 1:
            parts.append(_dot_nt(q, k_ref[0, r0 - TQ:r0, c0:c0 + HEAD_DIM]) + bias_ref[j, 1])
        parts.append(_dot_nt(q, k_ref[0, r0:r0 + TQ, c0:c0 + HEAD_DIM]) + bias_ref[j, 0])
        return parts

    def probs(parts):
        m = _fold_lanes(parts, jnp.maximum).max(axis=1, keepdims=True)
        ps = [jnp.exp2(p - m) for p in parts]
        l = _fold_lanes(ps, jnp.add).sum(axis=1, keepdims=True)
        return jnp.concatenate([p.astype(BF16) for p in ps], axis=1), l

    order = list(reversed(range(seq // TQ)))
    queue = [(logits(qi, 0), logits(qi, 1)) for qi in order[:DIFF_BLOCKS_AHEAD]]
    for t, qi in enumerate(order):
        r0 = qi * TQ
        parts0, parts1 = queue.pop(0)
        if t + DIFF_BLOCKS_AHEAD < len(order):
            nq = order[t + DIFF_BLOCKS_AHEAD]
            queue.append((logits(nq, 0), logits(nq, 1)))
        pb0, l0 = probs(parts0)
        pb1, l1 = probs(parts1)
        acc = jnp.dot(jnp.concatenate([pb0, pb1], axis=0), v_ref[0, 0:r0 + TQ, :], preferred_element_type=F32)
        o = acc[0:TQ] / l0 - lam * (acc[TQ:2 * TQ] / l1)
        ms = jnp.mean(o * o, axis=1, keepdims=True)
        o = o * lax.rsqrt(ms + RMS_EPS) * g_ref[...] * (1.0 - lam_init)
        z = z_ref[0, r0:r0 + TQ, :].astype(F32)
        o_ref[0, r0:r0 + TQ, :] = (o * _silu(z)).astype(BF16)


def _diff_attention(proj, lamv, subln_g, bias, lam_init):
    b, s, _ = proj.shape
    w = 2 * HEAD_DIM
    nh = D_INNER // w
    return pl.pallas_call(
        functools.partial(_diff_kernel, lam_init=lam_init, seq=s),
        grid=(b, DIFF_HEADS),
        in_specs=[
            pl.BlockSpec((4, HEAD_DIM), lambda i, h: (0, 0)),
            pl.BlockSpec((1, w), lambda i, h: (0, 0)),
            pl.BlockSpec((1, s, w), lambda i, h: (i, 0, h)),
            pl.BlockSpec((1, s, w), lambda i, h: (i, 0, nh + h)),
            pl.BlockSpec((1, s, w), lambda i, h: (i, 0, 2 * nh + h)),
            pl.BlockSpec((1, s, w), lambda i, h: (i, 0, 3 * nh + h)),
            pl.BlockSpec((2, 2, TQ, TQ), lambda i, h: (h, 0, 0, 0)),
        ],
        out_specs=pl.BlockSpec((1, s, w), lambda i, h: (i, 0, h)),
        out_shape=jax.ShapeDtypeStruct((b, s, D_INNER), BF16),
        compiler_params=pltpu.CompilerParams(
            dimension_semantics=("arbitrary", "arbitrary"),
            vmem_limit_bytes=VMEM_LIMIT),
        name="diff_attention",
    )(lamv, subln_g.reshape(1, w), proj, proj, proj, proj, bias)


def _block_indicator(seq):
    n_kb = seq // MOBA_BLOCK
    ind = np.zeros((HEAD_DIM, seq), np.float32)
    for j in range(n_kb):
        ind[j, j * MOBA_BLOCK:(j + 1) * MOBA_BLOCK] = 1.0
    return ind


HEADS_PER_STEP = 2


def _moba_kernel(ind_ref, kind_ref, q_ref, k_ref, v_ref, z_ref, bias_ref, o_ref, *, seq):
    n_kb = seq // MOBA_BLOCK
    sub = lax.broadcasted_iota(jnp.int32, (n_kb, TQ), 0)

    def block_means(hd):
        c0 = hd * HEAD_DIM
        ksum = jnp.dot(ind_ref[...], k_ref[0, :, c0:c0 + HEAD_DIM], preferred_element_type=F32)
        kmean = ksum * (1.0 / MOBA_BLOCK)
        hi = kmean.astype(BF16)
        lo = (kmean - hi.astype(F32)).astype(BF16)
        return jnp.concatenate([hi, lo], axis=1)

    km2 = [block_means(hd) for hd in range(HEADS_PER_STEP)]

    def block_mask(hd, qi, q):
        gate = _dot_nt(km2[hd], jnp.concatenate([q, q], axis=1))[0:n_kb]
        rank = jnp.zeros((n_kb, TQ), F32)
        for bp in range(qi):
            gb = gate[bp:bp + 1, :]
            rank = rank + jnp.where(sub > bp, jnp.where(gb >= gate, 1.0, 0.0), jnp.where(gb > gate, 1.0, 0.0))
        mt = jnp.where(sub >= qi, 0.0, jnp.where(rank < float(MOBA_TOPK), 0.0, NEG))
        mt = jnp.concatenate([mt, jnp.zeros((HEAD_DIM - n_kb, TQ), F32)], axis=0)
        return mt.T.astype(BF16)

    masks = {(hd, qi): block_mask(hd, qi, q_ref[0, qi * TQ:(qi + 1) * TQ, hd * HEAD_DIM:(hd + 1) * HEAD_DIM])
             for hd in range(HEADS_PER_STEP) for qi in reversed(range(MOBA_TOPK + 1, n_kb))}

    def logits(hd, qi):
        r0 = qi * TQ
        c0 = hd * HEAD_DIM
        q = q_ref[0, r0:r0 + TQ, c0:c0 + HEAD_DIM]
        masked = qi > MOBA_TOPK
        if masked:
            q = jnp.concatenate([q, masks[hd, qi]], axis=1)
        parts = []
        for j in range(qi + 1):
            kj = k_ref[0, j * TQ:(j + 1) * TQ, c0:c0 + HEAD_DIM]
            if masked:
                kj = jnp.concatenate([kj, kind_ref[j * TQ:(j + 1) * TQ, :]], axis=1)
            sj = _dot_nt(q, kj)
            if j >= qi - 1:
                sj = sj + bias_ref[hd, qi - j]
            parts.append(sj)
        return parts

    order = [MOBA_TOPK] + [qi for qi in reversed(range(n_kb)) if qi != MOBA_TOPK]
    tasks = [(hd, qi) for hd in range(HEADS_PER_STEP) for qi in order]
    nxt = logits(*tasks[0])
    for t, (hd, qi) in enumerate(tasks):
        r0 = qi * TQ
        c0 = hd * HEAD_DIM
        parts = nxt
        if t + 1 < len(tasks):
            nxt = logits(*tasks[t + 1])
        o = _softmax_pv(parts, v_ref[0, 0:r0 + TQ, c0:c0 + HEAD_DIM])
        z = z_ref[0, r0:r0 + TQ, c0:c0 + HEAD_DIM].astype(F32)
        o_ref[0, r0:r0 + TQ, c0:c0 + HEAD_DIM] = (o * _silu(z)).astype(BF16)


def _head_specs(s, nh):
    w = HEADS_PER_STEP * HEAD_DIM
    ng = nh // HEADS_PER_STEP
    return [
        pl.BlockSpec((1, s, w), lambda i, h: (i, 0, h)),
        pl.BlockSpec((1, s, w), lambda i, h: (i, 0, ng + h)),
        pl.BlockSpec((1, s, w), lambda i, h: (i, 0, 2 * ng + h)),
        pl.BlockSpec((1, s, w), lambda i, h: (i, 0, 3 * ng + h)),
    ]


def _moba_attention(proj, bias):
    b, s, _ = proj.shape
    assert s // MOBA_BLOCK == 8 and MOBA_BLOCK == TQ
    ind = _block_indicator(s)
    return pl.pallas_call(
        functools.partial(_moba_kernel, seq=s),
        grid=(b, MOBA_HEADS // HEADS_PER_STEP),
        in_specs=[pl.BlockSpec((HEAD_DIM, s), lambda i, h: (0, 0)),
                  pl.BlockSpec((s, HEAD_DIM), lambda i, h: (0, 0))]
        + _head_specs(s, MOBA_HEADS)
        + [pl.BlockSpec((HEADS_PER_STEP, 2, TQ, TQ), lambda i, h: (h, 0, 0, 0))],
        out_specs=pl.BlockSpec((1, s, HEADS_PER_STEP * HEAD_DIM), lambda i, h: (i, 0, h)),
        out_shape=jax.ShapeDtypeStruct((b, s, D_INNER), BF16),
        compiler_params=pltpu.CompilerParams(
            dimension_semantics=("arbitrary", "arbitrary"),
            vmem_limit_bytes=VMEM_LIMIT),
        name="moba_attention",
    )(jnp.asarray(ind, BF16), jnp.asarray(ind.T, BF16), proj, proj, proj, proj, bias)


def _fox_gate_kernel(x_ref, wf_ref, bf_ref, row_ref, col_ref, *, seq):
    f = _dot_nt(x_ref[0], wf_ref[...]) + bf_ref[...]
    c = jnp.minimum(f, 0.0) - jnp.log1p(jnp.exp(-jnp.abs(f)))
    c = c.T[0:FOX_HEADS]
    lane = lax.broadcasted_iota(jnp.int32, c.shape, 1)
    d = 1
    while d < seq:
        c = c + jnp.where(lane >= d, pltpu.roll(c, d, 1), 0.0)
        d *= 2
    c = c * LOG2E
    row_ref[0] = c
    col_ref[0] = jnp.concatenate([c, jnp.zeros((HEAD_DIM - FOX_HEADS, seq), F32)], axis=0).T


def _fox_gate(h_bf3, wf, bf_row):
    b, s, d = h_bf3.shape
    return pl.pallas_call(
        functools.partial(_fox_gate_kernel, seq=s),
        grid=(b,),
        in_specs=[
            pl.BlockSpec((1, s, d), lambda i: (i, 0, 0)),
            pl.BlockSpec((HEAD_DIM, d), lambda i: (0, 0)),
            pl.BlockSpec((1, HEAD_DIM), lambda i: (0, 0)),
        ],
        out_specs=[pl.BlockSpec((1, FOX_HEADS, s), lambda i: (i, 0, 0)),
                   pl.BlockSpec((1, s, HEAD_DIM), lambda i: (i, 0, 0))],
        out_shape=[jax.ShapeDtypeStruct((b, FOX_HEADS, s), F32),
                   jax.ShapeDtypeStruct((b, s, HEAD_DIM), F32)],
        compiler_params=pltpu.CompilerParams(
            dimension_semantics=("arbitrary",),
            vmem_limit_bytes=VMEM_LIMIT),
        name="fox_gate",
    )(h_bf3, wf, bf_row)


def _causal_tile():
    r = np.arange(TQ)[:, None]
    c = np.arange(TQ)[None, :]
    return np.where(c <= r, 0.0, NEG).astype(np.float32)


def _fox_kernel(mask_ref, crow_ref, ccol_ref, q_ref, k_ref, v_ref, z_ref, o_ref, *, seq):
    def logits(hd, qi):
        r0 = qi * TQ
        c0 = hd * HEAD_DIM
        q = q_ref[0, r0:r0 + TQ, c0:c0 + HEAD_DIM]
        parts = []
        if qi >= 1:
            parts.append(_dot_nt(q, k_ref[0, 0:r0, c0:c0 + HEAD_DIM]) - crow_ref[0, hd, :, 0:r0])
        parts.append(_dot_nt(q, k_ref[0, r0:r0 + TQ, c0:c0 + HEAD_DIM]) - crow_ref[0, hd, :, r0:r0 + TQ]
                     + mask_ref[...])
        return parts

    nq = seq // TQ
    lane = lax.broadcasted_iota(jnp.int32, (TQ, HEAD_DIM), 1)
    tasks = [(hd, qi) for hd in range(HEADS_PER_STEP) for qi in reversed(range(nq))]
    nxt = logits(*tasks[0])
    for t, (hd, qi) in enumerate(tasks):
        r0 = qi * TQ
        c0 = hd * HEAD_DIM
        parts = nxt
        if t + 1 < len(tasks):
            nxt = logits(*tasks[t + 1])
        head_lane = lane == pl.program_id(1) * HEADS_PER_STEP + hd
        c_t = jnp.sum(jnp.where(head_lane, ccol_ref[0, r0:r0 + TQ, :], 0.0), axis=1, keepdims=True)
        o = _softmax_pv(parts, v_ref[0, 0:r0 + TQ, c0:c0 + HEAD_DIM], row_shift=c_t)
        z = z_ref[0, r0:r0 + TQ, c0:c0 + HEAD_DIM].astype(F32)
        o_ref[0, r0:r0 + TQ, c0:c0 + HEAD_DIM] = (o * _silu(z)).astype(BF16)


def _fox_attention(proj, cum_row, cum_col):
    b, s, _ = proj.shape
    crow = cum_row.reshape(b, FOX_HEADS, 1, s)
    mask = jnp.asarray(_causal_tile())
    return pl.pallas_call(
        functools.partial(_fox_kernel, seq=s),
        grid=(b, FOX_HEADS // HEADS_PER_STEP),
        in_specs=[
            pl.BlockSpec((TQ, TQ), lambda i, h: (0, 0)),
            pl.BlockSpec((1, HEADS_PER_STEP, 1, s), lambda i, h: (i, h, 0, 0)),
            pl.BlockSpec((1, s, HEAD_DIM), lambda i, h: (i, 0, 0)),
        ] + _head_specs(s, FOX_HEADS),
        out_specs=pl.BlockSpec((1, s, HEADS_PER_STEP * HEAD_DIM), lambda i, h: (i, 0, h)),
        out_shape=jax.ShapeDtypeStruct((b, s, D_INNER), BF16),
        compiler_params=pltpu.CompilerParams(
            dimension_semantics=("arbitrary", "arbitrary"),
            vmem_limit_bytes=VMEM_LIMIT),
        name="fox_attention",
    )(mask, crow, cum_col, proj, proj, proj, proj)


def _col_scale():
    cs = np.ones((1, 4 * D_INNER), np.float32)
    cs[0, :D_INNER] = QK_SCALE * LOG2E
    return jnp.asarray(cs)


def kernel(x, rel_bias, w_in_0, lam_q1_0, lam_k1_0, lam_q2_0, lam_k2_0, subln_g_0, w_out_0, ln_g_0, ln_b_0, w_in_1, w_out_1, ln_g_1, ln_b_1, w_in_2, b_f_2, w_out_2, ln_g_2, ln_b_2, w_in_3, lam_q1_3, lam_k1_3, lam_q2_3, lam_k2_3, subln_g_3, w_out_3, ln_g_3, ln_b_3):
    b, s, d = x.shape
    m = b * s
    e = D_INNER
    col_scale = _col_scale()
    bias = _bias_tiles(rel_bias)

    h = x.reshape(m, d)
    h_bf = h
    layers = [
        (w_in_0, w_out_0, ln_g_0, ln_b_0),
        (w_in_1, w_out_1, ln_g_1, ln_b_1),
        (w_in_2, w_out_2, ln_g_2, ln_b_2),
        (w_in_3, w_out_3, ln_g_3, ln_b_3),
    ]
    diff_extra = {
        0: (lam_q1_0, lam_k1_0, lam_q2_0, lam_k2_0, subln_g_0),
        3: (lam_q1_3, lam_k1_3, lam_q2_3, lam_k2_3, subln_g_3),
    }
    for i, (w_in, w_out, g, beta) in enumerate(layers):
        kind = i % 3
        if kind == 2:
            w_t = w_in.T
            proj, w_out_bf = _in_proj(h_bf, w_t, col_scale, 4 * e, w_out, w_is_nk=True)
        else:
            proj, w_out_bf = _in_proj(h_bf, w_in, col_scale, 4 * e, w_out)
        proj = proj.reshape(b, s, 4 * e)
        if kind == 0:
            lq1, lk1, lq2, lk2, sg = diff_extra[i]
            lam_init = 0.8 - 0.6 * math.exp(-0.3 * i)
            a = _diff_attention(proj, jnp.stack([lq1, lk1, lq2, lk2]), sg, bias, lam_init)
        elif kind == 1:
            a = _moba_attention(proj, bias)
        else:
            wf = jnp.pad(w_t[4 * e:].astype(BF16), ((0, HEAD_DIM - FOX_HEADS), (0, 0)))
            bf_row = jnp.pad(b_f_2, (0, HEAD_DIM - FOX_HEADS)).reshape(1, HEAD_DIM)
            cum_row, cum_col = _fox_gate(h_bf.reshape(b, s, d), wf, bf_row)
            a = _fox_attention(proj, cum_row, cum_col)
        outs = _out_proj_ln(a.reshape(m, e), w_out_bf, h, g, beta, emit_bf16=i + 1 < DEPTH)
        h, h_bf = outs[0], outs[-1]
    return h.reshape(b, s, d)
```

```python
import functools
import math

import jax
import jax.numpy as jnp
import numpy as np
from jax import lax
from jax.experimental import pallas as pl
from jax.experimental.pallas import tpu as pltpu

D_MODEL = 2048
D_INNER = 2048
DEPTH = 4
DIFF_HEADS = 8
MOBA_HEADS = 16
MOBA_BLOCK = 256
MOBA_TOPK = 3
FOX_HEADS = 16
HEAD_DIM = 128
REL_BUCKETS = 32
REL_MAX_DIST = 128
REL_HEADS = 16
LN_EPS = 1e-5
RMS_EPS = 1e-5
NEG = -1e30
DN_ALPHA = (2.0 * DEPTH) ** 0.25
LOG2E = math.log2(math.e)
QK_SCALE = HEAD_DIM ** -0.5

TQ = 256
DIFF_BLOCKS_AHEAD = 2
MOBA_TASKS_AHEAD = 2
VMEM_LIMIT = 60 * 1024 * 1024

F32 = jnp.float32
BF16 = jnp.bfloat16


def _dot_nt(a, b):
    return lax.dot_general(a, b, (((1,), (1,)), ((), ())), preferred_element_type=F32)


def _silu(z):
    return z / (1.0 + jnp.exp(-z))


def _inproj_kernel(x_ref, w_ref, cs_ref, wo_ref, o_ref, wo_bf_ref, wbf_ref, *, w_is_nk):
    @pl.when(pl.program_id(1) == 0)
    def _():
        wbf_ref[...] = w_ref[...].astype(BF16)

    x = x_ref[...].astype(BF16)
    if w_is_nk:
        acc = _dot_nt(x, wbf_ref[...])
    else:
        acc = jnp.dot(x, wbf_ref[...], preferred_element_type=F32)
    o_ref[...] = (acc * cs_ref[...]).astype(o_ref.dtype)
    wo_bf_ref[...] = wo_ref[...].astype(BF16)


def _in_proj(x, w, col_scale, n_out, w_out, w_is_nk=False):
    m, k = x.shape
    tm, tn = (2048, 1024) if x.dtype == BF16 else (512, 2048)
    n_tiles, m_tiles = n_out // tn, m // tm
    ko, no = w_out.shape
    slab = ko // (n_tiles * m_tiles)
    if w_is_nk:
        w_block, w_spec = (tn, k), pl.BlockSpec((tn, k), lambda n, i: (n, 0))
    else:
        w_block, w_spec = (k, tn), pl.BlockSpec((k, tn), lambda n, i: (0, n))
    return pl.pallas_call(
        functools.partial(_inproj_kernel, w_is_nk=w_is_nk),
        grid=(n_tiles, m_tiles),
        in_specs=[
            pl.BlockSpec((tm, k), lambda n, i: (i, 0)),
            w_spec,
            pl.BlockSpec((1, tn), lambda n, i: (0, n)),
            pl.BlockSpec((slab, no), lambda n, i: (n * m_tiles + i, 0)),
        ],
        out_specs=[pl.BlockSpec((tm, tn), lambda n, i: (i, n)),
                   pl.BlockSpec((slab, no), lambda n, i: (n * m_tiles + i, 0))],
        out_shape=[jax.ShapeDtypeStruct((m, n_out), BF16),
                   jax.ShapeDtypeStruct((ko, no), BF16)],
        scratch_shapes=[pltpu.VMEM(w_block, BF16)],
        compiler_params=pltpu.CompilerParams(
            dimension_semantics=("arbitrary", "arbitrary"),
            vmem_limit_bytes=VMEM_LIMIT),
        name="in_proj",
    )(x, w, col_scale, w_out)


LN_CHUNKS = 4


def _data_zero(x):
    rows, n = x.shape
    x = jnp.max(x.reshape(rows // 8, 8, n), axis=0)
    bits = pltpu.bitcast(_fold_lanes([x], jnp.maximum), jnp.uint32)
    zero = lax.shift_right_logical(lax.shift_right_logical(bits, jnp.uint32(16)), jnp.uint32(16))
    return pltpu.bitcast(zero, F32)[0:1, 0:1]


def _outproj_ln_kernel(a_ref, w_ref, h_ref, g_ref, b_ref, o_ref, *rest, nblk):
    obf_ref = rest[0] if len(rest) == 3 else None
    y0_ref, y1_ref = rest[-2:]
    i = pl.program_id(0)
    ys = (y0_ref, y1_ref)
    tm, n = o_ref.shape
    rows, cols = tm // LN_CHUNKS, n // LN_CHUNKS

    def layer_norm(y_ref, c):
        rs = slice(c * rows, (c + 1) * rows)
        r = DN_ALPHA * h_ref[rs, :] + y_ref[rs, :]
        mu = jnp.mean(r, axis=1, keepdims=True)
        d = r - mu
        var = jnp.mean(d * d, axis=1, keepdims=True)
        out = d * lax.rsqrt(var + LN_EPS) * g_ref[...] + b_ref[...]
        o_ref[rs, :] = out
        if obf_ref is not None:
            obf_ref[rs, :] = out.astype(BF16)
        return out

    def matmul(y_ref, c, anchor=None):
        cs = slice(c * cols, (c + 1) * cols)
        if anchor is None:
            y = jnp.dot(a_ref[...], w_ref[:, cs], preferred_element_type=F32)
        else:
            head = a_ref[:, 0:HEAD_DIM] + anchor.astype(BF16)
            y = (jnp.dot(head, w_ref[0:HEAD_DIM, cs], preferred_element_type=F32)
                 + jnp.dot(a_ref[:, HEAD_DIM:], w_ref[HEAD_DIM:, cs], preferred_element_type=F32))
        y_ref[:, cs] = y

    @pl.when(i == 0)
    def _():
        for c in range(LN_CHUNKS):
            matmul(ys[0], c)

    for parity in range(2):
        @pl.when(jnp.logical_and(jnp.logical_and(i > 0, i < nblk), i % 2 == parity))
        def _():
            zero = None
            for c in range(LN_CHUNKS):
                matmul(ys[parity], c, anchor=zero)
                zero = _data_zero(layer_norm(ys[1 - parity], c))

    @pl.when(i == nblk)
    def _():
        for c in range(LN_CHUNKS):
            layer_norm(ys[(nblk - 1) % 2], c)


def _out_proj_ln(a_bf, w_bf, h, g, b, emit_bf16, tm=512):
    m, k = a_bf.shape
    n = w_bf.shape[1]
    nblk = m // tm
    cur = lambda i: (jnp.minimum(i, nblk - 1), 0)
    prev = lambda i: (jnp.maximum(i - 1, 0), 0)
    n_out = 2 if emit_bf16 else 1
    return pl.pallas_call(
        functools.partial(_outproj_ln_kernel, nblk=nblk),
        grid=(nblk + 1,),
        in_specs=[
            pl.BlockSpec((tm, k), cur),
            pl.BlockSpec((k, n), lambda i: (0, 0)),
            pl.BlockSpec((tm, n), prev),
            pl.BlockSpec((1, n), lambda i: (0, 0)),
            pl.BlockSpec((1, n), lambda i: (0, 0)),
        ],
        out_specs=[pl.BlockSpec((tm, n), prev)] * n_out,
        out_shape=[jax.ShapeDtypeStruct((m, n), F32), jax.ShapeDtypeStruct((m, n), BF16)][:n_out],
        scratch_shapes=[pltpu.VMEM((tm, n), F32), pltpu.VMEM((tm, n), F32)],
        compiler_params=pltpu.CompilerParams(
            dimension_semantics=("arbitrary",),
            vmem_limit_bytes=VMEM_LIMIT),
        name="out_proj_ln",
    )(a_bf, w_bf, h, g.reshape(1, n), b.reshape(1, n))


BIAS_SPAN = 4 * TQ


def _bucket_profile():
    n = np.maximum(2 * TQ - np.arange(BIAS_SPAN), 0)
    max_exact = REL_BUCKETS // 2
    nf = np.maximum(n, 1).astype(np.float32)
    large = max_exact + (np.log(nf / max_exact) / math.log(REL_MAX_DIST / max_exact)
                         * (REL_BUCKETS - max_exact)).astype(np.int32)
    large = np.minimum(large, REL_BUCKETS - 1)
    return np.where(n < max_exact, n, large).astype(np.int32).reshape(1, BIAS_SPAN)


def _bias_kernel(tab_ref, bkt_ref, o_ref):
    c = pl.program_id(0)
    far = tab_ref[c, REL_BUCKETS - 1]
    bk = bkt_ref[...]
    acc = jnp.zeros((1, BIAS_SPAN), F32)
    for j in range(REL_BUCKETS):
        acc = jnp.where(bk == j, tab_ref[c, j], acc)
    lane = lax.broadcasted_iota(jnp.int32, (1, BIAS_SPAN), 1)
    prof = jnp.where(lane > 2 * TQ, NEG, (acc - far) * LOG2E)
    rolled = pltpu.roll(jnp.broadcast_to(prof, (TQ, BIAS_SPAN)), 0, 1, stride=1, stride_axis=0)
    o_ref[0, 0] = rolled[:, 2 * TQ:3 * TQ]
    o_ref[0, 1] = rolled[:, TQ:2 * TQ]


def _bias_tiles(rel_bias):
    tab = rel_bias.T
    bkt = jnp.asarray(_bucket_profile())
    return pl.pallas_call(
        _bias_kernel,
        grid=(REL_HEADS,),
        in_specs=[
            pl.BlockSpec(memory_space=pltpu.SMEM),
            pl.BlockSpec((1, BIAS_SPAN), lambda c: (0, 0)),
        ],
        out_specs=pl.BlockSpec((1, 2, TQ, TQ), lambda c: (c, 0, 0, 0)),
        out_shape=jax.ShapeDtypeStruct((REL_HEADS, 2, TQ, TQ), F32),
        name="rel_bias_tiles",
    )(tab, bkt)


def _fold_lanes(parts, op):
    acc = None
    for p in parts:
        for t in range(p.shape[1] // HEAD_DIM):
            blk = p[:, t * HEAD_DIM:(t + 1) * HEAD_DIM]
            acc = blk if acc is None else op(acc, blk)
    return acc


def _softmax_pv(parts, v, row_shift=None):
    assert v.shape[1] == HEAD_DIM
    m = _fold_lanes(parts, jnp.maximum).max(axis=1, keepdims=True)
    if row_shift is None:
        shift = -m
    else:
        m_full = m + row_shift
        shift = row_shift - m_full
    ps = [jnp.exp2(p + shift) for p in parts]
    pb = jnp.concatenate([p.astype(BF16) for p in ps], axis=1) if len(ps) > 1 else ps[0].astype(BF16)
    v1 = jnp.concatenate([v, jnp.ones(v.shape, v.dtype)], axis=1)
    acc = jnp.dot(pb, v1, preferred_element_type=F32)
    return acc[:, 0:HEAD_DIM] / acc[:, HEAD_DIM:HEAD_DIM + 1]


def _diff_kernel(lamv_ref, g_ref, q_ref, k_ref, v_ref, z_ref, bias_ref, o_ref, *, lam_init, seq):
    lv = lamv_ref[...]
    s1 = jnp.sum(lv[0:1] * lv[1:2], axis=1, keepdims=True)
    s2 = jnp.sum(lv[2:3] * lv[3:4], axis=1, keepdims=True)
    lam = jnp.exp(s1) - jnp.exp(s2) + lam_init

    def logits(qi, j):
        r0 = qi * TQ
        c0 = j * HEAD_DIM
        q = q_ref[0, r0:r0 + TQ, c0:c0 + HEAD_DIM]
        parts = []
        if qi >= 2:
            parts.append(_dot_nt(q, k_ref[0, 0:r0 - TQ, c0:c0 + HEAD_DIM]))
        if qi >= 1:
            parts.append(_dot_nt(q, k_ref[0, r0 - TQ:r0, c0:c0 + HEAD_DIM]) + bias_ref[j, 1])
        parts.append(_dot_nt(q, k_ref[0, r0:r0 + TQ, c0:c0 + HEAD_DIM]) + bias_ref[j, 0])
        return parts

    def probs(parts):
        m = _fold_lanes(parts, jnp.maximum).max(axis=1, keepdims=True)
        ps = [jnp.exp2(p - m) for p in parts]
        l = _fold_lanes(ps, jnp.add).sum(axis=1, keepdims=True)
        return jnp.concatenate([p.astype(BF16) for p in ps], axis=1), l

    order = list(reversed(range(seq // TQ)))
    queue = [(logits(qi, 0), logits(qi, 1)) for qi in order[:DIFF_BLOCKS_AHEAD]]
    for t, qi in enumerate(order):
        r0 = qi * TQ
        parts0, parts1 = queue.pop(0)
        if t + DIFF_BLOCKS_AHEAD < len(order):
            nq = order[t + DIFF_BLOCKS_AHEAD]
            queue.append((logits(nq, 0), logits(nq, 1)))
        pb0, l0 = probs(parts0)
        pb1, l1 = probs(parts1)
        acc = jnp.dot(jnp.concatenate([pb0, pb1], axis=0), v_ref[0, 0:r0 + TQ, :], preferred_element_type=F32)
        o = acc[0:TQ] / l0 - lam * (acc[TQ:2 * TQ] / l1)
        ms = jnp.mean(o * o, axis=1, keepdims=True)
        o = o * lax.rsqrt(ms + RMS_EPS) * g_ref[...] * (1.0 - lam_init)
        z = z_ref[0, r0:r0 + TQ, :].astype(F32)
        o_ref[0, r0:r0 + TQ, :] = (o * _silu(z)).astype(BF16)


def _diff_attention(proj, lamv, subln_g, bias, lam_init):
    b, s, _ = proj.shape
    w = 2 * HEAD_DIM
    nh = D_INNER // w
    return pl.pallas_call(
        functools.partial(_diff_kernel, lam_init=lam_init, seq=s),
        grid=(b, DIFF_HEADS),
        in_specs=[
            pl.BlockSpec((4, HEAD_DIM), lambda i, h: (0, 0)),
            pl.BlockSpec((1, w), lambda i, h: (0, 0)),
            pl.BlockSpec((1, s, w), lambda i, h: (i, 0, h)),
            pl.BlockSpec((1, s, w), lambda i, h: (i, 0, nh + h)),
            pl.BlockSpec((1, s, w), lambda i, h: (i, 0, 2 * nh + h)),
            pl.BlockSpec((1, s, w), lambda i, h: (i, 0, 3 * nh + h)),
            pl.BlockSpec((2, 2, TQ, TQ), lambda i, h: (h, 0, 0, 0)),
        ],
        out_specs=pl.BlockSpec((1, s, w), lambda i, h: (i, 0, h)),
        out_shape=jax.ShapeDtypeStruct((b, s, D_INNER), BF16),
        compiler_params=pltpu.CompilerParams(
            dimension_semantics=("arbitrary", "arbitrary"),
            vmem_limit_bytes=VMEM_LIMIT),
        name="diff_attention",
    )(lamv, subln_g.reshape(1, w), proj, proj, proj, proj, bias)


def _block_indicator(seq):
    n_kb = seq // MOBA_BLOCK
    ind = np.zeros((HEAD_DIM, seq), np.float32)
    for j in range(n_kb):
        ind[j, j * MOBA_BLOCK:(j + 1) * MOBA_BLOCK] = 1.0
    return ind


HEADS_PER_STEP = 2


def _moba_kernel(ind_ref, kind_ref, q_ref, k_ref, v_ref, z_ref, bias_ref, o_ref, *, seq):
    n_kb = seq // MOBA_BLOCK
    sub = lax.broadcasted_iota(jnp.int32, (n_kb, TQ), 0)

    def block_means(hd):
        c0 = hd * HEAD_DIM
        ksum = jnp.dot(ind_ref[...], k_ref[0, :, c0:c0 + HEAD_DIM], preferred_element_type=F32)
        kmean = ksum * (1.0 / MOBA_BLOCK)
        hi = kmean.astype(BF16)
        lo = (kmean - hi.astype(F32)).astype(BF16)
        return jnp.concatenate([hi, lo], axis=1)

    km2 = [block_means(hd) for hd in range(HEADS_PER_STEP)]

    def block_mask(hd, qi, q):
        gate = _dot_nt(km2[hd], jnp.concatenate([q, q], axis=1))[0:n_kb]
        rank = jnp.zeros((n_kb, TQ), F32)
        for bp in range(qi):
            gb = gate[bp:bp + 1, :]
            rank = rank + jnp.where(sub > bp, jnp.where(gb >= gate, 1.0, 0.0), jnp.where(gb > gate, 1.0, 0.0))
        mt = jnp.where(sub >= qi, 0.0, jnp.where(rank < float(MOBA_TOPK), 0.0, NEG))
        mt = jnp.concatenate([mt, jnp.zeros((HEAD_DIM - n_kb, TQ), F32)], axis=0)
        return mt.T.astype(BF16)

    masks = {(hd, qi): block_mask(hd, qi, q_ref[0, qi * TQ:(qi + 1) * TQ, hd * HEAD_DIM:(hd + 1) * HEAD_DIM])
             for hd in range(HEADS_PER_STEP) for qi in reversed(range(MOBA_TOPK + 1, n_kb))}

    def logits(hd, qi):
        r0 = qi * TQ
        c0 = hd * HEAD_DIM
        q = q_ref[0, r0:r0 + TQ, c0:c0 + HEAD_DIM]
        masked = qi > MOBA_TOPK
        if masked:
            q = jnp.concatenate([q, masks[hd, qi]], axis=1)
        parts = []
        for j in range(qi + 1):
            kj = k_ref[0, j * TQ:(j + 1) * TQ, c0:c0 + HEAD_DIM]
            if masked:
                kj = jnp.concatenate([kj, kind_ref[j * TQ:(j + 1) * TQ, :]], axis=1)
            sj = _dot_nt(q, kj)
            if j >= qi - 1:
                sj = sj + bias_ref[hd, qi - j]
            parts.append(sj)
        return parts

    order = [MOBA_TOPK] + [qi for qi in reversed(range(n_kb)) if qi != MOBA_TOPK]
    tasks = [(hd, qi) for hd in range(HEADS_PER_STEP) for qi in order]
    queue = [logits(*task) for task in tasks[:MOBA_TASKS_AHEAD]]
    for t, (hd, qi) in enumerate(tasks):
        r0 = qi * TQ
        c0 = hd * HEAD_DIM
        parts = queue.pop(0)
        if t + MOBA_TASKS_AHEAD < len(tasks):
            queue.append(logits(*tasks[t + MOBA_TASKS_AHEAD]))
        o = _softmax_pv(parts, v_ref[0, 0:r0 + TQ, c0:c0 + HEAD_DIM])
        z = z_ref[0, r0:r0 + TQ, c0:c0 + HEAD_DIM].astype(F32)
        o_ref[0, r0:r0 + TQ, c0:c0 + HEAD_DIM] = (o * _silu(z)).astype(BF16)


def _head_specs(s, nh):
    w = HEADS_PER_STEP * HEAD_DIM
    ng = nh // HEADS_PER_STEP
    return [
        pl.BlockSpec((1, s, w), lambda i, h: (i, 0, h)),
        pl.BlockSpec((1, s, w), lambda i, h: (i, 0, ng + h)),
        pl.BlockSpec((1, s, w), lambda i, h: (i, 0, 2 * ng + h)),
        pl.BlockSpec((1, s, w), lambda i, h: (i, 0, 3 * ng + h)),
    ]


def _moba_attention(proj, bias):
    b, s, _ = proj.shape
    assert s // MOBA_BLOCK == 8 and MOBA_BLOCK == TQ
    ind = _block_indicator(s)
    return pl.pallas_call(
        functools.partial(_moba_kernel, seq=s),
        grid=(b, MOBA_HEADS // HEADS_PER_STEP),
        in_specs=[pl.BlockSpec((HEAD_DIM, s), lambda i, h: (0, 0)),
                  pl.BlockSpec((s, HEAD_DIM), lambda i, h: (0, 0))]
        + _head_specs(s, MOBA_HEADS)
        + [pl.BlockSpec((HEADS_PER_STEP, 2, TQ, TQ), lambda i, h: (h, 0, 0, 0))],
        out_specs=pl.BlockSpec((1, s, HEADS_PER_STEP * HEAD_DIM), lambda i, h: (i, 0, h)),
        out_shape=jax.ShapeDtypeStruct((b, s, D_INNER), BF16),
        compiler_params=pltpu.CompilerParams(
            dimension_semantics=("arbitrary", "arbitrary"),
            vmem_limit_bytes=VMEM_LIMIT),
        name="moba_attention",
    )(jnp.asarray(ind, BF16), jnp.asarray(ind.T, BF16), proj, proj, proj, proj, bias)


def _fox_gate_kernel(x_ref, wf_ref, bf_ref, row_ref, col_ref, *, seq):
    f = _dot_nt(x_ref[0], wf_ref[...]) + bf_ref[...]
    c = jnp.minimum(f, 0.0) - jnp.log1p(jnp.exp(-jnp.abs(f)))
    c = c.T[0:FOX_HEADS]
    lane = lax.broadcasted_iota(jnp.int32, c.shape, 1)
    d = 1
    while d < seq:
        c = c + jnp.where(lane >= d, pltpu.roll(c, d, 1), 0.0)
        d *= 2
    c = c * LOG2E
    row_ref[0] = c
    col_ref[0] = jnp.concatenate([c, jnp.zeros((HEAD_DIM - FOX_HEADS, seq), F32)], axis=0).T


def _fox_gate(h_bf3, wf, bf_row):
    b, s, d = h_bf3.shape
    return pl.pallas_call(
        functools.partial(_fox_gate_kernel, seq=s),
        grid=(b,),
        in_specs=[
            pl.BlockSpec((1, s, d), lambda i: (i, 0, 0)),
            pl.BlockSpec((HEAD_DIM, d), lambda i: (0, 0)),
            pl.BlockSpec((1, HEAD_DIM), lambda i: (0, 0)),
        ],
        out_specs=[pl.BlockSpec((1, FOX_HEADS, s), lambda i: (i, 0, 0)),
                   pl.BlockSpec((1, s, HEAD_DIM), lambda i: (i, 0, 0))],
        out_shape=[jax.ShapeDtypeStruct((b, FOX_HEADS, s), F32),
                   jax.ShapeDtypeStruct((b, s, HEAD_DIM), F32)],
        compiler_params=pltpu.CompilerParams(
            dimension_semantics=("arbitrary",),
            vmem_limit_bytes=VMEM_LIMIT),
        name="fox_gate",
    )(h_bf3, wf, bf_row)


def _causal_tile():
    r = np.arange(TQ)[:, None]
    c = np.arange(TQ)[None, :]
    return np.where(c <= r, 0.0, NEG).astype(np.float32)


def _fox_kernel(mask_ref, crow_ref, ccol_ref, q_ref, k_ref, v_ref, z_ref, o_ref, *, seq):
    def logits(hd, qi):
        r0 = qi * TQ
        c0 = hd * HEAD_DIM
        q = q_ref[0, r0:r0 + TQ, c0:c0 + HEAD_DIM]
        parts = []
        if qi >= 1:
            parts.append(_dot_nt(q, k_ref[0, 0:r0, c0:c0 + HEAD_DIM]) - crow_ref[0, hd, :, 0:r0])
        parts.append(_dot_nt(q, k_ref[0, r0:r0 + TQ, c0:c0 + HEAD_DIM]) - crow_ref[0, hd, :, r0:r0 + TQ]
                     + mask_ref[...])
        return parts

    nq = seq // TQ
    lane = lax.broadcasted_iota(jnp.int32, (TQ, HEAD_DIM), 1)
    tasks = [(hd, qi) for hd in range(HEADS_PER_STEP) for qi in reversed(range(nq))]
    nxt = logits(*tasks[0])
    for t, (hd, qi) in enumerate(tasks):
        r0 = qi * TQ
        c0 = hd * HEAD_DIM
        parts = nxt
        if t + 1 < len(tasks):
            nxt = logits(*tasks[t + 1])
        head_lane = lane == pl.program_id(1) * HEADS_PER_STEP + hd
        c_t = jnp.sum(jnp.where(head_lane, ccol_ref[0, r0:r0 + TQ, :], 0.0), axis=1, keepdims=True)
        o = _softmax_pv(parts, v_ref[0, 0:r0 + TQ, c0:c0 + HEAD_DIM], row_shift=c_t)
        z = z_ref[0, r0:r0 + TQ, c0:c0 + HEAD_DIM].astype(F32)
        o_ref[0, r0:r0 + TQ, c0:c0 + HEAD_DIM] = (o * _silu(z)).astype(BF16)


def _fox_attention(proj, cum_row, cum_col):
    b, s, _ = proj.shape
    crow = cum_row.reshape(b, FOX_HEADS, 1, s)
    mask = jnp.asarray(_causal_tile())
    return pl.pallas_call(
        functools.partial(_fox_kernel, seq=s),
        grid=(b, FOX_HEADS // HEADS_PER_STEP),
        in_specs=[
            pl.BlockSpec((TQ, TQ), lambda i, h: (0, 0)),
            pl.BlockSpec((1, HEADS_PER_STEP, 1, s), lambda i, h: (i, h, 0, 0)),
            pl.BlockSpec((1, s, HEAD_DIM), lambda i, h: (i, 0, 0)),
        ] + _head_specs(s, FOX_HEADS),
        out_specs=pl.BlockSpec((1, s, HEADS_PER_STEP * HEAD_DIM), lambda i, h: (i, 0, h)),
        out_shape=jax.ShapeDtypeStruct((b, s, D_INNER), BF16),
        compiler_params=pltpu.CompilerParams(
            dimension_semantics=("arbitrary", "arbitrary"),
            vmem_limit_bytes=VMEM_LIMIT),
        name="fox_attention",
    )(mask, crow, cum_col, proj, proj, proj, proj)


def _col_scale():
    cs = np.ones((1, 4 * D_INNER), np.float32)
    cs[0, :D_INNER] = QK_SCALE * LOG2E
    return jnp.asarray(cs)


def kernel(x, rel_bias, w_in_0, lam_q1_0, lam_k1_0, lam_q2_0, lam_k2_0, subln_g_0, w_out_0, ln_g_0, ln_b_0, w_in_1, w_out_1, ln_g_1, ln_b_1, w_in_2, b_f_2, w_out_2, ln_g_2, ln_b_2, w_in_3, lam_q1_3, lam_k1_3, lam_q2_3, lam_k2_3, subln_g_3, w_out_3, ln_g_3, ln_b_3):
    b, s, d = x.shape
    m = b * s
    e = D_INNER
    col_scale = _col_scale()
    bias = _bias_tiles(rel_bias)

    h = x.reshape(m, d)
    h_bf = h
    layers = [
        (w_in_0, w_out_0, ln_g_0, ln_b_0),
        (w_in_1, w_out_1, ln_g_1, ln_b_1),
        (w_in_2, w_out_2, ln_g_2, ln_b_2),
        (w_in_3, w_out_3, ln_g_3, ln_b_3),
    ]
    diff_extra = {
        0: (lam_q1_0, lam_k1_0, lam_q2_0, lam_k2_0, subln_g_0),
        3: (lam_q1_3, lam_k1_3, lam_q2_3, lam_k2_3, subln_g_3),
    }
    for i, (w_in, w_out, g, beta) in enumerate(layers):
        kind = i % 3
        if kind == 2:
            w_t = w_in.T
            proj, w_out_bf = _in_proj(h_bf, w_t, col_scale, 4 * e, w_out, w_is_nk=True)
        else:
            proj, w_out_bf = _in_proj(h_bf, w_in, col_scale, 4 * e, w_out)
        proj = proj.reshape(b, s, 4 * e)
        if kind == 0:
            lq1, lk1, lq2, lk2, sg = diff_extra[i]
            lam_init = 0.8 - 0.6 * math.exp(-0.3 * i)
            a = _diff_attention(proj, jnp.stack([lq1, lk1, lq2, lk2]), sg, bias, lam_init)
        elif kind == 1:
            a = _moba_attention(proj, bias)
        else:
            wf = jnp.pad(w_t[4 * e:].astype(BF16), ((0, HEAD_DIM - FOX_HEADS), (0, 0)))
            bf_row = jnp.pad(b_f_2, (0, HEAD_DIM - FOX_HEADS)).reshape(1, HEAD_DIM)
            cum_row, cum_col = _fox_gate(h_bf.reshape(b, s, d), wf, bf_row)
            a = _fox_attention(proj, cum_row, cum_col)
        outs = _out_proj_ln(a.reshape(m, e), w_out_bf, h, g, beta, emit_bf16=i + 1 < DEPTH)
        h, h_bf = outs[0], outs[-1]
    return h.reshape(b, s, d)
```

```python
import functools
import math

import jax
import jax.numpy as jnp
import numpy as np
from jax import lax
from jax.experimental import pallas as pl
from jax.experimental.pallas import tpu as pltpu

D_MODEL = 2048
D_INNER = 2048
DEPTH = 4
DIFF_HEADS = 8
MOBA_HEADS = 16
MOBA_BLOCK = 256
MOBA_TOPK = 3
FOX_HEADS = 16
HEAD_DIM = 128
REL_BUCKETS = 32
REL_MAX_DIST = 128
REL_HEADS = 16
LN_EPS = 1e-5
RMS_EPS = 1e-5
NEG = -1e30
DN_ALPHA = (2.0 * DEPTH) ** 0.25
LOG2E = math.log2(math.e)
QK_SCALE = HEAD_DIM ** -0.5

TQ = 256
DIFF_BLOCKS_AHEAD = 2
MOBA_TASKS_AHEAD = 2
VMEM_LIMIT = 60 * 1024 * 1024

F32 = jnp.float32
BF16 = jnp.bfloat16


def _dot_nt(a, b):
    return lax.dot_general(a, b, (((1,), (1,)), ((), ())), preferred_element_type=F32)


def _silu(z):
    return z / (1.0 + jnp.exp(-z))


def _inproj_kernel(x_ref, w_ref, cs_ref, wo_ref, o_ref, wo_bf_ref, wbf_ref, *, w_is_nk):
    @pl.when(pl.program_id(1) == 0)
    def _():
        wbf_ref[...] = w_ref[...].astype(BF16)

    x = x_ref[...].astype(BF16)
    if w_is_nk:
        acc = _dot_nt(x, wbf_ref[...])
    else:
        acc = jnp.dot(x, wbf_ref[...], preferred_element_type=F32)
    o_ref[...] = (acc * cs_ref[...]).astype(o_ref.dtype)
    wo_bf_ref[...] = wo_ref[...].astype(BF16)


def _in_proj(x, w, col_scale, n_out, w_out, w_is_nk=False):
    m, k = x.shape
    tm, tn = (2048, 1024) if x.dtype == BF16 else (512, 2048)
    n_tiles, m_tiles = n_out // tn, m // tm
    ko, no = w_out.shape
    slab = ko // (n_tiles * m_tiles)
    if w_is_nk:
        w_block, w_spec = (tn, k), pl.BlockSpec((tn, k), lambda n, i: (n, 0))
    else:
        w_block, w_spec = (k, tn), pl.BlockSpec((k, tn), lambda n, i: (0, n))
    return pl.pallas_call(
        functools.partial(_inproj_kernel, w_is_nk=w_is_nk),
        grid=(n_tiles, m_tiles),
        in_specs=[
            pl.BlockSpec((tm, k), lambda n, i: (i, 0)),
            w_spec,
            pl.BlockSpec((1, tn), lambda n, i: (0, n)),
            pl.BlockSpec((slab, no), lambda n, i: (n * m_tiles + i, 0)),
        ],
        out_specs=[pl.BlockSpec((tm, tn), lambda n, i: (i, n)),
                   pl.BlockSpec((slab, no), lambda n, i: (n * m_tiles + i, 0))],
        out_shape=[jax.ShapeDtypeStruct((m, n_out), BF16),
                   jax.ShapeDtypeStruct((ko, no), BF16)],
        scratch_shapes=[pltpu.VMEM(w_block, BF16)],
        compiler_params=pltpu.CompilerParams(
            dimension_semantics=("arbitrary", "arbitrary"),
            vmem_limit_bytes=VMEM_LIMIT),
        name="in_proj",
    )(x, w, col_scale, w_out)


LN_CHUNKS = 4


def _data_zero(x):
    rows, n = x.shape
    x = jnp.max(x.reshape(rows // 8, 8, n), axis=0)
    bits = pltpu.bitcast(_fold_lanes([x], jnp.maximum), jnp.uint32)
    zero = lax.shift_right_logical(lax.shift_right_logical(bits, jnp.uint32(16)), jnp.uint32(16))
    return pltpu.bitcast(zero, F32)[0:1, 0:1]


def _outproj_ln_kernel(a_ref, w_ref, h_ref, g_ref, b_ref, o_ref, *rest, nblk):
    obf_ref = rest[0] if len(rest) == 3 else None
    y0_ref, y1_ref = rest[-2:]
    i = pl.program_id(0)
    ys = (y0_ref, y1_ref)
    tm, n = o_ref.shape
    rows, cols = tm // LN_CHUNKS, n // LN_CHUNKS

    def layer_norm(y_ref, c):
        rs = slice(c * rows, (c + 1) * rows)
        r = DN_ALPHA * h_ref[rs, :] + y_ref[rs, :]
        mu = jnp.mean(r, axis=1, keepdims=True)
        d = r - mu
        var = jnp.mean(d * d, axis=1, keepdims=True)
        out = d * lax.rsqrt(var + LN_EPS) * g_ref[...] + b_ref[...]
        o_ref[rs, :] = out
        if obf_ref is not None:
            obf_ref[rs, :] = out.astype(BF16)
        return out

    def matmul(y_ref, c, anchor=None):
        cs = slice(c * cols, (c + 1) * cols)
        if anchor is None:
            y = jnp.dot(a_ref[...], w_ref[:, cs], preferred_element_type=F32)
        else:
            head = a_ref[:, 0:HEAD_DIM] + anchor.astype(BF16)
            y = (jnp.dot(head, w_ref[0:HEAD_DIM, cs], preferred_element_type=F32)
                 + jnp.dot(a_ref[:, HEAD_DIM:], w_ref[HEAD_DIM:, cs], preferred_element_type=F32))
        y_ref[:, cs] = y

    @pl.when(i == 0)
    def _():
        for c in range(LN_CHUNKS):
            matmul(ys[0], c)

    for parity in range(2):
        @pl.when(jnp.logical_and(jnp.logical_and(i > 0, i < nblk), i % 2 == parity))
        def _():
            zero = None
            for c in range(LN_CHUNKS):
                matmul(ys[parity], c, anchor=zero)
                zero = _data_zero(layer_norm(ys[1 - parity], c))

    @pl.when(i == nblk)
    def _():
        for c in range(LN_CHUNKS):
            layer_norm(ys[(nblk - 1) % 2], c)


def _out_proj_ln(a_bf, w_bf, h, g, b, emit_bf16, tm=512):
    m, k = a_bf.shape
    n = w_bf.shape[1]
    nblk = m // tm
    cur = lambda i: (jnp.minimum(i, nblk - 1), 0)
    prev = lambda i: (jnp.maximum(i - 1, 0), 0)
    n_out = 2 if emit_bf16 else 1
    return pl.pallas_call(
        functools.partial(_outproj_ln_kernel, nblk=nblk),
        grid=(nblk + 1,),
        in_specs=[
            pl.BlockSpec((tm, k), cur),
            pl.BlockSpec((k, n), lambda i: (0, 0)),
            pl.BlockSpec((tm, n), prev),
            pl.BlockSpec((1, n), lambda i: (0, 0)),
            pl.BlockSpec((1, n), lambda i: (0, 0)),
        ],
        out_specs=[pl.BlockSpec((tm, n), prev)] * n_out,
        out_shape=[jax.ShapeDtypeStruct((m, n), F32), jax.ShapeDtypeStruct((m, n), BF16)][:n_out],
        scratch_shapes=[pltpu.VMEM((tm, n), F32), pltpu.VMEM((tm, n), F32)],
        compiler_params=pltpu.CompilerParams(
            dimension_semantics=("arbitrary",),
            vmem_limit_bytes=VMEM_LIMIT),
        name="out_proj_ln",
    )(a_bf, w_bf, h, g.reshape(1, n), b.reshape(1, n))


BIAS_SPAN = 4 * TQ


def _bucket_profile():
    n = np.maximum(2 * TQ - np.arange(BIAS_SPAN), 0)
    max_exact = REL_BUCKETS // 2
    nf = np.maximum(n, 1).astype(np.float32)
    large = max_exact + (np.log(nf / max_exact) / math.log(REL_MAX_DIST / max_exact)
                         * (REL_BUCKETS - max_exact)).astype(np.int32)
    large = np.minimum(large, REL_BUCKETS - 1)
    return np.where(n < max_exact, n, large).astype(np.int32).reshape(1, BIAS_SPAN)


def _bias_kernel(tab_ref, bkt_ref, o_ref):
    c = pl.program_id(0)
    far = tab_ref[c, REL_BUCKETS - 1]
    bk = bkt_ref[...]
    acc = jnp.zeros((1, BIAS_SPAN), F32)
    for j in range(REL_BUCKETS):
        acc = jnp.where(bk == j, tab_ref[c, j], acc)
    lane = lax.broadcasted_iota(jnp.int32, (1, BIAS_SPAN), 1)
    prof = jnp.where(lane > 2 * TQ, NEG, (acc - far) * LOG2E)
    rolled = pltpu.roll(jnp.broadcast_to(prof, (TQ, BIAS_SPAN)), 0, 1, stride=1, stride_axis=0)
    o_ref[0, 0] = rolled[:, 2 * TQ:3 * TQ]
    o_ref[0, 1] = rolled[:, TQ:2 * TQ]


def _bias_tiles(rel_bias):
    tab = rel_bias.T
    bkt = jnp.asarray(_bucket_profile())
    return pl.pallas_call(
        _bias_kernel,
        grid=(REL_HEADS,),
        in_specs=[
            pl.BlockSpec(memory_space=pltpu.SMEM),
            pl.BlockSpec((1, BIAS_SPAN), lambda c: (0, 0)),
        ],
        out_specs=pl.BlockSpec((1, 2, TQ, TQ), lambda c: (c, 0, 0, 0)),
        out_shape=jax.ShapeDtypeStruct((REL_HEADS, 2, TQ, TQ), F32),
        name="rel_bias_tiles",
    )(tab, bkt)


def _fold_lanes(parts, op):
    acc = None
    for p in parts:
        for t in range(p.shape[1] // HEAD_DIM):
            blk = p[:, t * HEAD_DIM:(t + 1) * HEAD_DIM]
            acc = blk if acc is None else op(acc, blk)
    return acc


def _softmax_pv(parts, v, row_shift=None):
    assert v.shape[1] == HEAD_DIM
    m = _fold_lanes(parts, jnp.maximum).max(axis=1, keepdims=True)
    if row_shift is None:
        shift = -m
    else:
        m_full = m + row_shift
        shift = row_shift - m_full
    ps = [jnp.exp2(p + shift) for p in parts]
    pb = jnp.concatenate([p.astype(BF16) for p in ps], axis=1) if len(ps) > 1 else ps[0].astype(BF16)
    v1 = jnp.concatenate([v, jnp.ones(v.shape, v.dtype)], axis=1)
    acc = jnp.dot(pb, v1, preferred_element_type=F32)
    return acc[:, 0:HEAD_DIM] / acc[:, HEAD_DIM:HEAD_DIM + 1]


def _diff_kernel(lamv_ref, g_ref, q_ref, k_ref, v_ref, z_ref, bias_ref, o_ref, *, lam_init, seq):
    lv = lamv_ref[...]
    s1 = jnp.sum(lv[0:1] * lv[1:2], axis=1, keepdims=True)
    s2 = jnp.sum(lv[2:3] * lv[3:4], axis=1, keepdims=True)
    lam = jnp.exp(s1) - jnp.exp(s2) + lam_init

    def logits(qi, j):
        r0 = qi * TQ
        c0 = j * HEAD_DIM
        q = q_ref[0, r0:r0 + TQ, c0:c0 + HEAD_DIM]
        parts = []
        if qi >= 2:
            parts.append(_dot_nt(q, k_ref[0, 0:r0 - TQ, c0:c0 + HEAD_DIM]))
        if qi >= 1:
            parts.append(_dot_nt(q, k_ref[0, r0 - TQ:r0, c0:c0 + HEAD_DIM]) + bias_ref[j, 1])
        parts.append(_dot_nt(q, k_ref[0, r0:r0 + TQ, c0:c0 + HEAD_DIM]) + bias_ref[j, 0])
        return parts

    def probs(parts):
        m = _fold_lanes(parts, jnp.maximum).max(axis=1, keepdims=True)
        ps = [jnp.exp2(p - m) for p in parts]
        l = _fold_lanes(ps, jnp.add).sum(axis=1, keepdims=True)
        return jnp.concatenate([p.astype(BF16) for p in ps], axis=1), l

    order = list(reversed(range(seq // TQ)))
    queue = [(logits(qi, 0), logits(qi, 1)) for qi in order[:DIFF_BLOCKS_AHEAD]]
    for t, qi in enumerate(order):
        r0 = qi * TQ
        parts0, parts1 = queue.pop(0)
        if t + DIFF_BLOCKS_AHEAD < len(order):
            nq = order[t + DIFF_BLOCKS_AHEAD]
            queue.append((logits(nq, 0), logits(nq, 1)))
        pb0, l0 = probs(parts0)
        pb1, l1 = probs(parts1)
        acc = jnp.dot(jnp.concatenate([pb0, pb1], axis=0), v_ref[0, 0:r0 + TQ, :], preferred_element_type=F32)
        o = acc[0:TQ] / l0 - lam * (acc[TQ:2 * TQ] / l1)
        ms = jnp.mean(o * o, axis=1, keepdims=True)
        o = o * lax.rsqrt(ms + RMS_EPS) * g_ref[...] * (1.0 - lam_init)
        z = z_ref[0, r0:r0 + TQ, :].astype(F32)
        o_ref[0, r0:r0 + TQ, :] = (o * _silu(z)).astype(BF16)


def _diff_attention(proj, lamv, subln_g, bias, lam_init):
    b, s, _ = proj.shape
    w = 2 * HEAD_DIM
    nh = D_INNER // w
    return pl.pallas_call(
        functools.partial(_diff_kernel, lam_init=lam_init, seq=s),
        grid=(b, DIFF_HEADS),
        in_specs=[
            pl.BlockSpec((4, HEAD_DIM), lambda i, h: (0, 0)),
            pl.BlockSpec((1, w), lambda i, h: (0, 0)),
            pl.BlockSpec((1, s, w), lambda i, h: (i, 0, h)),
            pl.BlockSpec((1, s, w), lambda i, h: (i, 0, nh + h)),
            pl.BlockSpec((1, s, w), lambda i, h: (i, 0, 2 * nh + h)),
            pl.BlockSpec((1, s, w), lambda i, h: (i, 0, 3 * nh + h)),
            pl.BlockSpec((2, 2, TQ, TQ), lambda i, h: (h, 0, 0, 0)),
        ],
        out_specs=pl.BlockSpec((1, s, w), lambda i, h: (i, 0, h)),
        out_shape=jax.ShapeDtypeStruct((b, s, D_INNER), BF16),
        compiler_params=pltpu.CompilerParams(
            dimension_semantics=("arbitrary", "arbitrary"),
            vmem_limit_bytes=VMEM_LIMIT),
        name="diff_attention",
    )(lamv, subln_g.reshape(1, w), proj, proj, proj, proj, bias)


def _block_indicator(seq):
    n_kb = seq // MOBA_BLOCK
    ind = np.zeros((HEAD_DIM, seq), np.float32)
    for j in range(n_kb):
        ind[j, j * MOBA_BLOCK:(j + 1) * MOBA_BLOCK] = 1.0
    return ind


HEADS_PER_STEP = 2


def _moba_kernel(ind_ref, kind_ref, q_ref, k_ref, v_ref, z_ref, bias_ref, o_ref, *, seq):
    n_kb = seq // MOBA_BLOCK
    sub = lax.broadcasted_iota(jnp.int32, (n_kb, TQ), 0)

    def block_means(hd):
        c0 = hd * HEAD_DIM
        ksum = jnp.dot(ind_ref[...], k_ref[0, :, c0:c0 + HEAD_DIM], preferred_element_type=F32)
        kmean = ksum * (1.0 / MOBA_BLOCK)
        hi = kmean.astype(BF16)
        lo = (kmean - hi.astype(F32)).astype(BF16)
        return jnp.concatenate([hi, lo], axis=1)

    km2 = [block_means(hd) for hd in range(HEADS_PER_STEP)]

    def block_mask(hd, qi, q):
        gate = _dot_nt(km2[hd], jnp.concatenate([q, q], axis=1))[0:n_kb]
        rank = jnp.zeros((n_kb, TQ), F32)
        for bp in range(qi):
            gb = gate[bp:bp + 1, :]
            rank = rank + jnp.where(sub > bp, jnp.where(gb >= gate, 1.0, 0.0), jnp.where(gb > gate, 1.0, 0.0))
        mt = jnp.where(sub >= qi, 0.0, jnp.where(rank < float(MOBA_TOPK), 0.0, NEG))
        mt = jnp.concatenate([mt, jnp.zeros((HEAD_DIM - n_kb, TQ), F32)], axis=0)
        return mt.T.astype(BF16)

    masks = {(hd, qi): block_mask(hd, qi, q_ref[0, qi * TQ:(qi + 1) * TQ, hd * HEAD_DIM:(hd + 1) * HEAD_DIM])
             for hd in range(HEADS_PER_STEP) for qi in reversed(range(MOBA_TOPK + 1, n_kb))}

    def logits(hd, qi):
        r0 = qi * TQ
        c0 = hd * HEAD_DIM
        q = q_ref[0, r0:r0 + TQ, c0:c0 + HEAD_DIM]
        masked = qi > MOBA_TOPK
        if masked:
            q = jnp.concatenate([q, masks[hd, qi]], axis=1)
        parts = []
        for j in range(qi + 1):
            kj = k_ref[0, j * TQ:(j + 1) * TQ, c0:c0 + HEAD_DIM]
            if masked:
                kj = jnp.concatenate([kj, kind_ref[j * TQ:(j + 1) * TQ, :]], axis=1)
            sj = _dot_nt(q, kj)
            if j >= qi - 1:
                sj = sj + bias_ref[hd, qi - j]
            parts.append(sj)
        return parts

    order = list(reversed(range(n_kb)))
    tasks = [(hd, qi) for hd in range(HEADS_PER_STEP) for qi in order]
    queue = [logits(*task) for task in tasks[:MOBA_TASKS_AHEAD]]
    for t, (hd, qi) in enumerate(tasks):
        r0 = qi * TQ
        c0 = hd * HEAD_DIM
        parts = queue.pop(0)
        if t + MOBA_TASKS_AHEAD < len(tasks):
            queue.append(logits(*tasks[t + MOBA_TASKS_AHEAD]))
        o = _softmax_pv(parts, v_ref[0, 0:r0 + TQ, c0:c0 + HEAD_DIM])
        z = z_ref[0, r0:r0 + TQ, c0:c0 + HEAD_DIM].astype(F32)
        o_ref[0, r0:r0 + TQ, c0:c0 + HEAD_DIM] = (o * _silu(z)).astype(BF16)


def _head_specs(s, nh):
    w = HEADS_PER_STEP * HEAD_DIM
    ng = nh // HEADS_PER_STEP
    return [
        pl.BlockSpec((1, s, w), lambda i, h: (i, 0, h)),
        pl.BlockSpec((1, s, w), lambda i, h: (i, 0, ng + h)),
        pl.BlockSpec((1, s, w), lambda i, h: (i, 0, 2 * ng + h)),
        pl.BlockSpec((1, s, w), lambda i, h: (i, 0, 3 * ng + h)),
    ]


def _moba_attention(proj, bias):
    b, s, _ = proj.shape
    assert s // MOBA_BLOCK == 8 and MOBA_BLOCK == TQ
    ind = _block_indicator(s)
    return pl.pallas_call(
        functools.partial(_moba_kernel, seq=s),
        grid=(b, MOBA_HEADS // HEADS_PER_STEP),
        in_specs=[pl.BlockSpec((HEAD_DIM, s), lambda i, h: (0, 0)),
                  pl.BlockSpec((s, HEAD_DIM), lambda i, h: (0, 0))]
        + _head_specs(s, MOBA_HEADS)
        + [pl.BlockSpec((HEADS_PER_STEP, 2, TQ, TQ), lambda i, h: (h, 0, 0, 0))],
        out_specs=pl.BlockSpec((1, s, HEADS_PER_STEP * HEAD_DIM), lambda i, h: (i, 0, h)),
        out_shape=jax.ShapeDtypeStruct((b, s, D_INNER), BF16),
        compiler_params=pltpu.CompilerParams(
            dimension_semantics=("arbitrary", "arbitrary"),
            vmem_limit_bytes=VMEM_LIMIT),
        name="moba_attention",
    )(jnp.asarray(ind, BF16), jnp.asarray(ind.T, BF16), proj, proj, proj, proj, bias)


def _fox_gate_kernel(x_ref, wf_ref, bf_ref, row_ref, col_ref, *, seq):
    f = _dot_nt(x_ref[0], wf_ref[...]) + bf_ref[...]
    c = jnp.minimum(f, 0.0) - jnp.log1p(jnp.exp(-jnp.abs(f)))
    c = c.T[0:FOX_HEADS]
    lane = lax.broadcasted_iota(jnp.int32, c.shape, 1)
    d = 1
    while d < seq:
        c = c + jnp.where(lane >= d, pltpu.roll(c, d, 1), 0.0)
        d *= 2
    c = c * LOG2E
    row_ref[0] = c
    col_ref[0] = jnp.concatenate([c, jnp.zeros((HEAD_DIM - FOX_HEADS, seq), F32)], axis=0).T


def _fox_gate(h_bf3, wf, bf_row):
    b, s, d = h_bf3.shape
    return pl.pallas_call(
        functools.partial(_fox_gate_kernel, seq=s),
        grid=(b,),
        in_specs=[
            pl.BlockSpec((1, s, d), lambda i: (i, 0, 0)),
            pl.BlockSpec((HEAD_DIM, d), lambda i: (0, 0)),
            pl.BlockSpec((1, HEAD_DIM), lambda i: (0, 0)),
        ],
        out_specs=[pl.BlockSpec((1, FOX_HEADS, s), lambda i: (i, 0, 0)),
                   pl.BlockSpec((1, s, HEAD_DIM), lambda i: (i, 0, 0))],
        out_shape=[jax.ShapeDtypeStruct((b, FOX_HEADS, s), F32),
                   jax.ShapeDtypeStruct((b, s, HEAD_DIM), F32)],
        compiler_params=pltpu.CompilerParams(
            dimension_semantics=("arbitrary",),
            vmem_limit_bytes=VMEM_LIMIT),
        name="fox_gate",
    )(h_bf3, wf, bf_row)


def _causal_tile():
    r = np.arange(TQ)[:, None]
    c = np.arange(TQ)[None, :]
    return np.where(c <= r, 0.0, NEG).astype(np.float32)


def _fox_kernel(mask_ref, crow_ref, ccol_ref, q_ref, k_ref, v_ref, z_ref, o_ref, *, seq):
    def logits(hd, qi):
        r0 = qi * TQ
        c0 = hd * HEAD_DIM
        q = q_ref[0, r0:r0 + TQ, c0:c0 + HEAD_DIM]
        parts = []
        if qi >= 1:
            parts.append(_dot_nt(q, k_ref[0, 0:r0, c0:c0 + HEAD_DIM]) - crow_ref[0, hd, :, 0:r0])
        parts.append(_dot_nt(q, k_ref[0, r0:r0 + TQ, c0:c0 + HEAD_DIM]) - crow_ref[0, hd, :, r0:r0 + TQ]
                     + mask_ref[...])
        return parts

    nq = seq // TQ
    lane = lax.broadcasted_iota(jnp.int32, (TQ, HEAD_DIM), 1)
    tasks = [(hd, qi) for hd in range(HEADS_PER_STEP) for qi in reversed(range(nq))]
    nxt = logits(*tasks[0])
    for t, (hd, qi) in enumerate(tasks):
        r0 = qi * TQ
        c0 = hd * HEAD_DIM
        parts = nxt
        if t + 1 < len(tasks):
            nxt = logits(*tasks[t + 1])
        head_lane = lane == pl.program_id(1) * HEADS_PER_STEP + hd
        c_t = jnp.sum(jnp.where(head_lane, ccol_ref[0, r0:r0 + TQ, :], 0.0), axis=1, keepdims=True)
        o = _softmax_pv(parts, v_ref[0, 0:r0 + TQ, c0:c0 + HEAD_DIM], row_shift=c_t)
        z = z_ref[0, r0:r0 + TQ, c0:c0 + HEAD_DIM].astype(F32)
        o_ref[0, r0:r0 + TQ, c0:c0 + HEAD_DIM] = (o * _silu(z)).astype(BF16)


def _fox_attention(proj, cum_row, cum_col):
    b, s, _ = proj.shape
    crow = cum_row.reshape(b, FOX_HEADS, 1, s)
    mask = jnp.asarray(_causal_tile())
    return pl.pallas_call(
        functools.partial(_fox_kernel, seq=s),
        grid=(b, FOX_HEADS // HEADS_PER_STEP),
        in_specs=[
            pl.BlockSpec((TQ, TQ), lambda i, h: (0, 0)),
            pl.BlockSpec((1, HEADS_PER_STEP, 1, s), lambda i, h: (i, h, 0, 0)),
            pl.BlockSpec((1, s, HEAD_DIM), lambda i, h: (i, 0, 0)),
        ] + _head_specs(s, FOX_HEADS),
        out_specs=pl.BlockSpec((1, s, HEADS_PER_STEP * HEAD_DIM), lambda i, h: (i, 0, h)),
        out_shape=jax.ShapeDtypeStruct((b, s, D_INNER), BF16),
        compiler_params=pltpu.CompilerParams(
            dimension_semantics=("arbitrary", "arbitrary"),
            vmem_limit_bytes=VMEM_LIMIT),
        name="fox_attention",
    )(mask, crow, cum_col, proj, proj, proj, proj)


def _col_scale():
    cs = np.ones((1, 4 * D_INNER), np.float32)
    cs[0, :D_INNER] = QK_SCALE * LOG2E
    return jnp.asarray(cs)


def kernel(x, rel_bias, w_in_0, lam_q1_0, lam_k1_0, lam_q2_0, lam_k2_0, subln_g_0, w_out_0, ln_g_0, ln_b_0, w_in_1, w_out_1, ln_g_1, ln_b_1, w_in_2, b_f_2, w_out_2, ln_g_2, ln_b_2, w_in_3, lam_q1_3, lam_k1_3, lam_q2_3, lam_k2_3, subln_g_3, w_out_3, ln_g_3, ln_b_3):
    b, s, d = x.shape
    m = b * s
    e = D_INNER
    col_scale = _col_scale()
    bias = _bias_tiles(rel_bias)

    h = x.reshape(m, d)
    h_bf = h
    layers = [
        (w_in_0, w_out_0, ln_g_0, ln_b_0),
        (w_in_1, w_out_1, ln_g_1, ln_b_1),
        (w_in_2, w_out_2, ln_g_2, ln_b_2),
        (w_in_3, w_out_3, ln_g_3, ln_b_3),
    ]
    diff_extra = {
        0: (lam_q1_0, lam_k1_0, lam_q2_0, lam_k2_0, subln_g_0),
        3: (lam_q1_3, lam_k1_3, lam_q2_3, lam_k2_3, subln_g_3),
    }
    for i, (w_in, w_out, g, beta) in enumerate(layers):
        kind = i % 3
        if kind == 2:
            w_t = w_in.T
            proj, w_out_bf = _in_proj(h_bf, w_t, col_scale, 4 * e, w_out, w_is_nk=True)
        else:
            proj, w_out_bf = _in_proj(h_bf, w_in, col_scale, 4 * e, w_out)
        proj = proj.reshape(b, s, 4 * e)
        if kind == 0:
            lq1, lk1, lq2, lk2, sg = diff_extra[i]
            lam_init = 0.8 - 0.6 * math.exp(-0.3 * i)
            a = _diff_attention(proj, jnp.stack([lq1, lk1, lq2, lk2]), sg, bias, lam_init)
        elif kind == 1:
            a = _moba_attention(proj, bias)
        else:
            wf = jnp.pad(w_t[4 * e:].astype(BF16), ((0, HEAD_DIM - FOX_HEADS), (0, 0)))
            bf_row = jnp.pad(b_f_2, (0, HEAD_DIM - FOX_HEADS)).reshape(1, HEAD_DIM)
            cum_row, cum_col = _fox_gate(h_bf.reshape(b, s, d), wf, bf_row)
            a = _fox_attention(proj, cum_row, cum_col)
        outs = _out_proj_ln(a.reshape(m, e), w_out_bf, h, g, beta, emit_bf16=i + 1 < DEPTH)
        h, h_bf = outs[0], outs[-1]
    return h.reshape(b, s, d)
```

```python
import functools
import math

import jax
import jax.numpy as jnp
import numpy as np
from jax import lax
from jax.experimental import pallas as pl
from jax.experimental.pallas import tpu as pltpu

D_MODEL = 2048
D_INNER = 2048
DEPTH = 4
DIFF_HEADS = 8
MOBA_HEADS = 16
MOBA_BLOCK = 256
MOBA_TOPK = 3
FOX_HEADS = 16
HEAD_DIM = 128
REL_BUCKETS = 32
REL_MAX_DIST = 128
REL_HEADS = 16
LN_EPS = 1e-5
RMS_EPS = 1e-5
NEG = -1e30
DN_ALPHA = (2.0 * DEPTH) ** 0.25
LOG2E = math.log2(math.e)
QK_SCALE = HEAD_DIM ** -0.5

TQ = 256
DIFF_BLOCKS_AHEAD = 2
MOBA_TASKS_AHEAD = 2
VMEM_LIMIT = 60 * 1024 * 1024

F32 = jnp.float32
BF16 = jnp.bfloat16


def _dot_nt(a, b):
    return lax.dot_general(a, b, (((1,), (1,)), ((), ())), preferred_element_type=F32)


def _silu(z):
    return z / (1.0 + jnp.exp(-z))


def _inproj_kernel(x_ref, w_ref, cs_ref, wo_ref, o_ref, wo_bf_ref, wbf_ref, *, w_is_nk):
    @pl.when(pl.program_id(1) == 0)
    def _():
        wbf_ref[...] = w_ref[...].astype(BF16)

    x = x_ref[...].astype(BF16)
    if w_is_nk:
        acc = _dot_nt(x, wbf_ref[...])
    else:
        acc = jnp.dot(x, wbf_ref[...], preferred_element_type=F32)
    o_ref[...] = (acc * cs_ref[...]).astype(o_ref.dtype)
    wo_bf_ref[...] = wo_ref[...].astype(BF16)


def _in_proj(x, w, col_scale, n_out, w_out, w_is_nk=False):
    m, k = x.shape
    tm, tn = (2048, 1024) if x.dtype == BF16 else (512, 2048)
    n_tiles, m_tiles = n_out // tn, m // tm
    ko, no = w_out.shape
    slab = ko // (n_tiles * m_tiles)
    if w_is_nk:
        w_block, w_spec = (tn, k), pl.BlockSpec((tn, k), lambda n, i: (n, 0))
    else:
        w_block, w_spec = (k, tn), pl.BlockSpec((k, tn), lambda n, i: (0, n))
    return pl.pallas_call(
        functools.partial(_inproj_kernel, w_is_nk=w_is_nk),
        grid=(n_tiles, m_tiles),
        in_specs=[
            pl.BlockSpec((tm, k), lambda n, i: (i, 0)),
            w_spec,
            pl.BlockSpec((1, tn), lambda n, i: (0, n)),
            pl.BlockSpec((slab, no), lambda n, i: (n * m_tiles + i, 0)),
        ],
        out_specs=[pl.BlockSpec((tm, tn), lambda n, i: (i, n)),
                   pl.BlockSpec((slab, no), lambda n, i: (n * m_tiles + i, 0))],
        out_shape=[jax.ShapeDtypeStruct((m, n_out), BF16),
                   jax.ShapeDtypeStruct((ko, no), BF16)],
        scratch_shapes=[pltpu.VMEM(w_block, BF16)],
        compiler_params=pltpu.CompilerParams(
            dimension_semantics=("arbitrary", "arbitrary"),
            vmem_limit_bytes=VMEM_LIMIT),
        name="in_proj",
    )(x, w, col_scale, w_out)


LN_CHUNKS = 4


def _data_zero(x):
    rows, n = x.shape
    x = jnp.max(x.reshape(rows // 8, 8, n), axis=0)
    bits = pltpu.bitcast(_fold_lanes([x], jnp.maximum), jnp.uint32)
    zero = lax.shift_right_logical(lax.shift_right_logical(bits, jnp.uint32(16)), jnp.uint32(16))
    return pltpu.bitcast(zero, F32)[0:1, 0:1]


def _outproj_ln_kernel(a_ref, w_ref, h_ref, g_ref, b_ref, o_ref, *rest, nblk):
    obf_ref = rest[0] if len(rest) == 3 else None
    y0_ref, y1_ref = rest[-2:]
    i = pl.program_id(0)
    ys = (y0_ref, y1_ref)
    tm, n = o_ref.shape
    rows, cols = tm // LN_CHUNKS, n // LN_CHUNKS

    def layer_norm(y_ref, c):
        rs = slice(c * rows, (c + 1) * rows)
        r = DN_ALPHA * h_ref[rs, :] + y_ref[rs, :]
        mu = jnp.mean(r, axis=1, keepdims=True)
        d = r - mu
        var = jnp.mean(d * d, axis=1, keepdims=True)
        out = d * lax.rsqrt(var + LN_EPS) * g_ref[...] + b_ref[...]
        o_ref[rs, :] = out
        if obf_ref is not None:
            obf_ref[rs, :] = out.astype(BF16)
        return out

    def matmul(y_ref, c, anchor=None):
        cs = slice(c * cols, (c + 1) * cols)
        if anchor is None:
            y = jnp.dot(a_ref[...], w_ref[:, cs], preferred_element_type=F32)
        else:
            head = a_ref[:, 0:HEAD_DIM] + anchor.astype(BF16)
            y = (jnp.dot(head, w_ref[0:HEAD_DIM, cs], preferred_element_type=F32)
                 + jnp.dot(a_ref[:, HEAD_DIM:], w_ref[HEAD_DIM:, cs], preferred_element_type=F32))
        y_ref[:, cs] = y

    @pl.when(i == 0)
    def _():
        for c in range(LN_CHUNKS):
            matmul(ys[0], c)

    for parity in range(2):
        @pl.when(jnp.logical_and(jnp.logical_and(i > 0, i < nblk), i % 2 == parity))
        def _():
            zero = None
            for c in range(LN_CHUNKS):
                matmul(ys[parity], c, anchor=zero)
                zero = _data_zero(layer_norm(ys[1 - parity], c))

    @pl.when(i == nblk)
    def _():
        for c in range(LN_CHUNKS):
            layer_norm(ys[(nblk - 1) % 2], c)


def _out_proj_ln(a_bf, w_bf, h, g, b, emit_bf16, tm=512):
    m, k = a_bf.shape
    n = w_bf.shape[1]
    nblk = m // tm
    cur = lambda i: (jnp.minimum(i, nblk - 1), 0)
    prev = lambda i: (jnp.maximum(i - 1, 0), 0)
    n_out = 2 if emit_bf16 else 1
    return pl.pallas_call(
        functools.partial(_outproj_ln_kernel, nblk=nblk),
        grid=(nblk + 1,),
        in_specs=[
            pl.BlockSpec((tm, k), cur),
            pl.BlockSpec((k, n), lambda i: (0, 0)),
            pl.BlockSpec((tm, n), prev),
            pl.BlockSpec((1, n), lambda i: (0, 0)),
            pl.BlockSpec((1, n), lambda i: (0, 0)),
        ],
        out_specs=[pl.BlockSpec((tm, n), prev)] * n_out,
        out_shape=[jax.ShapeDtypeStruct((m, n), F32), jax.ShapeDtypeStruct((m, n), BF16)][:n_out],
        scratch_shapes=[pltpu.VMEM((tm, n), F32), pltpu.VMEM((tm, n), F32)],
        compiler_params=pltpu.CompilerParams(
            dimension_semantics=("arbitrary",),
            vmem_limit_bytes=VMEM_LIMIT),
        name="out_proj_ln",
    )(a_bf, w_bf, h, g.reshape(1, n), b.reshape(1, n))


BIAS_SPAN = 4 * TQ


def _bucket_profile():
    n = np.maximum(2 * TQ - np.arange(BIAS_SPAN), 0)
    max_exact = REL_BUCKETS // 2
    nf = np.maximum(n, 1).astype(np.float32)
    large = max_exact + (np.log(nf / max_exact) / math.log(REL_MAX_DIST / max_exact)
                         * (REL_BUCKETS - max_exact)).astype(np.int32)
    large = np.minimum(large, REL_BUCKETS - 1)
    return np.where(n < max_exact, n, large).astype(np.int32).reshape(1, BIAS_SPAN)


def _bias_kernel(tab_ref, bkt_ref, o_ref):
    c = pl.program_id(0)
    far = tab_ref[c, REL_BUCKETS - 1]
    bk = bkt_ref[...]
    acc = jnp.zeros((1, BIAS_SPAN), F32)
    for j in range(REL_BUCKETS):
        acc = jnp.where(bk == j, tab_ref[c, j], acc)
    lane = lax.broadcasted_iota(jnp.int32, (1, BIAS_SPAN), 1)
    prof = jnp.where(lane > 2 * TQ, NEG, (acc - far) * LOG2E)
    rolled = pltpu.roll(jnp.broadcast_to(prof, (TQ, BIAS_SPAN)), 0, 1, stride=1, stride_axis=0)
    o_ref[0, 0] = rolled[:, 2 * TQ:3 * TQ]
    o_ref[0, 1] = rolled[:, TQ:2 * TQ]


def _bias_tiles(rel_bias):
    tab = rel_bias.T
    bkt = jnp.asarray(_bucket_profile())
    return pl.pallas_call(
        _bias_kernel,
        grid=(REL_HEADS,),
        in_specs=[
            pl.BlockSpec(memory_space=pltpu.SMEM),
            pl.BlockSpec((1, BIAS_SPAN), lambda c: (0, 0)),
        ],
        out_specs=pl.BlockSpec((1, 2, TQ, TQ), lambda c: (c, 0, 0, 0)),
        out_shape=jax.ShapeDtypeStruct((REL_HEADS, 2, TQ, TQ), F32),
        name="rel_bias_tiles",
    )(tab, bkt)


def _fold_lanes(parts, op):
    acc = None
    for p in parts:
        for t in range(p.shape[1] // HEAD_DIM):
            blk = p[:, t * HEAD_DIM:(t + 1) * HEAD_DIM]
            acc = blk if acc is None else op(acc, blk)
    return acc


def _softmax_pv(parts, v, row_shift=None):
    assert v.shape[1] == HEAD_DIM
    m = _fold_lanes(parts, jnp.maximum).max(axis=1, keepdims=True)
    if row_shift is None:
        shift = -m
    else:
        m_full = m + row_shift
        shift = row_shift - m_full
    ps = [jnp.exp2(p + shift) for p in parts]
    pb = jnp.concatenate([p.astype(BF16) for p in ps], axis=1) if len(ps) > 1 else ps[0].astype(BF16)
    v1 = jnp.concatenate([v, jnp.ones(v.shape, v.dtype)], axis=1)
    acc = jnp.dot(pb, v1, preferred_element_type=F32)
    return acc[:, 0:HEAD_DIM] / acc[:, HEAD_DIM:HEAD_DIM + 1]


def _diff_kernel(lamv_ref, g_ref, q_ref, k_ref, v_ref, z_ref, bias_ref, o_ref, *, lam_init, seq):
    lv = lamv_ref[...]
    s1 = jnp.sum(lv[0:1] * lv[1:2], axis=1, keepdims=True)
    s2 = jnp.sum(lv[2:3] * lv[3:4], axis=1, keepdims=True)
    lam = jnp.exp(s1) - jnp.exp(s2) + lam_init

    def logits(qi, j):
        r0 = qi * TQ
        c0 = j * HEAD_DIM
        q = q_ref[0, r0:r0 + TQ, c0:c0 + HEAD_DIM]
        parts = []
        if qi >= 2:
            parts.append(_dot_nt(q, k_ref[0, 0:r0 - TQ, c0:c0 + HEAD_DIM]))
        if qi >= 1:
            parts.append(_dot_nt(q, k_ref[0, r0 - TQ:r0, c0:c0 + HEAD_DIM]) + bias_ref[j, 1])
        parts.append(_dot_nt(q, k_ref[0, r0:r0 + TQ, c0:c0 + HEAD_DIM]) + bias_ref[j, 0])
        return parts

    def probs(parts):
        m = _fold_lanes(parts, jnp.maximum).max(axis=1, keepdims=True)
        ps = [jnp.exp2(p - m) for p in parts]
        l = _fold_lanes(ps, jnp.add).sum(axis=1, keepdims=True)
        return jnp.concatenate([p.astype(BF16) for p in ps], axis=1), l

    order = list(reversed(range(seq // TQ)))
    queue = [(logits(qi, 0), logits(qi, 1)) for qi in order[:DIFF_BLOCKS_AHEAD]]
    for t, qi in enumerate(order):
        r0 = qi * TQ
        parts0, parts1 = queue.pop(0)
        if t + DIFF_BLOCKS_AHEAD < len(order):
            nq = order[t + DIFF_BLOCKS_AHEAD]
            queue.append((logits(nq, 0), logits(nq, 1)))
        pb0, l0 = probs(parts0)
        pb1, l1 = probs(parts1)
        acc = jnp.dot(jnp.concatenate([pb0, pb1], axis=0), v_ref[0, 0:r0 + TQ, :], preferred_element_type=F32)
        o = acc[0:TQ] / l0 - lam * (acc[TQ:2 * TQ] / l1)
        ms = jnp.mean(o * o, axis=1, keepdims=True)
        o = o * lax.rsqrt(ms + RMS_EPS) * g_ref[...] * (1.0 - lam_init)
        z = z_ref[0, r0:r0 + TQ, :].astype(F32)
        o_ref[0, r0:r0 + TQ, :] = (o * _silu(z)).astype(BF16)


def _diff_attention(proj, lamv, subln_g, bias, lam_init):
    b, s, _ = proj.shape
    w = 2 * HEAD_DIM
    nh = D_INNER // w
    return pl.pallas_call(
        functools.partial(_diff_kernel, lam_init=lam_init, seq=s),
        grid=(b, DIFF_HEADS),
        in_specs=[
            pl.BlockSpec((4, HEAD_DIM), lambda i, h: (0, 0)),
            pl.BlockSpec((1, w), lambda i, h: (0, 0)),
            pl.BlockSpec((1, s, w), lambda i, h: (i, 0, h)),
            pl.BlockSpec((1, s, w), lambda i, h: (i, 0, nh + h)),
            pl.BlockSpec((1, s, w), lambda i, h: (i, 0, 2 * nh + h)),
            pl.BlockSpec((1, s, w), lambda i, h: (i, 0, 3 * nh + h)),
            pl.BlockSpec((2, 2, TQ, TQ), lambda i, h: (h, 0, 0, 0)),
        ],
        out_specs=pl.BlockSpec((1, s, w), lambda i, h: (i, 0, h)),
        out_shape=jax.ShapeDtypeStruct((b, s, D_INNER), BF16),
        compiler_params=pltpu.CompilerParams(
            dimension_semantics=("arbitrary", "arbitrary"),
            vmem_limit_bytes=VMEM_LIMIT),
        name="diff_attention",
    )(lamv, subln_g.reshape(1, w), proj, proj, proj, proj, bias)


def _block_indicator(seq):
    n_kb = seq // MOBA_BLOCK
    ind = np.zeros((HEAD_DIM, seq), np.float32)
    for j in range(n_kb):
        ind[j, j * MOBA_BLOCK:(j + 1) * MOBA_BLOCK] = 1.0
    return ind


HEADS_PER_STEP = 2


def _moba_kernel(ind_ref, kind_ref, q_ref, k_ref, v_ref, z_ref, bias_ref, o_ref, *, seq):
    n_kb = seq // MOBA_BLOCK
    sub = lax.broadcasted_iota(jnp.int32, (n_kb, TQ), 0)

    def block_means(hd):
        c0 = hd * HEAD_DIM
        ksum = jnp.dot(ind_ref[...], k_ref[0, :, c0:c0 + HEAD_DIM], preferred_element_type=F32)
        kmean = ksum * (1.0 / MOBA_BLOCK)
        hi = kmean.astype(BF16)
        lo = (kmean - hi.astype(F32)).astype(BF16)
        return jnp.concatenate([hi, lo], axis=1)

    km2 = [block_means(hd) for hd in range(HEADS_PER_STEP)]

    def block_mask(hd, qi, q):
        gate = _dot_nt(km2[hd], jnp.concatenate([q, q], axis=1))[0:n_kb]
        rank = jnp.zeros((n_kb, TQ), F32)
        for bp in range(qi):
            gb = gate[bp:bp + 1, :]
            rank = rank + jnp.where(sub > bp, jnp.where(gb >= gate, 1.0, 0.0), jnp.where(gb > gate, 1.0, 0.0))
        mt = jnp.where(sub >= qi, 0.0, jnp.where(rank < float(MOBA_TOPK), 0.0, NEG))
        mt = jnp.concatenate([mt, jnp.zeros((HEAD_DIM - n_kb, TQ), F32)], axis=0)
        return mt.T.astype(BF16)

    masks = {(hd, qi): block_mask(hd, qi, q_ref[0, qi * TQ:(qi + 1) * TQ, hd * HEAD_DIM:(hd + 1) * HEAD_DIM])
             for hd in range(HEADS_PER_STEP) for qi in reversed(range(MOBA_TOPK + 1, n_kb))}

    def logits(hd, qi):
        r0 = qi * TQ
        c0 = hd * HEAD_DIM
        q = q_ref[0, r0:r0 + TQ, c0:c0 + HEAD_DIM]
        masked = qi > MOBA_TOPK
        if masked:
            q = jnp.concatenate([q, masks[hd, qi]], axis=1)
        parts = []
        for j in range(qi + 1):
            kj = k_ref[0, j * TQ:(j + 1) * TQ, c0:c0 + HEAD_DIM]
            if masked:
                kj = jnp.concatenate([kj, kind_ref[j * TQ:(j + 1) * TQ, :]], axis=1)
            sj = _dot_nt(q, kj)
            if j >= qi - 1:
                sj = sj + bias_ref[hd, qi - j]
            parts.append(sj)
        return parts

    order = [MOBA_TOPK] + [qi for qi in reversed(range(n_kb)) if qi != MOBA_TOPK]
    tasks = [(hd, qi) for hd in range(HEADS_PER_STEP) for qi in order]
    queue = [logits(*task) for task in tasks[:MOBA_TASKS_AHEAD]]
    for t, (hd, qi) in enumerate(tasks):
        r0 = qi * TQ
        c0 = hd * HEAD_DIM
        parts = queue.pop(0)
        if t + MOBA_TASKS_AHEAD < len(tasks):
            queue.append(logits(*tasks[t + MOBA_TASKS_AHEAD]))
        o = _softmax_pv(parts, v_ref[0, 0:r0 + TQ, c0:c0 + HEAD_DIM])
        z = z_ref[0, r0:r0 + TQ, c0:c0 + HEAD_DIM].astype(F32)
        o_ref[0, r0:r0 + TQ, c0:c0 + HEAD_DIM] = (o * _silu(z)).astype(BF16)


def _head_specs(s, nh):
    w = HEADS_PER_STEP * HEAD_DIM
    ng = nh // HEADS_PER_STEP
    return [
        pl.BlockSpec((1, s, w), lambda i, h: (i, 0, h)),
        pl.BlockSpec((1, s, w), lambda i, h: (i, 0, ng + h)),
        pl.BlockSpec((1, s, w), lambda i, h: (i, 0, 2 * ng + h)),
        pl.BlockSpec((1, s, w), lambda i, h: (i, 0, 3 * ng + h)),
    ]


def _moba_attention(proj, bias):
    b, s, _ = proj.shape
    assert s // MOBA_BLOCK == 8 and MOBA_BLOCK == TQ
    ind = _block_indicator(s)
    return pl.pallas_call(
        functools.partial(_moba_kernel, seq=s),
        grid=(b, MOBA_HEADS // HEADS_PER_STEP),
        in_specs=[pl.BlockSpec((HEAD_DIM, s), lambda i, h: (0, 0)),
                  pl.BlockSpec((s, HEAD_DIM), lambda i, h: (0, 0))]
        + _head_specs(s, MOBA_HEADS)
        + [pl.BlockSpec((HEADS_PER_STEP, 2, TQ, TQ), lambda i, h: (h, 0, 0, 0))],
        out_specs=pl.BlockSpec((1, s, HEADS_PER_STEP * HEAD_DIM), lambda i, h: (i, 0, h)),
        out_shape=jax.ShapeDtypeStruct((b, s, D_INNER), BF16),
        compiler_params=pltpu.CompilerParams(
            dimension_semantics=("arbitrary", "arbitrary"),
            vmem_limit_bytes=VMEM_LIMIT),
        name="moba_attention",
    )(jnp.asarray(ind, BF16), jnp.asarray(ind.T, BF16), proj, proj, proj, proj, bias)


def _fox_gate_kernel(x_ref, wf_ref, bf_ref, row_ref, col_ref, *, seq):
    f = _dot_nt(x_ref[0], wf_ref[...]) + bf_ref[...]
    c = jnp.minimum(f, 0.0) - jnp.log1p(jnp.exp(-jnp.abs(f)))
    c = c.T[0:FOX_HEADS]
    lane = lax.broadcasted_iota(jnp.int32, c.shape, 1)
    d = 1
    while d < seq:
        c = c + jnp.where(lane >= d, pltpu.roll(c, d, 1), 0.0)
        d *= 2
    c = c * LOG2E
    row_ref[0] = c
    col_ref[0] = jnp.concatenate([c, jnp.zeros((HEAD_DIM - FOX_HEADS, seq), F32)], axis=0).T


def _fox_gate(h_bf3, wf, bf_row):
    b, s, d = h_bf3.shape
    return pl.pallas_call(
        functools.partial(_fox_gate_kernel, seq=s),
        grid=(b,),
        in_specs=[
            pl.BlockSpec((1, s, d), lambda i: (i, 0, 0)),
            pl.BlockSpec((HEAD_DIM, d), lambda i: (0, 0)),
            pl.BlockSpec((1, HEAD_DIM), lambda i: (0, 0)),
        ],
        out_specs=[pl.BlockSpec((1, FOX_HEADS, s), lambda i: (i, 0, 0)),
                   pl.BlockSpec((1, s, HEAD_DIM), lambda i: (i, 0, 0))],
        out_shape=[jax.ShapeDtypeStruct((b, FOX_HEADS, s), F32),
                   jax.ShapeDtypeStruct((b, s, HEAD_DIM), F32)],
        compiler_params=pltpu.CompilerParams(
            dimension_semantics=("arbitrary",),
            vmem_limit_bytes=VMEM_LIMIT),
        name="fox_gate",
    )(h_bf3, wf, bf_row)


def _causal_tile():
    r = np.arange(TQ)[:, None]
    c = np.arange(TQ)[None, :]
    return np.where(c <= r, 0.0, NEG).astype(np.float32)


FOX_TASKS_AHEAD = 2


def _fox_kernel(mask_ref, crow_ref, ccol_ref, q_ref, k_ref, v_ref, z_ref, o_ref, *, seq):
    pieces = 3
    row16 = lax.broadcasted_iota(jnp.int32, (16, HEAD_DIM), 0)
    lane16 = lax.broadcasted_iota(jnp.int32, (16, HEAD_DIM), 1)
    place = jnp.where(jnp.logical_and(row16 == lane16, row16 < pieces), 1.0, 0.0).astype(BF16)
    q_ext = jnp.where(lax.broadcasted_iota(jnp.int32, (TQ, HEAD_DIM), 1) < pieces, -1.0, 0.0).astype(BF16)

    def key_ext(hd):
        c = crow_ref[0, hd]
        hi = c.astype(BF16)
        r1 = c - hi.astype(F32)
        mid = r1.astype(BF16)
        lo = (r1 - mid.astype(F32)).astype(BF16)
        rows = jnp.concatenate([hi, mid, lo, jnp.zeros((16 - pieces, seq), BF16)], axis=0)
        ext = lax.dot_general(rows, place, (((0,), (0,)), ((), ())), preferred_element_type=F32)
        return ext.astype(BF16)

    k_ext = [key_ext(hd) for hd in range(HEADS_PER_STEP)]

    def logits(hd, qi):
        r0 = qi * TQ
        c0 = hd * HEAD_DIM
        q = jnp.concatenate([q_ref[0, r0:r0 + TQ, c0:c0 + HEAD_DIM], q_ext], axis=1)

        def keys(lo, hi):
            return jnp.concatenate([k_ref[0, lo:hi, c0:c0 + HEAD_DIM], k_ext[hd][lo:hi]], axis=1)

        parts = []
        if qi >= 1:
            parts.append(_dot_nt(q, keys(0, r0)))
        parts.append(_dot_nt(q, keys(r0, r0 + TQ)) + mask_ref[...])
        return parts

    nq = seq // TQ
    lane = lax.broadcasted_iota(jnp.int32, (TQ, HEAD_DIM), 1)
    tasks = [(hd, qi) for hd in range(HEADS_PER_STEP) for qi in reversed(range(nq))]
    queue = [logits(*task) for task in tasks[:FOX_TASKS_AHEAD]]
    for t, (hd, qi) in enumerate(tasks):
        r0 = qi * TQ
        c0 = hd * HEAD_DIM
        parts = queue.pop(0)
        if t + FOX_TASKS_AHEAD < len(tasks):
            queue.append(logits(*tasks[t + FOX_TASKS_AHEAD]))
        head_lane = lane == pl.program_id(1) * HEADS_PER_STEP + hd
        c_t = jnp.sum(jnp.where(head_lane, ccol_ref[0, r0:r0 + TQ, :], 0.0), axis=1, keepdims=True)
        o = _softmax_pv(parts, v_ref[0, 0:r0 + TQ, c0:c0 + HEAD_DIM], row_shift=c_t)
        z = z_ref[0, r0:r0 + TQ, c0:c0 + HEAD_DIM].astype(F32)
        o_ref[0, r0:r0 + TQ, c0:c0 + HEAD_DIM] = (o * _silu(z)).astype(BF16)


def _fox_attention(proj, cum_row, cum_col):
    b, s, _ = proj.shape
    crow = cum_row.reshape(b, FOX_HEADS, 1, s)
    mask = jnp.asarray(_causal_tile())
    return pl.pallas_call(
        functools.partial(_fox_kernel, seq=s),
        grid=(b, FOX_HEADS // HEADS_PER_STEP),
        in_specs=[
            pl.BlockSpec((TQ, TQ), lambda i, h: (0, 0)),
            pl.BlockSpec((1, HEADS_PER_STEP, 1, s), lambda i, h: (i, h, 0, 0)),
            pl.BlockSpec((1, s, HEAD_DIM), lambda i, h: (i, 0, 0)),
        ] + _head_specs(s, FOX_HEADS),
        out_specs=pl.BlockSpec((1, s, HEADS_PER_STEP * HEAD_DIM), lambda i, h: (i, 0, h)),
        out_shape=jax.ShapeDtypeStruct((b, s, D_INNER), BF16),
        compiler_params=pltpu.CompilerParams(
            dimension_semantics=("arbitrary", "arbitrary"),
            vmem_limit_bytes=VMEM_LIMIT),
        name="fox_attention",
    )(mask, crow, cum_col, proj, proj, proj, proj)


def _col_scale():
    cs = np.ones((1, 4 * D_INNER), np.float32)
    cs[0, :D_INNER] = QK_SCALE * LOG2E
    return jnp.asarray(cs)


def kernel(x, rel_bias, w_in_0, lam_q1_0, lam_k1_0, lam_q2_0, lam_k2_0, subln_g_0, w_out_0, ln_g_0, ln_b_0, w_in_1, w_out_1, ln_g_1, ln_b_1, w_in_2, b_f_2, w_out_2, ln_g_2, ln_b_2, w_in_3, lam_q1_3, lam_k1_3, lam_q2_3, lam_k2_3, subln_g_3, w_out_3, ln_g_3, ln_b_3):
    b, s, d = x.shape
    m = b * s
    e = D_INNER
    col_scale = _col_scale()
    bias = _bias_tiles(rel_bias)

    h = x.reshape(m, d)
    h_bf = h
    layers = [
        (w_in_0, w_out_0, ln_g_0, ln_b_0),
        (w_in_1, w_out_1, ln_g_1, ln_b_1),
        (w_in_2, w_out_2, ln_g_2, ln_b_2),
        (w_in_3, w_out_3, ln_g_3, ln_b_3),
    ]
    diff_extra = {
        0: (lam_q1_0, lam_k1_0, lam_q2_0, lam_k2_0, subln_g_0),
        3: (lam_q1_3, lam_k1_3, lam_q2_3, lam_k2_3, subln_g_3),
    }
    for i, (w_in, w_out, g, beta) in enumerate(layers):
        kind = i % 3
        if kind == 2:
            w_t = w_in.T
            proj, w_out_bf = _in_proj(h_bf, w_t, col_scale, 4 * e, w_out, w_is_nk=True)
        else:
            proj, w_out_bf = _in_proj(h_bf, w_in, col_scale, 4 * e, w_out)
        proj = proj.reshape(b, s, 4 * e)
        if kind == 0:
            lq1, lk1, lq2, lk2, sg = diff_extra[i]
            lam_init = 0.8 - 0.6 * math.exp(-0.3 * i)
            a = _diff_attention(proj, jnp.stack([lq1, lk1, lq2, lk2]), sg, bias, lam_init)
        elif kind == 1:
            a = _moba_attention(proj, bias)
        else:
            wf = jnp.pad(w_t[4 * e:].astype(BF16), ((0, HEAD_DIM - FOX_HEADS), (0, 0)))
            bf_row = jnp.pad(b_f_2, (0, HEAD_DIM - FOX_HEADS)).reshape(1, HEAD_DIM)
            cum_row, cum_col = _fox_gate(h_bf.reshape(b, s, d), wf, bf_row)
            a = _fox_attention(proj, cum_row, cum_col)
        outs = _out_proj_ln(a.reshape(m, e), w_out_bf, h, g, beta, emit_bf16=i + 1 < DEPTH)
        h, h_bf = outs[0], outs[-1]
    return h.reshape(b, s, d)
```
